```python
import math
import jax
import jax.numpy as jnp
from jax import lax
import numpy as np

D_MODEL = 2048
BATCH = 1
SEQ = 8192
DEPTH = 1

GRID_W = 64
CTX_LEN = 256
N_MOD = 9
EPS = 1e-6
CONV_W = 5
D_FF = 256 * ((8 * D_MODEL // 3 + 255) // 256)

ML_HEADS = 8
ML_DV = D_MODEL // ML_HEADS
ML_DQK = ML_DV // 2
ML_CHUNK = 64
ML_QK = ML_HEADS * ML_DQK
ML_V = ML_HEADS * ML_DV
ML_COLS = 2 * ML_QK + 2 * ML_V + 4 * ML_HEADS

SSM_DINNER = 2 * D_MODEL
SSM_HEADDIM = 64
SSM_HEADS = SSM_DINNER // SSM_HEADDIM
SSM_GROUPS = 8
SSM_HPG = SSM_HEADS // SSM_GROUPS
SSM_DSTATE = 128
SSM_CHUNK = 128
SSM_XBC = SSM_DINNER + 2 * SSM_GROUPS * SSM_DSTATE
SSM_COLS = SSM_DINNER + SSM_XBC + 2 * SSM_HEADS

IN_COLS = ML_COLS + SSM_COLS

kernel_name = 'hybrid_mlstm_mamba2_macaron_dit_block'


def rmsnorm(x, w):
    xf = x.astype(jnp.float32)
    y = xf * lax.rsqrt(jnp.mean(jnp.square(xf), axis=-1, keepdims=True) + EPS)
    return y.astype(x.dtype) * w


def modulate(x, w, shift, scale):
    return rmsnorm(x, w) * (1 + scale) + shift


def swiglu(h, w_gate, w_up, w_down):
    return (jax.nn.silu(h @ w_gate) * (h @ w_up)) @ w_down


def dwconv_centred(x, w, b):
    pad = CONV_W // 2
    y = lax.conv_general_dilated(x, w[:, None, :].astype(x.dtype), window_strides=(1,),
                                 padding=[(pad, pad)], dimension_numbers=('NWC', 'WIO', 'NWC'),
                                 feature_group_count=x.shape[-1])
    return y + b


def flip(t):
    return jnp.flip(t, axis=1)


def to_colmajor(t, rows):
    bsz, n, ch = t.shape
    return t.reshape(bsz, rows, GRID_W, ch).transpose(0, 2, 1, 3).reshape(bsz, n, ch)


def from_colmajor(t, rows):
    bsz, n, ch = t.shape
    return t.reshape(bsz, GRID_W, rows, ch).transpose(0, 2, 1, 3).reshape(bsz, n, ch)


def mlstm_zero_state(bsz):
    one = (jnp.zeros((bsz, ML_HEADS, ML_DV, ML_DQK), jnp.float32),
           jnp.zeros((bsz, ML_HEADS, ML_DQK), jnp.float32),
           jnp.zeros((bsz, ML_HEADS), jnp.float32))
    return (one, one)


def ssd_zero_state(bsz):
    one = jnp.zeros((bsz, SSM_GROUPS, SSM_HPG, SSM_HEADDIM, SSM_DSTATE), jnp.float32)
    return (one, one)


def mlstm_dir(q, k, v, ig, lf, state0, with_output):
    f32 = jnp.float32
    bsz, n_tok, nh, dk = q.shape
    dv = v.shape[-1]
    L = ML_CHUNK
    nc = n_tok // L
    qc = q.astype(f32).reshape(bsz, nc, L, nh, dk)
    kc = k.astype(f32).reshape(bsz, nc, L, nh, dk)
    vc = v.astype(f32).reshape(bsz, nc, L, nh, dv)
    igc = ig.astype(f32).reshape(bsz, nc, L, nh)
    b = jnp.cumsum(lf.astype(f32).reshape(bsz, nc, L, nh), axis=2)
    b_end = b[:, :, -1]
    a = b_end[:, :, None] + igc - b
    m_loc = jnp.max(a, axis=2)
    kw = kc * jnp.exp(a - m_loc[:, :, None])[..., None]
    c_loc = jnp.einsum('bclhv,bclhk->bchvk', vc, kw)
    n_loc = jnp.sum(kw, axis=2)

    def step(carry, xs):
        c_st, n_st, m_st = carry
        be, cl, nl, ml = xs
        m_new = jnp.maximum(be + m_st, ml)
        s_old = jnp.exp(be + m_st - m_new)
        s_new = jnp.exp(ml - m_new)
        c_new = s_old[..., None, None] * c_st + s_new[..., None, None] * cl
        n_new = s_old[..., None] * n_st + s_new[..., None] * nl
        return (c_new, n_new, m_new), (c_st, n_st, m_st)

    xs = (jnp.moveaxis(b_end, 1, 0), jnp.moveaxis(c_loc, 1, 0),
          jnp.moveaxis(n_loc, 1, 0), jnp.moveaxis(m_loc, 1, 0))
    init = (state0[0].astype(f32), state0[1].astype(f32), state0[2].astype(f32))
    final, starts = lax.scan(step, init, xs)
    if not with_output:
        return None, final
    c0 = jnp.moveaxis(starts[0], 0, 1)
    n0 = jnp.moveaxis(starts[1], 0, 1)
    m0 = jnp.moveaxis(starts[2], 0, 1)
    causal = jnp.tril(jnp.ones((L, L), dtype=bool))
    logw = b[:, :, :, None, :] - b[:, :, None, :, :] + igc[:, :, None, :, :]
    logw = jnp.where(causal[None, None, :, :, None], logw, -jnp.inf)
    m_inter = b + m0[:, :, None, :]
    m_out = jnp.maximum(m_inter, jnp.max(logw, axis=3))
    s = jnp.einsum('bcihk,bcjhk->bcijh', qc, kc) * jnp.exp(logw - m_out[:, :, :, None, :])
    s_inter = jnp.exp(m_inter - m_out)
    num = (jnp.einsum('bcijh,bcjhv->bcihv', s, vc)
           + s_inter[..., None] * jnp.einsum('bcihk,bchvk->bcihv', qc, c0))
    den = jnp.sum(s, axis=3) + s_inter * jnp.einsum('bcihk,bchk->bcih', qc, n0)
    h = num / jnp.maximum(jnp.abs(den), jnp.exp(-m_out))[..., None]
    return h.reshape(bsz, n_tok, nh, dv).astype(v.dtype), final


def ssd_dir(x, dt, A, bm, cm, state0, with_output):
    f32 = jnp.float32
    bsz, n_tok, ng, ne, hp = x.shape
    ns = bm.shape[-1]
    L = SSM_CHUNK
    nc = n_tok // L
    xc = x.astype(f32).reshape(bsz, nc, L, ng, ne, hp)
    dtc = dt.astype(f32).reshape(bsz, nc, L, ng, ne)
    bc = bm.astype(f32).reshape(bsz, nc, L, ng, ns)
    cc = cm.astype(f32).reshape(bsz, nc, L, ng, ns)
    b = jnp.cumsum(dtc * A.astype(f32), axis=2)
    b_end = b[:, :, -1]
    xdt = xc * dtc[..., None]
    s_loc = jnp.einsum('bclgep,bclgn->bcgepn',
                       xdt * jnp.exp(b_end[:, :, None] - b)[..., None], bc)

    def step(s_st, xs):
        be, sl = xs
        return jnp.exp(be)[..., None, None] * s_st + sl, s_st

    final, starts = lax.scan(step, state0.astype(f32),
                             (jnp.moveaxis(b_end, 1, 0), jnp.moveaxis(s_loc, 1, 0)))
    if not with_output:
        return None, final
    s0 = jnp.moveaxis(starts, 0, 1)
    causal = jnp.tril(jnp.ones((L, L), dtype=bool))
    seg = b[:, :, :, None] - b[:, :, None, :]
    decay = jnp.exp(jnp.where(causal[None, None, :, :, None, None], seg, -jnp.inf))
    cb = jnp.einsum('bcign,bcjgn->bcijg', cc, bc)
    y = jnp.einsum('bcijge,bcjgep->bcigep', cb[..., None] * decay, xdt)
    y = y + jnp.exp(b)[..., None] * jnp.einsum('bcign,bcgepn->bcigep', cc, s0)
    return y.reshape(bsz, n_tok, ng, ne, hp).astype(x.dtype), final


def mlstm_branch(u, init, conv_w, conv_b, gate_b, norm_w, w_proj, with_output):
    bsz, n_tok, _ = u.shape
    qk, v, o, g = jnp.split(u, [2 * ML_QK, 2 * ML_QK + ML_V, 2 * ML_QK + 2 * ML_V], axis=-1)
    qk = jax.nn.silu(dwconv_centred(qk, conv_w, conv_b))
    q, k = jnp.split(qk, 2, axis=-1)
    q = q.reshape(bsz, n_tok, ML_HEADS, ML_DQK) * (ML_DQK ** -0.5)
    k = k.reshape(bsz, n_tok, ML_HEADS, ML_DQK)
    v = v.reshape(bsz, n_tok, ML_HEADS, ML_DV)
    g = g.astype(jnp.float32).reshape(bsz, n_tok, 4, ML_HEADS) + gate_b.astype(jnp.float32)
    ig_f, lf_f = g[:, :, 0], jax.nn.log_sigmoid(g[:, :, 1])
    ig_b, lf_b = g[:, :, 2], jax.nn.log_sigmoid(g[:, :, 3])
    h_f, st_f = mlstm_dir(q, k, v, ig_f, lf_f, init[0], with_output)
    h_b, st_b = mlstm_dir(flip(q), flip(k), flip(v), flip(ig_b), flip(lf_b), init[1], with_output)
    if not with_output:
        return None, (st_f, st_b)
    h = h_f + flip(h_b)
    h = rmsnorm(h, norm_w.reshape(ML_HEADS, ML_DV)).reshape(bsz, n_tok, ML_V)
    y = (jax.nn.sigmoid(o) * h) @ w_proj
    return y, (st_f, st_b)


def ssm_branch(u, init, conv_w, conv_b, dt_bias, a_log, d_skip, norm_w, w_proj, with_output):
    bsz, n_tok, _ = u.shape
    z, xbc, dt = jnp.split(u, [SSM_DINNER, SSM_DINNER + SSM_XBC], axis=-1)
    xbc = jax.nn.silu(dwconv_centred(xbc, conv_w, conv_b))
    xs, bm, cm = jnp.split(xbc, [SSM_DINNER, SSM_DINNER + SSM_GROUPS * SSM_DSTATE], axis=-1)
    xs = xs.reshape(bsz, n_tok, SSM_GROUPS, SSM_HPG, SSM_HEADDIM)
    bm = bm.reshape(bsz, n_tok, SSM_GROUPS, SSM_DSTATE)
    cm = cm.reshape(bsz, n_tok, SSM_GROUPS, SSM_DSTATE)
    dt = jax.nn.softplus(dt.astype(jnp.float32).reshape(bsz, n_tok, 2, SSM_GROUPS, SSM_HPG)
                         + dt_bias.astype(jnp.float32).reshape(2, SSM_GROUPS, SSM_HPG))
    A = -jnp.exp(a_log.astype(jnp.float32)).reshape(2, SSM_GROUPS, SSM_HPG)
    y_f, st_f = ssd_dir(xs, dt[:, :, 0], A[0], bm, cm, init[0], with_output)
    y_b, st_b = ssd_dir(flip(xs), flip(dt[:, :, 1]), A[1], flip(bm), flip(cm), init[1], with_output)
    if not with_output:
        return None, (st_f, st_b)
    y = y_f + flip(y_b) + d_skip.reshape(SSM_GROUPS, SSM_HPG)[..., None] * xs
    y = y.reshape(bsz, n_tok, SSM_DINNER) * jax.nn.silu(z)
    y = rmsnorm(y.reshape(bsz, n_tok, SSM_GROUPS, SSM_DINNER // SSM_GROUPS),
                norm_w.reshape(SSM_GROUPS, SSM_DINNER // SSM_GROUPS)).reshape(bsz, n_tok, SSM_DINNER)
    return y @ w_proj, (st_f, st_b)


def hybrid_layer(x, ctx, mod, mod_c, last, norm_w, ffn_w_gate, ffn_w_up, ffn_w_down, w_in,
                 ml_conv_w, ml_conv_b, ml_gate_b, ml_norm_w, w_proj_ml,
                 ssm_conv_w, ssm_conv_b, ssm_dt_bias, ssm_a_log, ssm_d, ssm_norm_w, w_proj_ssm,
                 w_gate, b_gate, w_out):
    m = jnp.split(mod, N_MOD, axis=-1)
    mc = jnp.split(mod_c, N_MOD, axis=-1)
    bsz, n_tok, _ = x.shape
    rows = n_tok // GRID_W

    def half_ffn(s, mm, sub, j):
        h = modulate(s, norm_w[sub], mm[3 * sub], mm[3 * sub + 1])
        return s + 0.5 * mm[3 * sub + 2] * swiglu(h, ffn_w_gate[j], ffn_w_up[j], ffn_w_down[j])

    def merge(h, y_ml, y_ssm):
        gml, gssm = jnp.split(jax.nn.sigmoid(h @ w_gate + b_gate), 2, axis=-1)
        return (gml * y_ml + gssm * y_ssm) @ w_out

    ml_args = (ml_conv_w, ml_conv_b, ml_gate_b, ml_norm_w, w_proj_ml)
    ssm_args = (ssm_conv_w, ssm_conv_b, ssm_dt_bias, ssm_a_log, ssm_d, ssm_norm_w, w_proj_ssm)

    x = half_ffn(x, m, 0, 0)
    ctx = half_ffn(ctx, mc, 0, 0)

    h = modulate(x, norm_w[1], m[3], m[4])
    hc = modulate(ctx, norm_w[1], mc[3], mc[4])
    u = h @ w_in
    uc = hc @ w_in
    yc_ml, st_ml = mlstm_branch(uc[..., :ML_COLS], mlstm_zero_state(bsz), *ml_args, not last)
    yc_ssm, st_ssm = ssm_branch(uc[..., ML_COLS:], ssd_zero_state(bsz), *ssm_args, not last)
    y_ml, _ = mlstm_branch(to_colmajor(u[..., :ML_COLS], rows), st_ml, *ml_args, True)
    y_ml = from_colmajor(y_ml, rows)
    y_ssm, _ = ssm_branch(u[..., ML_COLS:], st_ssm, *ssm_args, True)
    x = x + m[5] * merge(h, y_ml, y_ssm)

    x = half_ffn(x, m, 2, 1)
    if not last:
        ctx = ctx + mc[5] * merge(hc, yc_ml, yc_ssm)
        ctx = half_ffn(ctx, mc, 2, 1)
    return x, ctx


def setup_inputs(seed: int = 0) -> dict:
    key = jax.random.key(seed)
    ks = jax.random.split(key, 32)
    f32 = jnp.float32
    D = D_MODEL

    def nrm(k, shape, scale=1.0):
        return scale * jax.random.normal(k, shape, f32)

    def dense(k, shape, fan_in, gain=1.0):
        return nrm(k, shape, gain * fan_in ** -0.5)

    x = nrm(ks[0], (BATCH, SEQ, D))
    c = nrm(ks[1], (BATCH, D))
    ctx = nrm(ks[2], (BATCH, CTX_LEN, D))
    c_ctx = nrm(ks[3], (D,))
    w_ada = dense(ks[4], (DEPTH, D, N_MOD * D), D, 0.3)
    b_ada = nrm(ks[5], (DEPTH, N_MOD * D), 0.02)
    norm_w = 1.0 + nrm(ks[6], (DEPTH, 3, D), 0.05)
    ffn_w_gate = dense(ks[7], (DEPTH, 2, D, D_FF), D)
    ffn_w_up = dense(ks[8], (DEPTH, 2, D, D_FF), D)
    ffn_w_down = dense(ks[9], (DEPTH, 2, D_FF, D), D_FF)
    w_in = dense(ks[10], (DEPTH, D, IN_COLS), D)
    ml_conv_w = dense(ks[11], (DEPTH, CONV_W, 2 * ML_QK), CONV_W)
    ml_conv_b = nrm(ks[12], (DEPTH, 2 * ML_QK), 0.02)
    ig_b = nrm(ks[13], (DEPTH, 2, ML_HEADS), 0.1)
    fg_b = 3.0 + 3.0 * jax.random.uniform(ks[14], (DEPTH, 2, ML_HEADS), f32)
    ml_gate_b = jnp.stack([ig_b[:, 0], fg_b[:, 0], ig_b[:, 1], fg_b[:, 1]], axis=1)
    ml_norm_w = 1.0 + nrm(ks[15], (DEPTH, ML_V), 0.05)
    w_proj_ml = dense(ks[16], (DEPTH, ML_V, D), ML_V)
    ssm_conv_w = dense(ks[17], (DEPTH, CONV_W, SSM_XBC), CONV_W)
    ssm_conv_b = nrm(ks[18], (DEPTH, SSM_XBC), 0.02)
    dt0 = jnp.exp(jax.random.uniform(ks[19], (DEPTH, 2, SSM_HEADS), f32,
                                     math.log(1e-3), math.log(1e-1)))
    ssm_dt_bias = dt0 + jnp.log(-jnp.expm1(-dt0))
    ssm_a_log = jnp.log(jax.random.uniform(ks[20], (DEPTH, 2, SSM_HEADS), f32, 1.0, 16.0))
    ssm_d = 1.0 + nrm(ks[21], (DEPTH, SSM_HEADS), 0.1)
    ssm_norm_w = 1.0 + nrm(ks[22], (DEPTH, SSM_DINNER), 0.05)
    w_proj_ssm = dense(ks[23], (DEPTH, SSM_DINNER, D), SSM_DINNER)
    w_gate = dense(ks[24], (DEPTH, D, 2 * D), D)
    b_gate = nrm(ks[25], (DEPTH, 2 * D), 0.1)
    w_out = dense(ks[26], (DEPTH, D, D), D)
    final_norm_w = 1.0 + nrm(ks[27], (D,), 0.05)
    return {'x': x, 'c': c, 'ctx': ctx, 'c_ctx': c_ctx, 'w_ada': w_ada, 'b_ada': b_ada,
            'norm_w': norm_w, 'ffn_w_gate': ffn_w_gate, 'ffn_w_up': ffn_w_up,
            'ffn_w_down': ffn_w_down, 'w_in': w_in, 'ml_conv_w': ml_conv_w,
            'ml_conv_b': ml_conv_b, 'ml_gate_b': ml_gate_b, 'ml_norm_w': ml_norm_w,
            'w_proj_ml': w_proj_ml, 'ssm_conv_w': ssm_conv_w, 'ssm_conv_b': ssm_conv_b,
            'ssm_dt_bias': ssm_dt_bias, 'ssm_a_log': ssm_a_log, 'ssm_d': ssm_d,
            'ssm_norm_w': ssm_norm_w, 'w_proj_ssm': w_proj_ssm, 'w_gate': w_gate,
            'b_gate': b_gate, 'w_out': w_out, 'final_norm_w': final_norm_w}


def reference(x, c, ctx, c_ctx, w_ada, b_ada, norm_w, ffn_w_gate, ffn_w_up, ffn_w_down, w_in,
              ml_conv_w, ml_conv_b, ml_gate_b, ml_norm_w, w_proj_ml, ssm_conv_w, ssm_conv_b,
              ssm_dt_bias, ssm_a_log, ssm_d, ssm_norm_w, w_proj_ssm, w_gate, b_gate, w_out,
              final_norm_w):
    for l in range(DEPTH):
        last = l == DEPTH - 1
        mod = (jax.nn.silu(c) @ w_ada[l] + b_ada[l])[:, None, :]
        mod_c = jax.nn.silu(c_ctx) @ w_ada[l] + b_ada[l]
        x, ctx = hybrid_layer(x, ctx, mod, mod_c, last, norm_w[l], ffn_w_gate[l], ffn_w_up[l],
                              ffn_w_down[l], w_in[l], ml_conv_w[l], ml_conv_b[l], ml_gate_b[l],
                              ml_norm_w[l], w_proj_ml[l], ssm_conv_w[l], ssm_conv_b[l],
                              ssm_dt_bias[l], ssm_a_log[l], ssm_d[l], ssm_norm_w[l],
                              w_proj_ssm[l], w_gate[l], b_gate[l], w_out[l])
    return rmsnorm(x, final_norm_w)
```

```python
import functools
import math

import jax
import jax.numpy as jnp
from jax import lax
from jax.experimental import pallas as pl
from jax.experimental.pallas import tpu as pltpu

F32 = jnp.float32
BF16 = jnp.bfloat16

GRID_W = 64
N_MOD = 9
EPS = 1e-6
CONV_W = 5
CONV_PAD = CONV_W // 2
HALO_ROWS = 16
ML_HEADS = 8
ML_L = 256
SSM_HEADDIM = 64
SSM_GROUPS = 8
SSM_DSTATE = 128
SSM_L = 128
VMEM_LIMIT_BYTES = 56 * 1024 * 1024
NEG_INF = float("-inf")


def _cparams(sem):
    return pltpu.CompilerParams(dimension_semantics=sem, vmem_limit_bytes=VMEM_LIMIT_BYTES)


def _pick(n, cands):
    for c in cands:
        if n % c == 0:
            return c
    raise ValueError(f"no tile for {n} in {cands}")


def _silu(x):
    return x * (1.0 / (1.0 + jnp.exp(-x)))


def _sigmoid(x):
    return 1.0 / (1.0 + jnp.exp(-x))


def _log_sigmoid(x):
    return jnp.minimum(x, 0.0) - jnp.log1p(jnp.exp(-jnp.abs(x)))


def _softplus(x):
    return jnp.maximum(x, 0.0) + jnp.log1p(jnp.exp(-jnp.abs(x)))


def _rms(x):
    return x * lax.rsqrt(jnp.mean(jnp.square(x), axis=-1, keepdims=True) + EPS)


def _dot(a, b):
    return jnp.dot(a, b, preferred_element_type=F32)


def _dot_nt(a, b):
    return lax.dot_general(a, b, (((1,), (1,)), ((), ())), preferred_element_type=F32)


def _dot_tn(a, b):
    return lax.dot_general(a, b, (((0,), (0,)), ((), ())), preferred_element_type=F32)


def _dot_exact(a, b):
    return jnp.dot(a, b, preferred_element_type=F32, precision=lax.Precision.HIGHEST)


def _mod_kernel(c_ref, w_ref, b_ref, o_ref):
    a = _silu(c_ref[...]).astype(BF16)
    o_ref[...] = _dot(a, w_ref[...].astype(BF16)) + b_ref[...]


def _mod_call(cc, w_ada, b_ada):
    k, n = w_ada.shape
    tn = _pick(n, (1024, 512, 256, 128))
    return pl.pallas_call(
        _mod_kernel,
        grid=(n // tn,),
        in_specs=[pl.BlockSpec((8, k), lambda j: (0, 0)),
                  pl.BlockSpec((k, tn), lambda j: (0, j)),
                  pl.BlockSpec((1, tn), lambda j: (0, j))],
        out_specs=pl.BlockSpec((8, tn), lambda j: (0, j)),
        out_shape=jax.ShapeDtypeStruct((8, n), F32),
        compiler_params=_cparams(("arbitrary",)),
    )(cc, w_ada, b_ada.reshape(1, n))


def _ffn_kernel(x_ref, vec_ref, wg_ref, wu_ref, wd_ref, *rest, n_lat, final):
    if final:
        o_ref, h_s, acc_s = rest
    else:
        x1_ref, h2_ref, h_s, acc_s = rest
    i = pl.program_id(0)
    f = pl.program_id(1)
    tm = x_ref.shape[0]
    is_ctx = (i * tm + lax.broadcasted_iota(jnp.int32, (tm, 1), 0)) >= n_lat

    def vec(lat, ctx):
        return jnp.where(is_ctx, vec_ref[ctx:ctx + 1, :], vec_ref[lat:lat + 1, :])

    @pl.when(f == 0)
    def _():
        x = x_ref[...]
        h = _rms(x) * vec_ref[0:1, :] * (1.0 + vec(2, 5)) + vec(1, 4)
        h_s[...] = h.astype(BF16)
        acc_s[...] = jnp.zeros_like(acc_s)

    h = h_s[...]
    a = _silu(_dot(h, wg_ref[...])) * _dot(h, wu_ref[...])
    acc_s[...] += _dot(a.astype(BF16), wd_ref[...])

    @pl.when(f == pl.num_programs(1) - 1)
    def _():
        x1 = x_ref[...] + 0.5 * vec(3, 6) * acc_s[...]
        if final:
            o_ref[...] = _rms(x1) * vec_ref[7:8, :]
        else:
            x1_ref[...] = x1
            h2 = _rms(x1) * vec_ref[7:8, :] * (1.0 + vec(9, 11)) + vec(8, 10)
            h2_ref[...] = h2.astype(BF16)


def _ffn_call(x, vecs, wg, wu, wd, n_lat, final):
    m, d = x.shape
    dff = wg.shape[1]
    tm = _pick(m, (768, 512, 256))
    tf = _pick(dff, (512, 256, 128))
    row = pl.BlockSpec((tm, d), lambda i, f: (i, 0), pipeline_mode=pl.Buffered(1))
    if final:
        out_shape = jax.ShapeDtypeStruct((m, d), F32)
        out_specs = row
    else:
        out_shape = (jax.ShapeDtypeStruct((m, d), F32), jax.ShapeDtypeStruct((m, d), BF16))
        out_specs = (row, row)
    return pl.pallas_call(
        functools.partial(_ffn_kernel, n_lat=n_lat, final=final),
        grid=(m // tm, dff // tf),
        in_specs=[row,
                  pl.BlockSpec(vecs.shape, lambda i, f: (0, 0)),
                  pl.BlockSpec((d, tf), lambda i, f: (0, f)),
                  pl.BlockSpec((d, tf), lambda i, f: (0, f)),
                  pl.BlockSpec((tf, d), lambda i, f: (f, 0))],
        out_specs=out_specs,
        out_shape=out_shape,
        scratch_shapes=[pltpu.VMEM((tm, d), BF16), pltpu.VMEM((tm, d), F32)],
        compiler_params=_cparams(("arbitrary", "arbitrary")),
    )(x, vecs, wg, wu, wd)


def _mm_kernel(a_ref, w_ref, o_ref):
    o_ref[...] = _dot(a_ref[...], w_ref[...]).astype(o_ref.dtype)


def _mm_call(a, w, out_dtype):
    m, k = a.shape
    n = w.shape[1]
    tm = _pick(m, (768, 512, 256))
    tn = _pick(n, (512, 256, 128))
    return pl.pallas_call(
        _mm_kernel,
        grid=(m // tm, n // tn),
        in_specs=[pl.BlockSpec((tm, k), lambda i, j: (i, 0)),
                  pl.BlockSpec((k, tn), lambda i, j: (0, j))],
        out_specs=pl.BlockSpec((tm, tn), lambda i, j: (i, j)),
        out_shape=jax.ShapeDtypeStruct((m, n), out_dtype),
        compiler_params=_cparams(("arbitrary", "arbitrary")),
    )(a, w)


def _chunk_order(c, n_lat_chunks, n_ctx_chunks, reverse):
    if reverse:
        return n_lat_chunks + n_ctx_chunks - 1 - c
    return jnp.where(c < n_ctx_chunks, n_lat_chunks + c, c - n_ctx_chunks)


def _make_halo(u, chunk, n_lat):
    rows, ch = u.shape
    nc = rows // chunk
    ur = u.reshape(nc, chunk, ch)
    zero = jnp.zeros((1, CONV_PAD, ch), u.dtype)
    prev = jnp.concatenate([zero, ur[:-1, chunk - CONV_PAD:, :]], axis=0)
    nxt = jnp.concatenate([ur[1:, :CONV_PAD, :], zero], axis=0)
    cb = n_lat // chunk
    cidx = jnp.arange(nc)[:, None, None]
    prev = jnp.where(cidx == cb, jnp.zeros_like(prev), prev)
    nxt = jnp.where(cidx == cb - 1, jnp.zeros_like(nxt), nxt)
    pad_a = jnp.zeros((nc, HALO_ROWS // 2 - CONV_PAD, ch), u.dtype)
    return jnp.concatenate([pad_a, prev, nxt, pad_a], axis=1)


def _conv_silu(xs_ref, w_ref, b_ref, chunk):
    acc = b_ref[...] + w_ref[0:1, :] * xs_ref[pl.ds(8 - CONV_PAD, chunk), :]
    for j in range(1, CONV_W):
        acc = acc + w_ref[j:j + 1, :] * xs_ref[pl.ds(8 - CONV_PAD + j, chunk), :]
    return _silu(acc)


def _tri(chunk, reverse):
    r = lax.broadcasted_iota(jnp.int32, (chunk, chunk), 0)
    c = lax.broadcasted_iota(jnp.int32, (chunk, chunk), 1)
    return (c >= r) if reverse else (c <= r)


def _mlstm_kernel(q_ref, k_ref, v_ref, halo_ref, g_ref, cw_ref, cb_ref, gb_ref, o_ref,
                  xs, qk_s, ct_s, n_s, m_s, *, reverse):
    L = q_ref.shape[0]
    nqk = q_ref.shape[1]
    dk = nqk // ML_HEADS
    dv = v_ref.shape[1] // ML_HEADS
    c = pl.program_id(0)

    @pl.when(c == 0)
    def _():
        ct_s[...] = jnp.zeros_like(ct_s)
        n_s[...] = jnp.zeros_like(n_s)
        m_s[...] = jnp.zeros_like(m_s)

    halo = halo_ref[0].astype(F32)
    xs[0:8, :] = halo[0:8]
    xs[8:8 + L, 0:nqk] = q_ref[...].astype(F32)
    xs[8:8 + L, nqk:2 * nqk] = k_ref[...].astype(F32)
    xs[8 + L:16 + L, :] = halo[8:16]
    qk = _conv_silu(xs, cw_ref, cb_ref, L)
    qk_s[:, 0:nqk] = (qk[:, 0:nqk] * (dk ** -0.5)).astype(BF16)
    qk_s[:, nqk:2 * nqk] = qk[:, nqk:2 * nqk].astype(BF16)

    mask = _tri(L, reverse)
    g = g_ref[...] + gb_ref[...]
    bc = _dot_exact(mask.astype(F32), _log_sigmoid(g))
    g_t = g.T
    bc_t = bc.T
    end = 0 if reverse else L - 1
    base = 2 * ML_HEADS if reverse else 0

    for h in range(ML_HEADS):
        io = base + h
        fo = base + ML_HEADS + h
        b_col = bc[:, fo:fo + 1]
        b_row = bc_t[fo:fo + 1, :]
        ig_col = g[:, io:io + 1]
        ig_row = g_t[io:io + 1, :]
        b_end = bc[end:end + 1, fo:fo + 1]
        m0 = m_s[h:h + 1, 0:1]
        n0 = n_s[h:h + 1, :]
        ct0 = ct_s[h]
        q = qk_s[:, h * dk:(h + 1) * dk]
        k = qk_s[:, nqk + h * dk:nqk + (h + 1) * dk]
        v = v_ref[:, h * dv:(h + 1) * dv]

        logw = jnp.where(mask, b_col - b_row + ig_row, NEG_INF)
        m_inter = b_col + m0
        m_out = jnp.maximum(m_inter, jnp.max(logw, axis=1, keepdims=True))
        p = _dot_nt(q, k) * jnp.exp(logw - m_out)
        s_inter = jnp.exp(m_inter - m_out)
        num = _dot(p.astype(BF16), v) + s_inter * _dot(q, ct0.astype(BF16))
        den = (jnp.sum(p, axis=1, keepdims=True)
               + s_inter * jnp.sum(q.astype(F32) * n0, axis=1, keepdims=True))
        o_ref[:, h * dv:(h + 1) * dv] = num / jnp.maximum(jnp.abs(den), jnp.exp(-m_out))

        a_col = b_end + ig_col - b_col
        m_loc = jnp.max(a_col, axis=0, keepdims=True)
        kw = k.astype(F32) * jnp.exp(a_col - m_loc)
        c_loc = _dot_tn(kw.astype(BF16), v)
        n_loc = jnp.sum(kw, axis=0, keepdims=True)
        m_new = jnp.maximum(b_end + m0, m_loc)
        s_old = jnp.exp(b_end + m0 - m_new)
        s_new = jnp.exp(m_loc - m_new)
        ct_s[h] = s_old * ct0 + s_new * c_loc
        n_s[h:h + 1, :] = s_old * n0 + s_new * n_loc
        m_s[h:h + 1, :] = jnp.broadcast_to(m_new, (1, m_s.shape[1]))


def _mlstm_call(u_ml, halo, gates, conv_w, conv_b, gate_b, n_lat, reverse):
    rows = u_ml.shape[0]
    L = ML_L
    nqk = halo.shape[2] // 2
    nv = 2 * nqk
    dk = nqk // ML_HEADS
    dv = nv // ML_HEADS
    nlc = n_lat // L
    ncc = (rows - n_lat) // L
    order = functools.partial(_chunk_order, n_lat_chunks=nlc, n_ctx_chunks=ncc, reverse=reverse)
    return pl.pallas_call(
        functools.partial(_mlstm_kernel, reverse=reverse),
        grid=(nlc + ncc,),
        in_specs=[pl.BlockSpec((L, nqk), lambda c: (order(c), 0)),
                  pl.BlockSpec((L, nqk), lambda c: (order(c), 1)),
                  pl.BlockSpec((L, nv), lambda c: (order(c), 1)),
                  pl.BlockSpec((1, HALO_ROWS, 2 * nqk), lambda c: (order(c), 0, 0)),
                  pl.BlockSpec((L, 128), lambda c: (order(c), 0)),
                  pl.BlockSpec((8, 2 * nqk), lambda c: (0, 0)),
                  pl.BlockSpec((1, 2 * nqk), lambda c: (0, 0)),
                  pl.BlockSpec((1, 128), lambda c: (0, 0))],
        out_specs=pl.BlockSpec((L, nv), lambda c: (order(c), 0)),
        out_shape=jax.ShapeDtypeStruct((rows, nv), F32),
        scratch_shapes=[pltpu.VMEM((L + 16, 2 * nqk), F32),
                        pltpu.VMEM((L, 2 * nqk), BF16),
                        pltpu.VMEM((ML_HEADS, dk, dv), F32),
                        pltpu.VMEM((ML_HEADS, dk), F32),
                        pltpu.VMEM((ML_HEADS, 128), F32)],
        compiler_params=_cparams(("arbitrary",)),
    )(u_ml, u_ml, u_ml, halo, gates, conv_w, conv_b, gate_b)


def _ssd_kernel(x_ref, bm_ref, cm_ref, halo_ref, dt_ref, cw_ref, cb_ref, dtb_ref, a_ref, dsk_ref, y_ref,
                xs, xc, st_s, *, reverse):
    L = x_ref.shape[0]
    nx = x_ref.shape[1]
    nb = bm_ref.shape[1]
    c = pl.program_id(0)

    @pl.when(c == 0)
    def _():
        st_s[...] = jnp.zeros_like(st_s)

    halo = halo_ref[0].astype(F32)
    xs[0:8, :] = halo[0:8]
    xs[8:8 + L, 0:nx] = x_ref[...].astype(F32)
    xs[8:8 + L, nx:nx + nb] = bm_ref[...].astype(F32)
    xs[8:8 + L, nx + nb:nx + 2 * nb] = cm_ref[...].astype(F32)
    xs[8 + L:16 + L, :] = halo[8:16]
    xc[...] = _conv_silu(xs, cw_ref, cb_ref, L)

    mask = _tri(L, reverse)
    dtv = _softplus(dt_ref[...] + dtb_ref[...])
    bc = _dot_exact(mask.astype(F32), dtv * a_ref[...])
    bc_t = bc.T
    dt_t = dtv.T
    end = 0 if reverse else L - 1
    nheads = nx // SSM_HEADDIM
    hpg = nheads // SSM_GROUPS
    base = nheads if reverse else 0
    lo = lax.broadcasted_iota(jnp.int32, (1, 2 * SSM_HEADDIM), 1) < SSM_HEADDIM

    for g in range(SSM_GROUPS):
        bm = xc[:, nx + g * SSM_DSTATE:nx + (g + 1) * SSM_DSTATE].astype(BF16)
        cm = xc[:, nx + nb + g * SSM_DSTATE:nx + nb + (g + 1) * SSM_DSTATE].astype(BF16)
        cb = _dot_nt(cm, bm)
        for pr in range(hpg // 2):
            col = (g * hpg + 2 * pr) * SSM_HEADDIM
            xp = xc[:, col:col + 2 * SSM_HEADDIM]
            ms, b_cols, w_cols, decs = [], [], [], []
            for e in range(2):
                co = base + g * hpg + 2 * pr + e
                b_col = bc[:, co:co + 1]
                b_row = bc_t[co:co + 1, :]
                b_end = bc[end:end + 1, co:co + 1]
                seg = jnp.where(mask, b_col - b_row, NEG_INF)
                ms.append((cb * jnp.exp(seg) * dt_t[co:co + 1, :]).astype(BF16))
                b_cols.append(b_col)
                w_cols.append(dtv[:, co:co + 1] * jnp.exp(b_end - b_col))
                decs.append(jnp.exp(b_end))
            x_lo = jnp.where(lo, xp, 0.0).astype(BF16)
            x_hi = jnp.where(lo, 0.0, xp).astype(BF16)
            s0 = st_s[:, col:col + 2 * SSM_HEADDIM]
            y = _dot(ms[0], x_lo) + _dot(ms[1], x_hi)
            y = y + jnp.where(lo, jnp.exp(b_cols[0]), jnp.exp(b_cols[1])) * _dot(cm, s0.astype(BF16))
            if not reverse:
                y = y + dsk_ref[:, col:col + 2 * SSM_HEADDIM] * xp
            y_ref[:, col:col + 2 * SSM_HEADDIM] = y.astype(y_ref.dtype)
            xw = (xp * jnp.where(lo, w_cols[0], w_cols[1])).astype(BF16)
            st_s[:, col:col + 2 * SSM_HEADDIM] = (s0 * jnp.where(lo, decs[0], decs[1]) + _dot_tn(bm, xw))


def _ssd_call(u_ssm, halo, dt, conv_w, conv_b, dt_bias, a_neg, d_skip, n_lat, reverse):
    rows = u_ssm.shape[0]
    L = SSM_L
    nb = SSM_GROUPS * SSM_DSTATE
    nx = halo.shape[2] - 2 * nb
    nlc = n_lat // L
    ncc = (rows - n_lat) // L
    order = functools.partial(_chunk_order, n_lat_chunks=nlc, n_ctx_chunks=ncc, reverse=reverse)
    xblk = nx // nb
    return pl.pallas_call(
        functools.partial(_ssd_kernel, reverse=reverse),
        grid=(nlc + ncc,),
        in_specs=[pl.BlockSpec((L, nx), lambda c: (order(c), 1)),
                  pl.BlockSpec((L, nb), lambda c: (order(c), 2 * xblk)),
                  pl.BlockSpec((L, nb), lambda c: (order(c), 2 * xblk + 1)),
                  pl.BlockSpec((1, HALO_ROWS, nx + 2 * nb), lambda c: (order(c), 0, 0)),
                  pl.BlockSpec((L, 128), lambda c: (order(c), 0)),
                  pl.BlockSpec((8, nx + 2 * nb), lambda c: (0, 0)),
                  pl.BlockSpec((1, nx + 2 * nb), lambda c: (0, 0)),
                  pl.BlockSpec((1, 128), lambda c: (0, 0)),
                  pl.BlockSpec((1, 128), lambda c: (0, 0)),
                  pl.BlockSpec((1, nx), lambda c: (0, 0))],
        out_specs=pl.BlockSpec((L, nx), lambda c: (order(c), 0)),
        out_shape=jax.ShapeDtypeStruct((rows, nx), F32),
        scratch_shapes=[pltpu.VMEM((L + 16, nx + 2 * nb), F32),
                        pltpu.VMEM((L, nx + 2 * nb), F32),
                        pltpu.VMEM((SSM_DSTATE, nx), F32)],
        compiler_params=_cparams(("arbitrary",)),
    )(u_ssm, u_ssm, u_ssm, halo, dt, conv_w, conv_b, dt_bias, a_neg, d_skip)


def _mlproj_kernel(hf_ref, hb_ref, o_ref, nw_ref, w_ref, y_ref, a_s):
    @pl.when(pl.program_id(1) == 0)
    def _():
        dv = a_s.shape[1] // ML_HEADS
        for h in range(ML_HEADS):
            sl = slice(h * dv, (h + 1) * dv)
            hh = _rms(hf_ref[:, sl] + hb_ref[:, sl]) * nw_ref[:, sl]
            a_s[:, sl] = (_sigmoid(o_ref[:, sl].astype(F32)) * hh).astype(BF16)
    y_ref[...] = _dot(a_s[...], w_ref[...]).astype(y_ref.dtype)


def _mlproj_call(hf, hb, u_ml, norm_w, w, n_lat):
    nv = hf.shape[1]
    n = w.shape[1]
    tm = _pick(n_lat, (512, 256))
    tn = _pick(n, (512, 256, 128))
    oblk = u_ml.shape[1] // nv - 1
    row = pl.BlockSpec((tm, nv), lambda i, j: (i, 0))
    return pl.pallas_call(
        _mlproj_kernel,
        grid=(n_lat // tm, n // tn),
        in_specs=[row, row,
                  pl.BlockSpec((tm, nv), lambda i, j: (i, oblk)),
                  pl.BlockSpec((1, nv), lambda i, j: (0, 0)),
                  pl.BlockSpec((nv, tn), lambda i, j: (0, j))],
        out_specs=pl.BlockSpec((tm, tn), lambda i, j: (i, j)),
        out_shape=jax.ShapeDtypeStruct((n_lat, n), BF16),
        scratch_shapes=[pltpu.VMEM((tm, nv), BF16)],
        compiler_params=_cparams(("arbitrary", "arbitrary")),
    )(hf, hb, u_ml, norm_w, w)


def _ssmproj_kernel(yf_ref, yb_ref, z_ref, nw_ref, w_ref, y_ref, a_s):
    @pl.when(pl.program_id(1) == 0)
    def _():
        gw = a_s.shape[1] // SSM_GROUPS
        for g in range(SSM_GROUPS):
            sl = slice(g * gw, (g + 1) * gw)
            y = (yf_ref[:, sl] + yb_ref[:, sl]) * _silu(z_ref[:, sl].astype(F32))
            a_s[:, sl] = (_rms(y) * nw_ref[:, sl]).astype(BF16)
    y_ref[...] = _dot(a_s[...], w_ref[...]).astype(y_ref.dtype)


def _ssmproj_call(yf, yb, u_ssm, norm_w, w, n_lat):
    nx = yf.shape[1]
    n = w.shape[1]
    tm = _pick(n_lat, (256,))
    tn = _pick(n, (512, 256, 128))
    row = pl.BlockSpec((tm, nx), lambda i, j: (i, 0))
    return pl.pallas_call(
        _ssmproj_kernel,
        grid=(n_lat // tm, n // tn),
        in_specs=[row, row, row,
                  pl.BlockSpec((1, nx), lambda i, j: (0, 0)),
                  pl.BlockSpec((nx, tn), lambda i, j: (0, j))],
        out_specs=pl.BlockSpec((tm, tn), lambda i, j: (i, j)),
        out_shape=jax.ShapeDtypeStruct((n_lat, n), BF16),
        scratch_shapes=[pltpu.VMEM((tm, nx), BF16)],
        compiler_params=_cparams(("arbitrary", "arbitrary")),
    )(yf, yb, u_ssm, norm_w, w)


def _gate_kernel(h_ref, w1_ref, w2_ref, b1_ref, b2_ref, yml_ref, yssm_ref, o_ref):
    h = h_ref[...]
    g1 = _sigmoid(_dot(h, w1_ref[...]) + b1_ref[...])
    g2 = _sigmoid(_dot(h, w2_ref[...]) + b2_ref[...])
    o_ref[...] = (g1 * yml_ref[...].astype(F32) + g2 * yssm_ref[...].astype(F32)).astype(o_ref.dtype)


def _gate_call(h, w_gate, b_gate, y_ml, y_ssm, n_lat):
    d = h.shape[1]
    tm = _pick(n_lat, (512, 256))
    tn = _pick(d, (512, 256, 128))
    nj = d // tn
    tile = pl.BlockSpec((tm, tn), lambda i, j: (i, j))
    return pl.pallas_call(
        _gate_kernel,
        grid=(n_lat // tm, nj),
        in_specs=[pl.BlockSpec((tm, d), lambda i, j: (i, 0)),
                  pl.BlockSpec((d, tn), lambda i, j: (0, j)),
                  pl.BlockSpec((d, tn), lambda i, j: (0, nj + j)),
                  pl.BlockSpec((1, tn), lambda i, j: (0, j)),
                  pl.BlockSpec((1, tn), lambda i, j: (0, nj + j)),
                  tile, tile],
        out_specs=tile,
        out_shape=jax.ShapeDtypeStruct((n_lat, d), BF16),
        compiler_params=_cparams(("arbitrary", "arbitrary")),
    )(h, w_gate, w_gate, b_gate, b_gate, y_ml, y_ssm)


def _out_kernel(a_ref, w_ref, x_ref, g_ref, o_ref):
    o_ref[...] = x_ref[...] + g_ref[...] * _dot(a_ref[...], w_ref[...])


def _out_call(mix, w_out, x1, gate, n_lat):
    d = mix.shape[1]
    tm = _pick(n_lat, (512, 256))
    tn = _pick(d, (512, 256, 128))
    tile = pl.BlockSpec((tm, tn), lambda i, j: (i, j))
    return pl.pallas_call(
        _out_kernel,
        grid=(n_lat // tm, d // tn),
        in_specs=[pl.BlockSpec((tm, d), lambda i, j: (i, 0)),
                  pl.BlockSpec((d, tn), lambda i, j: (0, j)),
                  tile,
                  pl.BlockSpec((1, tn), lambda i, j: (0, j))],
        out_specs=tile,
        out_shape=jax.ShapeDtypeStruct((n_lat, d), F32),
        compiler_params=_cparams(("arbitrary", "arbitrary")),
    )(mix, w_out, x1, gate)


def _pad_rows(a, rows):
    return jnp.concatenate([a, jnp.zeros((rows - a.shape[0],) + a.shape[1:], a.dtype)], axis=0)


def _pad_cols(a, cols):
    return jnp.concatenate([a, jnp.zeros(a.shape[:-1] + (cols - a.shape[-1],), a.dtype)], axis=-1)


def kernel(x, c, ctx, c_ctx, w_ada, b_ada, norm_w, ffn_w_gate, ffn_w_up, ffn_w_down, w_in, ml_conv_w, ml_conv_b, ml_gate_b, ml_norm_w, w_proj_ml, ssm_conv_w, ssm_conv_b, ssm_dt_bias, ssm_a_log, ssm_d, ssm_norm_w, w_proj_ssm, w_gate, b_gate, w_out, final_norm_w):
    depth = w_ada.shape[0]
    assert depth == 1 and x.shape[0] == 1, "single layer, single batch element"
    t, d = x.shape[1], x.shape[2]
    n_ctx = ctx.shape[1]
    rows_cm = t // GRID_W
    ml_qk = ml_conv_w.shape[2] // 2
    ml_v = ml_norm_w.shape[1]
    ml_cols = 2 * ml_qk + 2 * ml_v + 4 * ML_HEADS
    ssm_inner = ssm_norm_w.shape[1]
    ssm_heads = ssm_d.shape[1]
    ssm_xbc = ssm_conv_w.shape[2]

    cc = _pad_rows(jnp.concatenate([c, c_ctx[None, :]], axis=0), 8)
    mod = _mod_call(cc, w_ada[0], b_ada[0])
    ml, mc = mod[0].reshape(N_MOD, d), mod[1].reshape(N_MOD, d)
    nw = norm_w[0]
    vec1 = _pad_rows(jnp.stack([nw[0], ml[0], ml[1], ml[2], mc[0], mc[1], mc[2],
                                nw[1], ml[3], ml[4], mc[3], mc[4]]), 16)
    vec2 = _pad_rows(jnp.stack([nw[2], ml[6], ml[7], ml[8], ml[6], ml[7], ml[8], final_norm_w]), 16)

    xall = jnp.concatenate([x[0], ctx[0]], axis=0)
    x1, h = _ffn_call(xall, vec1, ffn_w_gate[0, 0].astype(BF16), ffn_w_up[0, 0].astype(BF16),
                      ffn_w_down[0, 0].astype(BF16), t, final=False)

    h_cm = jnp.concatenate(
        [h[:t].reshape(rows_cm, GRID_W, d).transpose(1, 0, 2).reshape(t, d), h[t:]], axis=0)
    w = w_in[0]
    n_ml = 2 * ml_qk + 2 * ml_v
    u_ml = _mm_call(h_cm, w[:, :n_ml].astype(BF16), BF16)
    gates = _mm_call(h_cm, _pad_cols(w[:, n_ml:ml_cols], 128).astype(BF16), F32)
    n_ssm = ssm_inner + ssm_xbc
    u_ssm = _mm_call(h, w[:, ml_cols:ml_cols + n_ssm].astype(BF16), BF16)
    dt = _mm_call(h, w[:, ml_cols + n_ssm:].astype(BF16), F32)

    ml_halo = _make_halo(u_ml[:, :2 * ml_qk], ML_L, t)
    ml_cw = _pad_rows(ml_conv_w[0], 8)
    ml_cb = ml_conv_b[0][None, :]
    gate_b = _pad_cols(ml_gate_b[0].reshape(1, 4 * ML_HEADS), 128)
    h_f = _mlstm_call(u_ml, ml_halo, gates, ml_cw, ml_cb, gate_b, t, reverse=False)
    h_b = _mlstm_call(u_ml, ml_halo, gates, ml_cw, ml_cb, gate_b, t, reverse=True)
    y_ml = _mlproj_call(h_f, h_b, u_ml, ml_norm_w[0][None, :], w_proj_ml[0].astype(BF16), t)
    y_ml = y_ml.reshape(GRID_W, rows_cm, d).transpose(1, 0, 2).reshape(t, d)

    ssm_halo = _make_halo(u_ssm[:, ssm_inner:], SSM_L, t)
    ssm_cw = _pad_rows(ssm_conv_w[0], 8)
    ssm_cb = ssm_conv_b[0][None, :]
    dt_bias = ssm_dt_bias[0].reshape(1, 2 * ssm_heads)
    a_neg = -jnp.exp(ssm_a_log[0].astype(F32)).reshape(1, 2 * ssm_heads)
    d_skip = jnp.repeat(ssm_d[0], SSM_HEADDIM)[None, :]
    y_f = _ssd_call(u_ssm, ssm_halo, dt, ssm_cw, ssm_cb, dt_bias, a_neg, d_skip, t, reverse=False)
    y_b = _ssd_call(u_ssm, ssm_halo, dt, ssm_cw, ssm_cb, dt_bias, a_neg, d_skip, t, reverse=True)
    y_ssm = _ssmproj_call(y_f, y_b, u_ssm, ssm_norm_w[0][None, :], w_proj_ssm[0].astype(BF16), t)

    mix = _gate_call(h, w_gate[0].astype(BF16), b_gate[0][None, :], y_ml, y_ssm, t)
    x2 = _out_call(mix, w_out[0].astype(BF16), x1, ml[5][None, :], t)
    out = _ffn_call(x2, vec2, ffn_w_gate[0, 1].astype(BF16), ffn_w_up[0, 1].astype(BF16),
                    ffn_w_down[0, 1].astype(BF16), t, final=True)
    return out[None]
```

```python
import functools

import jax
import jax.numpy as jnp
from jax import lax
from jax.experimental import pallas as pl
from jax.experimental.pallas import tpu as pltpu

F32 = jnp.float32
BF16 = jnp.bfloat16

GRID_W = 64
N_MOD = 9
EPS = 1e-6
CONV_W = 5
CONV_PAD = CONV_W // 2
HALO = 16
ROW_BLK = 16
FFN_ROW_BLK = 64
ML_HEADS = 8
ML_L = 256
SSM_HEADDIM = 64
SSM_GROUPS = 8
SSM_DSTATE = 128
SSM_L = 128
LANES = 128
VMEM_LIMIT_BYTES = 56 * 1024 * 1024
NEG_INF = float("-inf")


def _cparams(sem):
    return pltpu.CompilerParams(dimension_semantics=sem, vmem_limit_bytes=VMEM_LIMIT_BYTES)


def _pick(n, cands):
    for c in cands:
        if n % c == 0:
            return c
    raise ValueError(f"no tile for {n} in {cands}")


def _silu(x):
    return x * (1.0 / (1.0 + jnp.exp(-x)))


def _sigmoid(x):
    return 1.0 / (1.0 + jnp.exp(-x))


def _log_sigmoid(x):
    return jnp.minimum(x, 0.0) - jnp.log1p(jnp.exp(-jnp.abs(x)))


def _softplus(x):
    return jnp.maximum(x, 0.0) + jnp.log1p(jnp.exp(-jnp.abs(x)))


def _rms(x):
    return x * lax.rsqrt(jnp.mean(jnp.square(x), axis=-1, keepdims=True) + EPS)


def _dot(a, b):
    return jnp.dot(a, b, preferred_element_type=F32)


def _dot_nt(a, b):
    return lax.dot_general(a, b, (((1,), (1,)), ((), ())), preferred_element_type=F32)


def _dot_exact(a, b):
    return jnp.dot(a, b, preferred_element_type=F32, precision=lax.Precision.HIGHEST)


def _row_loop(n_rows, blk, body):
    def step(r, carry):
        body(pl.ds(pl.multiple_of(r * blk, blk), blk))
        return carry
    lax.fori_loop(0, n_rows // blk, step, 0, unroll=2)


def _mod_kernel(c_ref, w_ref, b_ref, o_ref):
    a = _silu(c_ref[...]).astype(BF16)
    o_ref[...] = _dot(a, w_ref[...].astype(BF16)) + b_ref[...]


def _mod_call(cc, w_ada, b_ada):
    k, n = w_ada.shape
    tn = _pick(n, (1024, 512, 256, 128))
    return pl.pallas_call(
        _mod_kernel,
        grid=(n // tn,),
        in_specs=[pl.BlockSpec((8, k), lambda j: (0, 0)),
                  pl.BlockSpec((k, tn), lambda j: (0, j)),
                  pl.BlockSpec((1, tn), lambda j: (0, j))],
        out_specs=pl.BlockSpec((8, tn), lambda j: (0, j)),
        out_shape=jax.ShapeDtypeStruct((8, n), F32),
        compiler_params=_cparams(("arbitrary",)),
    )(cc, w_ada, b_ada.reshape(1, n))


def _ffn_kernel(x_ref, vec_ref, wg_ref, wu_ref, wd_ref, *rest, n_lat, final):
    if final:
        o_ref, h_s, acc_s = rest
    else:
        x1_ref, h2_ref, h_s, acc_s = rest
    i = pl.program_id(0)
    f = pl.program_id(1)
    tm = x_ref.shape[0]

    def vec(row, rows):
        off = jnp.where(i * tm + rows.start >= n_lat, 3, 0)
        return vec_ref[pl.ds(row + off, 1), :]

    @pl.when(f == 0)
    def _():
        def body(rows):
            h = _rms(x_ref[rows, :]) * (vec_ref[0:1, :] * (1.0 + vec(2, rows))) + vec(1, rows)
            h_s[rows, :] = h.astype(BF16)
            acc_s[rows, :] = jnp.zeros((FFN_ROW_BLK, acc_s.shape[1]), F32)
        _row_loop(tm, FFN_ROW_BLK, body)

    h = h_s[...]
    a = _silu(_dot(h, wg_ref[...])) * _dot(h, wu_ref[...])
    acc_s[...] += _dot(a.astype(BF16), wd_ref[...])

    @pl.when(f == pl.num_programs(1) - 1)
    def _():
        def body(rows):
            x1 = x_ref[rows, :] + (0.5 * vec(3, rows)) * acc_s[rows, :]
            if final:
                o_ref[rows, :] = _rms(x1) * vec_ref[7:8, :]
            else:
                x1_ref[rows, :] = x1
                h2 = _rms(x1) * (vec_ref[7:8, :] * (1.0 + vec(9, rows))) + vec(8, rows)
                h2_ref[rows, :] = h2.astype(BF16)
        _row_loop(tm, FFN_ROW_BLK, body)


def _ffn_call(x, vecs, wg, wu, wd, which, n_lat, final):
    m, d = x.shape
    dff = wg.shape[3]
    tm = _pick(m, (768, 512, 256))
    tf = _pick(dff, (512, 256, 128))
    row = pl.BlockSpec((tm, d), lambda i, f: (i, 0), pipeline_mode=pl.Buffered(1))
    if final:
        out_shape = jax.ShapeDtypeStruct((m, d), F32)
        out_specs = row
    else:
        out_shape = (jax.ShapeDtypeStruct((m, d), F32), jax.ShapeDtypeStruct((m, d), BF16))
        out_specs = (row, row)
    return pl.pallas_call(
        functools.partial(_ffn_kernel, n_lat=n_lat, final=final),
        grid=(m // tm, dff // tf),
        in_specs=[row,
                  pl.BlockSpec(vecs.shape, lambda i, f: (0, 0)),
                  pl.BlockSpec((None, None, d, tf), lambda i, f: (0, which, 0, f)),
                  pl.BlockSpec((None, None, d, tf), lambda i, f: (0, which, 0, f)),
                  pl.BlockSpec((None, None, tf, d), lambda i, f: (0, which, f, 0))],
        out_specs=out_specs,
        out_shape=out_shape,
        scratch_shapes=[pltpu.VMEM((tm, d), BF16), pltpu.VMEM((tm, d), F32)],
        compiler_params=_cparams(("arbitrary", "arbitrary")),
    )(x, vecs, wg, wu, wd)


def _mm_kernel(a_ref, w_ref, o_ref):
    o_ref[...] = _dot(a_ref[...], w_ref[...]).astype(o_ref.dtype)


def _mm_call(a, w, col0, n, m, out_dtype):
    k = a.shape[1]
    tm = _pick(m, (768, 512, 256))
    tn = _pick(n, (512, 256, 128))
    assert col0 % tn == 0
    off = col0 // tn
    return pl.pallas_call(
        _mm_kernel,
        grid=(m // tm, n // tn),
        in_specs=[pl.BlockSpec((tm, k), lambda i, j: (i, 0)),
                  pl.BlockSpec((k, tn), lambda i, j: (0, off + j))],
        out_specs=pl.BlockSpec((tm, tn), lambda i, j: (i, j)),
        out_shape=jax.ShapeDtypeStruct((m, n), out_dtype),
        compiler_params=_cparams(("arbitrary", "arbitrary")),
    )(a, w)


def _mmconv_kernel(a_ref, ap_ref, an_ref, w_ref, cw_ref, cb_ref, post_ref, o_ref, xs, *, n_lat, n_tot, nj):
    s = pl.program_id(0)
    tm = a_ref.shape[0]
    n_tiles = pl.num_programs(0) - 1
    cur = s % 2

    @pl.when(s == 0)
    def _():
        xs[1] = jnp.zeros(xs.shape[1:], F32)

    row0 = (jnp.minimum(s, n_tiles - 1) // nj) * tm
    keep_prev = jnp.where((row0 == 0) | (row0 == n_lat), 0.0, 1.0)
    keep_next = jnp.where((row0 + tm == n_tot) | (row0 + tm == n_lat), 0.0, 1.0)
    w = w_ref[...]
    xs[cur, 0:HALO, :] = _dot(ap_ref[...], w) * keep_prev
    xs[cur, HALO:HALO + tm, :] = _dot(a_ref[...], w)
    xs[cur, HALO + tm:2 * HALO + tm, :] = _dot(an_ref[...], w) * keep_next

    def conv(start, rows, boundary=None):
        acc = cb_ref[...]
        for j in range(CONV_W):
            tap = xs[1 - cur, pl.ds(HALO - CONV_PAD + j + start, rows), :]
            if boundary is not None:
                r = start + lax.broadcasted_iota(jnp.int32, (rows, 1), 0)
                side_r = jnp.where(r >= boundary, 1, 0)
                side_j = jnp.where(r + (j - CONV_PAD) >= boundary, 1, 0)
                tap = jnp.where(side_r == side_j, tap, 0.0)
            acc = acc + cw_ref[j:j + 1, :] * tap
        return (_silu(acc) * post_ref[...]).astype(o_ref.dtype)

    o_ref[...] = conv(0, tm)
    split = n_lat % tm
    if split:
        @pl.when(jnp.maximum(s - 1, 0) // nj == n_lat // tm)
        def _():
            lo = (split // HALO) * HALO - HALO
            o_ref[lo:lo + 2 * HALO, :] = conv(lo, 2 * HALO, boundary=split)


def _mmconv_call(a, w, col0, conv_w, conv_b, post, n_lat):
    m, k = a.shape
    n = conv_w.shape[1]
    tm = _pick(m, (768, 512, 256))
    tn = _pick(n, (512, 256, 128))
    assert col0 % tn == 0 and tm % HALO == 0
    off = col0 // tn
    hb = tm // HALO
    last_halo = m // HALO - 1
    nj = n // tn
    n_tiles = (m // tm) * nj

    def cur(s):
        t = jnp.minimum(s, n_tiles - 1)
        return t // nj, t % nj

    def prv(s):
        t = jnp.maximum(s - 1, 0)
        return t // nj, t % nj

    return pl.pallas_call(
        functools.partial(_mmconv_kernel, n_lat=n_lat, n_tot=m, nj=nj),
        grid=(n_tiles + 1,),
        in_specs=[pl.BlockSpec((tm, k), lambda s: (cur(s)[0], 0)),
                  pl.BlockSpec((HALO, k), lambda s: (jnp.maximum(cur(s)[0] * hb - 1, 0), 0)),
                  pl.BlockSpec((HALO, k), lambda s: (jnp.minimum((cur(s)[0] + 1) * hb, last_halo), 0)),
                  pl.BlockSpec((k, tn), lambda s: (0, off + cur(s)[1])),
                  pl.BlockSpec((8, tn), lambda s: (0, prv(s)[1])),
                  pl.BlockSpec((1, tn), lambda s: (0, prv(s)[1])),
                  pl.BlockSpec((1, tn), lambda s: (0, prv(s)[1]))],
        out_specs=pl.BlockSpec((tm, tn), lambda s: prv(s)),
        out_shape=jax.ShapeDtypeStruct((m, n), BF16),
        scratch_shapes=[pltpu.VMEM((2, tm + 2 * HALO, tn), F32)],
        compiler_params=_cparams(("arbitrary",)),
    )(a, a, a, w, conv_w, conv_b, post)


def _chunk_order(c, n_lat_chunks, n_ctx_chunks, reverse):
    if reverse:
        return n_lat_chunks + n_ctx_chunks - 1 - c
    return jnp.where(c < n_ctx_chunks, n_lat_chunks + c, c - n_ctx_chunks)


def _tri(chunk, reverse):
    r = lax.broadcasted_iota(jnp.int32, (chunk, chunk), 0)
    c = lax.broadcasted_iota(jnp.int32, (chunk, chunk), 1)
    return (c >= r) if reverse else (c <= r)


def _mlstm_kernel(q_ref, k_ref, v_ref, g_ref, gb_ref, o_ref, ct_s, n_s, m_s, *, reverse):
    L = q_ref.shape[0]
    dk = q_ref.shape[1] // ML_HEADS
    dv = v_ref.shape[1] // ML_HEADS
    c = pl.program_id(0)

    @pl.when(c == 0)
    def _():
        ct_s[...] = jnp.zeros_like(ct_s)
        n_s[...] = jnp.zeros_like(n_s)
        m_s[...] = jnp.zeros_like(m_s)

    def lanes(x, n):
        return jnp.concatenate([x] * (n // LANES), axis=1)

    mask = _tri(L, reverse)
    g = g_ref[...] + gb_ref[...]
    bc = _dot_exact(mask.astype(F32), _log_sigmoid(g))
    g_t = g.T
    bc_t = bc.T
    end = 0 if reverse else L - 1
    base = 2 * ML_HEADS if reverse else 0
    ones = jnp.ones((L, LANES), BF16)

    for h in range(ML_HEADS):
        io = base + h
        fo = base + ML_HEADS + h
        bb = jnp.broadcast_to(bc[:, fo:fo + 1], (L, LANES))
        b_row = bc_t[fo:fo + 1, :]
        ig_row = g_t[io:io + 1, :]
        b_end = bc[end:end + 1, fo:fo + 1]
        m0 = m_s[h:h + 1, :]
        q = q_ref[:, h * dk:(h + 1) * dk]
        k = k_ref[:, h * dk:(h + 1) * dk]
        v = v_ref[:, h * dv:(h + 1) * dv]
        ct0 = ct_s[h]
        n0 = n_s[h]

        logw = jnp.where(mask, lanes(bb, L) + (ig_row - b_row), NEG_INF)
        m_inter = bb + m0
        m_intra = jnp.broadcast_to(jnp.max(logw, axis=1, keepdims=True), (L, LANES))
        m_out = jnp.maximum(m_inter, m_intra)
        p = (_dot_nt(q, k) * jnp.exp(logw - lanes(m_out, L))).astype(BF16)
        s_inter = jnp.exp(m_inter - m_out)
        num = _dot(p, v) + lanes(s_inter, dv) * _dot(q, ct0.astype(BF16))
        den = _dot(p, ones) + s_inter * _dot(q, n0.astype(BF16))
        inv = 1.0 / jnp.maximum(jnp.abs(den), jnp.exp(-m_out))
        o_ref[:, h * dv:(h + 1) * dv] = (num * lanes(inv, dv)).astype(o_ref.dtype)

        a_row = b_end + ig_row - b_row
        m_loc = jnp.max(a_row, axis=1, keepdims=True)
        kw_t = (k.T.astype(F32) * jnp.exp(a_row - m_loc)).astype(BF16)
        m_new = jnp.maximum(b_end + m0, m_loc)
        s_old = jnp.exp(b_end + m0 - m_new)
        s_new = jnp.exp(m_loc - m_new)
        ct_s[h] = lanes(s_old, dv) * ct0 + lanes(s_new, dv) * _dot(kw_t, v)
        n_s[h] = s_old * n0 + s_new * _dot(kw_t, ones)
        m_s[h:h + 1, :] = m_new


def _mlstm_call(qk, vo, gates, gate_b, n_lat, reverse):
    rows = qk.shape[0]
    L = ML_L
    nqk = qk.shape[1] // 2
    nv = vo.shape[1] // 2
    dk = nqk // ML_HEADS
    dv = nv // ML_HEADS
    nlc = n_lat // L
    ncc = (rows - n_lat) // L
    order = functools.partial(_chunk_order, n_lat_chunks=nlc, n_ctx_chunks=ncc, reverse=reverse)
    return pl.pallas_call(
        functools.partial(_mlstm_kernel, reverse=reverse),
        grid=(nlc + ncc,),
        in_specs=[pl.BlockSpec((L, nqk), lambda c: (order(c), 0)),
                  pl.BlockSpec((L, nqk), lambda c: (order(c), 1)),
                  pl.BlockSpec((L, nv), lambda c: (order(c), 0)),
                  pl.BlockSpec((L, LANES), lambda c: (order(c), 0)),
                  pl.BlockSpec((1, LANES), lambda c: (0, 0))],
        out_specs=pl.BlockSpec((L, nv), lambda c: (order(c), 0)),
        out_shape=jax.ShapeDtypeStruct((rows, nv), BF16),
        scratch_shapes=[pltpu.VMEM((ML_HEADS, dk, dv), F32),
                        pltpu.VMEM((ML_HEADS, dk, LANES), F32),
                        pltpu.VMEM((ML_HEADS, LANES), F32)],
        compiler_params=_cparams(("arbitrary",)),
    )(qk, qk, vo, gates, gate_b)


def _ssd_kernel(x_ref, bm_ref, cm_ref, dt_ref, dtb_ref, a_ref, dsk_ref, y_ref, st_s, *, reverse):
    L = x_ref.shape[0]
    nx = x_ref.shape[1]
    c = pl.program_id(0)

    @pl.when(c == 0)
    def _():
        st_s[...] = jnp.zeros_like(st_s)

    mask = _tri(L, reverse)
    dtv = _softplus(dt_ref[...] + dtb_ref[...])
    bc = _dot_exact(mask.astype(F32), dtv * a_ref[...])
    bc_t = bc.T
    dt_t = dtv.T
    end = 0 if reverse else L - 1
    nheads = nx // SSM_HEADDIM
    hpg = nheads // SSM_GROUPS
    base = nheads if reverse else 0
    pw = 2 * SSM_HEADDIM
    lo = lax.broadcasted_iota(jnp.int32, (1, pw), 1) < SSM_HEADDIM
    lo_b = jnp.where(lo, 1.0, 0.0).astype(BF16)
    hi_b = jnp.where(lo, 0.0, 1.0).astype(BF16)

    for g in range(SSM_GROUPS):
        bm = bm_ref[:, g * SSM_DSTATE:(g + 1) * SSM_DSTATE]
        cm = cm_ref[:, g * SSM_DSTATE:(g + 1) * SSM_DSTATE]
        cb = _dot_nt(cm, bm)
        bm_t = bm.astype(F32).T
        for pr in range(hpg // 2):
            col = (g * hpg + 2 * pr) * SSM_HEADDIM
            xp = x_ref[:, col:col + pw]
            ms, ebs, bts, decs = [], [], [], []
            for e in range(2):
                co = base + g * hpg + 2 * pr + e
                bb = jnp.broadcast_to(bc[:, co:co + 1], (L, L))
                b_row = bc_t[co:co + 1, :]
                dt_row = dt_t[co:co + 1, :]
                b_end = bc[end:end + 1, co:co + 1]
                dec = jnp.exp(jnp.where(mask, bb - b_row, NEG_INF))
                ms.append((cb * dec * dt_row).astype(BF16))
                ebs.append(jnp.exp(bb[:, 0:pw]))
                bts.append((bm_t * (dt_row * jnp.exp(b_end - b_row))).astype(BF16))
                decs.append(jnp.exp(b_end))
            x_lo = xp * lo_b
            x_hi = xp * hi_b
            s0 = st_s[:, col:col + pw]
            y = _dot(ms[0], x_lo) + _dot(ms[1], x_hi)
            y = y + jnp.where(lo, ebs[0], ebs[1]) * _dot(cm, s0.astype(BF16))
            if not reverse:
                y = y + dsk_ref[:, col:col + pw] * xp.astype(F32)
            y_ref[:, col:col + pw] = y.astype(y_ref.dtype)
            st_s[:, col:col + pw] = (s0 * jnp.where(lo, decs[0], decs[1])
                                     + _dot(bts[0], x_lo) + _dot(bts[1], x_hi))


def _ssd_call(xbc, dt, dt_bias, a_neg, d_skip, n_lat, reverse):
    rows = xbc.shape[0]
    L = SSM_L
    assert L >= 2 * SSM_HEADDIM
    nb = SSM_GROUPS * SSM_DSTATE
    nx = xbc.shape[1] - 2 * nb
    nlc = n_lat // L
    ncc = (rows - n_lat) // L
    order = functools.partial(_chunk_order, n_lat_chunks=nlc, n_ctx_chunks=ncc, reverse=reverse)
    xblk = nx // nb
    return pl.pallas_call(
        functools.partial(_ssd_kernel, reverse=reverse),
        grid=(nlc + ncc,),
        in_specs=[pl.BlockSpec((L, nx), lambda c: (order(c), 0)),
                  pl.BlockSpec((L, nb), lambda c: (order(c), xblk)),
                  pl.BlockSpec((L, nb), lambda c: (order(c), xblk + 1)),
                  pl.BlockSpec((L, LANES), lambda c: (order(c), 0)),
                  pl.BlockSpec((1, LANES), lambda c: (0, 0)),
                  pl.BlockSpec((1, LANES), lambda c: (0, 0)),
                  pl.BlockSpec((1, nx), lambda c: (0, 0))],
        out_specs=pl.BlockSpec((L, nx), lambda c: (order(c), 0)),
        out_shape=jax.ShapeDtypeStruct((rows, nx), BF16),
        scratch_shapes=[pltpu.VMEM((SSM_DSTATE, nx), F32)],
        compiler_params=_cparams(("arbitrary",)),
    )(xbc, xbc, xbc, dt, dt_bias, a_neg, d_skip)


def _mlproj_kernel(hf_ref, hb_ref, o_ref, nw_ref, w_ref, y_ref, a_s):
    @pl.when(pl.program_id(1) == 0)
    def _():
        dv = a_s.shape[1] // ML_HEADS

        def body(rows):
            for h in range(ML_HEADS):
                sl = slice(h * dv, (h + 1) * dv)
                hh = _rms(hf_ref[rows, sl].astype(F32) + hb_ref[rows, sl].astype(F32)) * nw_ref[:, sl]
                a_s[rows, sl] = (_sigmoid(o_ref[rows, sl].astype(F32)) * hh).astype(BF16)
        _row_loop(a_s.shape[0], ROW_BLK, body)
    y_ref[...] = _dot(a_s[...], w_ref[...]).astype(y_ref.dtype)


def _mlproj_call(hf, hb, vo, norm_w, w, n_lat):
    nv = hf.shape[1]
    n = w.shape[1]
    tm = _pick(n_lat, (1024, 512, 256))
    tn = _pick(n, (512, 256, 128))
    row = pl.BlockSpec((tm, nv), lambda i, j: (i, 0))
    return pl.pallas_call(
        _mlproj_kernel,
        grid=(n_lat // tm, n // tn),
        in_specs=[row, row,
                  pl.BlockSpec((tm, nv), lambda i, j: (i, 1)),
                  pl.BlockSpec((1, nv), lambda i, j: (0, 0)),
                  pl.BlockSpec((nv, tn), lambda i, j: (0, j))],
        out_specs=pl.BlockSpec((tm, tn), lambda i, j: (i, j)),
        out_shape=jax.ShapeDtypeStruct((n_lat, n), BF16),
        scratch_shapes=[pltpu.VMEM((tm, nv), BF16)],
        compiler_params=_cparams(("arbitrary", "arbitrary")),
    )(hf, hb, vo, norm_w, w)


def _ssmproj_kernel(yf_ref, yb_ref, z_ref, nw_ref, w_ref, y_ref, a_s):
    @pl.when(pl.program_id(1) == 0)
    def _():
        gw = a_s.shape[1] // SSM_GROUPS

        def body(rows):
            for g in range(SSM_GROUPS):
                sl = slice(g * gw, (g + 1) * gw)
                y = (yf_ref[rows, sl].astype(F32) + yb_ref[rows, sl].astype(F32)) * _silu(z_ref[rows, sl].astype(F32))
                a_s[rows, sl] = (_rms(y) * nw_ref[:, sl]).astype(BF16)
        _row_loop(a_s.shape[0], ROW_BLK, body)
    y_ref[...] = _dot(a_s[...], w_ref[...]).astype(y_ref.dtype)


def _ssmproj_call(yf, yb, z, norm_w, w, n_lat):
    nx = yf.shape[1]
    n = w.shape[1]
    tm = _pick(n_lat, (512, 256))
    tn = _pick(n, (512, 256, 128))
    row = pl.BlockSpec((tm, nx), lambda i, j: (i, 0))
    return pl.pallas_call(
        _ssmproj_kernel,
        grid=(n_lat // tm, n // tn),
        in_specs=[row, row, row,
                  pl.BlockSpec((1, nx), lambda i, j: (0, 0)),
                  pl.BlockSpec((nx, tn), lambda i, j: (0, j))],
        out_specs=pl.BlockSpec((tm, tn), lambda i, j: (i, j)),
        out_shape=jax.ShapeDtypeStruct((n_lat, n), BF16),
        scratch_shapes=[pltpu.VMEM((tm, nx), BF16)],
        compiler_params=_cparams(("arbitrary", "arbitrary")),
    )(yf, yb, z, norm_w, w)


def _gate_kernel(h_ref, w1_ref, w2_ref, b1_ref, b2_ref, yml_ref, yssm_ref, o_ref):
    h = h_ref[...]
    g1 = _sigmoid(_dot(h, w1_ref[...]) + b1_ref[...])
    g2 = _sigmoid(_dot(h, w2_ref[...]) + b2_ref[...])
    o_ref[...] = (g1 * yml_ref[...].astype(F32) + g2 * yssm_ref[...].astype(F32)).astype(o_ref.dtype)


def _gate_call(h, w_gate, b_gate, y_ml, y_ssm, n_lat):
    d = h.shape[1]
    tm = _pick(n_lat, (1024, 512, 256))
    tn = _pick(d, (512, 256, 128))
    nj = d // tn
    tile = pl.BlockSpec((tm, tn), lambda i, j: (i, j))
    return pl.pallas_call(
        _gate_kernel,
        grid=(n_lat // tm, nj),
        in_specs=[pl.BlockSpec((tm, d), lambda i, j: (i, 0)),
                  pl.BlockSpec((d, tn), lambda i, j: (0, j)),
                  pl.BlockSpec((d, tn), lambda i, j: (0, nj + j)),
                  pl.BlockSpec((1, tn), lambda i, j: (0, j)),
                  pl.BlockSpec((1, tn), lambda i, j: (0, nj + j)),
                  tile, tile],
        out_specs=tile,
        out_shape=jax.ShapeDtypeStruct((n_lat, d), BF16),
        compiler_params=_cparams(("arbitrary", "arbitrary")),
    )(h, w_gate, w_gate, b_gate, b_gate, y_ml, y_ssm)


def _out_kernel(a_ref, w_ref, x_ref, g_ref, o_ref):
    o_ref[...] = x_ref[...] + g_ref[...] * _dot(a_ref[...], w_ref[...])


def _out_call(mix, w_out, x1, gate, n_lat):
    d = mix.shape[1]
    tm = _pick(n_lat, (1024, 512, 256))
    tn = _pick(d, (512, 256, 128))
    tile = pl.BlockSpec((tm, tn), lambda i, j: (i, j))
    return pl.pallas_call(
        _out_kernel,
        grid=(n_lat // tm, d // tn),
        in_specs=[pl.BlockSpec((tm, d), lambda i, j: (i, 0)),
                  pl.BlockSpec((d, tn), lambda i, j: (0, j)),
                  tile,
                  pl.BlockSpec((1, tn), lambda i, j: (0, j))],
        out_specs=tile,
        out_shape=jax.ShapeDtypeStruct((n_lat, d), F32),
        compiler_params=_cparams(("arbitrary", "arbitrary")),
    )(mix, w_out, x1, gate)


def _pad_rows(a, rows):
    return jnp.concatenate([a, jnp.zeros((rows - a.shape[0],) + a.shape[1:], a.dtype)], axis=0)


def _pad_cols(a, cols):
    return jnp.concatenate([a, jnp.zeros(a.shape[:-1] + (cols - a.shape[-1],), a.dtype)], axis=-1)


def kernel(x, c, ctx, c_ctx, w_ada, b_ada, norm_w, ffn_w_gate, ffn_w_up, ffn_w_down, w_in, ml_conv_w, ml_conv_b, ml_gate_b, ml_norm_w, w_proj_ml, ssm_conv_w, ssm_conv_b, ssm_dt_bias, ssm_a_log, ssm_d, ssm_norm_w, w_proj_ssm, w_gate, b_gate, w_out, final_norm_w):
    depth = w_ada.shape[0]
    assert depth == 1 and x.shape[0] == 1, "single layer, single batch element"
    t, d = x.shape[1], x.shape[2]
    n_tot = t + ctx.shape[1]
    rows_cm = t // GRID_W
    ml_qk = ml_conv_w.shape[2] // 2
    ml_v = ml_norm_w.shape[1]
    n_ml = 2 * ml_qk + 2 * ml_v
    ml_cols = n_ml + 4 * ML_HEADS
    ssm_inner = ssm_norm_w.shape[1]
    ssm_heads = ssm_d.shape[1]
    ssm_xbc = ssm_conv_w.shape[2]

    cc = _pad_rows(jnp.concatenate([c, c_ctx[None, :]], axis=0), 8)
    mod = _mod_call(cc, w_ada[0], b_ada[0])
    ml, mc = mod[0].reshape(N_MOD, d), mod[1].reshape(N_MOD, d)
    nw = norm_w[0]
    zero = jnp.zeros_like(nw[0])
    vec1 = _pad_rows(jnp.stack([nw[0], ml[0], ml[1], ml[2], mc[0], mc[1], mc[2],
                                nw[1], ml[3], ml[4], zero, mc[3], mc[4]]), 16)
    vec2 = _pad_rows(jnp.stack([nw[2], ml[6], ml[7], ml[8], zero, zero, zero, final_norm_w]), 16)
    wg, wu, wd = ffn_w_gate.astype(BF16), ffn_w_up.astype(BF16), ffn_w_down.astype(BF16)

    xall = jnp.concatenate([x[0], ctx[0]], axis=0)
    x1, h = _ffn_call(xall, vec1, wg, wu, wd, 0, t, final=False)

    h_cm = jnp.concatenate(
        [h[:t].reshape(rows_cm, GRID_W, d).transpose(1, 0, 2).reshape(t, d), h[t:]], axis=0)
    w = w_in[0]
    n_ssm = ssm_inner + ssm_xbc
    wp = jnp.concatenate([w[:, :n_ml], w[:, ml_cols:ml_cols + n_ssm],
                          _pad_cols(w[:, n_ml:ml_cols], LANES), w[:, ml_cols + n_ssm:]], axis=1).astype(BF16)
    ml_cw = _pad_rows(ml_conv_w[0], 8)
    ml_post = jnp.concatenate([jnp.full((1, ml_qk), (ml_qk // ML_HEADS) ** -0.5, F32),
                               jnp.ones((1, ml_qk), F32)], axis=1)
    qk = _mmconv_call(h_cm, wp, 0, ml_cw, ml_conv_b[0][None, :], ml_post, t)
    vo = _mm_call(h_cm, wp, 2 * ml_qk, 2 * ml_v, n_tot, BF16)
    z = _mm_call(h, wp, n_ml, ssm_inner, t, BF16)
    xbc = _mmconv_call(h, wp, n_ml + ssm_inner, _pad_rows(ssm_conv_w[0], 8), ssm_conv_b[0][None, :],
                       jnp.ones((1, ssm_xbc), F32), t)
    gates = _mm_call(h_cm, wp, n_ml + n_ssm, LANES, n_tot, F32)
    dt = _mm_call(h, wp, n_ml + n_ssm + LANES, 2 * ssm_heads, n_tot, F32)

    gate_b = _pad_cols(ml_gate_b[0].reshape(1, 4 * ML_HEADS), LANES)
    h_f = _mlstm_call(qk, vo, gates, gate_b, t, reverse=False)
    h_b = _mlstm_call(qk, vo, gates, gate_b, t, reverse=True)
    y_ml = _mlproj_call(h_f, h_b, vo, ml_norm_w[0][None, :], w_proj_ml[0].astype(BF16), t)
    y_ml = y_ml.reshape(GRID_W, rows_cm, d).transpose(1, 0, 2).reshape(t, d)

    dt_bias = ssm_dt_bias[0].reshape(1, 2 * ssm_heads)
    a_neg = -jnp.exp(ssm_a_log[0].astype(F32)).reshape(1, 2 * ssm_heads)
    d_skip = jnp.repeat(ssm_d[0], SSM_HEADDIM)[None, :]
    y_f = _ssd_call(xbc, dt, dt_bias, a_neg, d_skip, t, reverse=False)
    y_b = _ssd_call(xbc, dt, dt_bias, a_neg, d_skip, t, reverse=True)
    y_ssm = _ssmproj_call(y_f, y_b, z, ssm_norm_w[0][None, :], w_proj_ssm[0].astype(BF16), t)

    mix = _gate_call(h, w_gate[0].astype(BF16), b_gate[0][None, :], y_ml, y_ssm, t)
    x2 = _out_call(mix, w_out[0].astype(BF16), x1, ml[5][None, :], t)
    out = _ffn_call(x2, vec2, wg, wu, wd, 1, t, final=True)
    return out[None]
```

```python
import functools

import jax
import jax.numpy as jnp
from jax import lax
from jax.experimental import pallas as pl
from jax.experimental.pallas import tpu as pltpu

F32 = jnp.float32
BF16 = jnp.bfloat16

GRID_W = 64
N_MOD = 9
EPS = 1e-6
CONV_W = 5
CONV_PAD = CONV_W // 2
HALO = 16
ROW_BLK = 16
FFN_ROW_BLK = 64
MM_ROWS = 256
CONV_ROWS = 128
ML_HEADS = 8
ML_L = 256
SSM_HEADDIM = 64
SSM_GROUPS = 8
SSM_DSTATE = 128
SSM_L = 128
LANES = 128
VMEM_LIMIT_BYTES = 56 * 1024 * 1024
NEG_INF = float("-inf")
LOG2E = 1.4426950408889634


def _cparams(sem):
    return pltpu.CompilerParams(dimension_semantics=sem, vmem_limit_bytes=VMEM_LIMIT_BYTES)


def _pick(n, cands):
    for c in cands:
        if n % c == 0:
            return c
    raise ValueError(f"no tile for {n} in {cands}")


def _silu(x):
    return x * (1.0 / (1.0 + jnp.exp(-x)))


def _sigmoid(x):
    return 1.0 / (1.0 + jnp.exp(-x))


def _log_sigmoid(x):
    return jnp.minimum(x, 0.0) - jnp.log1p(jnp.exp(-jnp.abs(x)))


def _softplus(x):
    return jnp.maximum(x, 0.0) + jnp.log1p(jnp.exp(-jnp.abs(x)))


def _rms(x):
    return x * lax.rsqrt(jnp.mean(jnp.square(x), axis=-1, keepdims=True) + EPS)


def _dot(a, b):
    return jnp.dot(a, b, preferred_element_type=F32)


def _dot_nt(a, b):
    return lax.dot_general(a, b, (((1,), (1,)), ((), ())), preferred_element_type=F32)


def _dot_exact(a, b):
    return jnp.dot(a, b, preferred_element_type=F32, precision=lax.Precision.HIGHEST)


def _row_loop(n_rows, blk, body):
    def step(r, carry):
        body(pl.ds(pl.multiple_of(r * blk, blk), blk))
        return carry
    lax.fori_loop(0, n_rows // blk, step, 0, unroll=2)


def _mod_kernel(c_ref, w_ref, b_ref, o_ref):
    a = _silu(c_ref[...]).astype(BF16)
    o_ref[...] = _dot(a, w_ref[...].astype(BF16)) + b_ref[...]


def _mod_call(cc, w_ada, b_ada):
    k, n = w_ada.shape
    tn = _pick(n, (1024, 512, 256, 128))
    return pl.pallas_call(
        _mod_kernel,
        grid=(n // tn,),
        in_specs=[pl.BlockSpec((8, k), lambda j: (0, 0)),
                  pl.BlockSpec((k, tn), lambda j: (0, j)),
                  pl.BlockSpec((1, tn), lambda j: (0, j))],
        out_specs=pl.BlockSpec((8, tn), lambda j: (0, j)),
        out_shape=jax.ShapeDtypeStruct((8, n), F32),
        compiler_params=_cparams(("arbitrary",)),
    )(cc, w_ada, b_ada.reshape(1, n))


def _ffn_kernel(x_ref, vec_ref, wg_ref, wu_ref, wd_ref, *rest, n_lat, final):
    if final:
        o_ref, h_s, acc_s = rest
    else:
        x1_ref, h2_ref, h_s, acc_s = rest
    i = pl.program_id(0)
    f = pl.program_id(1)
    tm = x_ref.shape[0]

    def vec(row, rows):
        off = jnp.where(i * tm + rows.start >= n_lat, 3, 0)
        return vec_ref[pl.ds(row + off, 1), :]

    @pl.when(f == 0)
    def _():
        def body(rows):
            h = _rms(x_ref[rows, :]) * (vec_ref[0:1, :] * (1.0 + vec(2, rows))) + vec(1, rows)
            h_s[rows, :] = h.astype(BF16)
            acc_s[rows, :] = jnp.zeros((FFN_ROW_BLK, acc_s.shape[1]), F32)
        _row_loop(tm, FFN_ROW_BLK, body)

    h = h_s[...]
    a = _silu(_dot(h, wg_ref[...])) * _dot(h, wu_ref[...])
    acc_s[...] += _dot(a.astype(BF16), wd_ref[...])

    @pl.when(f == pl.num_programs(1) - 1)
    def _():
        def body(rows):
            x1 = x_ref[rows, :] + (0.5 * vec(3, rows)) * acc_s[rows, :]
            if final:
                o_ref[rows, :] = _rms(x1) * vec_ref[7:8, :]
            else:
                x1_ref[rows, :] = x1
                h2 = _rms(x1) * (vec_ref[7:8, :] * (1.0 + vec(9, rows))) + vec(8, rows)
                h2_ref[rows, :] = h2.astype(BF16)
        _row_loop(tm, FFN_ROW_BLK, body)


def _ffn_call(x, vecs, wg, wu, wd, which, n_lat, final):
    m, d = x.shape
    dff = wg.shape[3]
    tm = _pick(m, (768, 512, 256))
    tf = _pick(dff, (512, 256, 128))
    row = pl.BlockSpec((tm, d), lambda i, f: (i, 0), pipeline_mode=pl.Buffered(1))
    if final:
        out_shape = jax.ShapeDtypeStruct((m, d), F32)
        out_specs = row
    else:
        out_shape = (jax.ShapeDtypeStruct((m, d), F32), jax.ShapeDtypeStruct((m, d), BF16))
        out_specs = (row, row)
    return pl.pallas_call(
        functools.partial(_ffn_kernel, n_lat=n_lat, final=final),
        grid=(m // tm, dff // tf),
        in_specs=[row,
                  pl.BlockSpec(vecs.shape, lambda i, f: (0, 0)),
                  pl.BlockSpec((None, None, d, tf), lambda i, f: (0, which, 0, f)),
                  pl.BlockSpec((None, None, d, tf), lambda i, f: (0, which, 0, f)),
                  pl.BlockSpec((None, None, tf, d), lambda i, f: (0, which, f, 0))],
        out_specs=out_specs,
        out_shape=out_shape,
        scratch_shapes=[pltpu.VMEM((tm, d), BF16), pltpu.VMEM((tm, d), F32)],
        compiler_params=_cparams(("arbitrary", "arbitrary")),
    )(x, vecs, wg, wu, wd)


def _mm_kernel(a_ref, w_ref, o_ref):
    o_ref[...] = _dot(a_ref[...], w_ref[...]).astype(o_ref.dtype)


def _mm_call(a, w, col0, n, m, out_dtype):
    k = a.shape[1]
    tm = _pick(m, (1024, 768, 512, 256))
    tn = _pick(n, (1024, 512, 256, 128))
    assert col0 % tn == 0
    off = col0 // tn
    return pl.pallas_call(
        _mm_kernel,
        grid=(m // tm, n // tn),
        in_specs=[pl.BlockSpec((tm, k), lambda i, j: (i, 0)),
                  pl.BlockSpec((k, tn), lambda i, j: (0, off + j))],
        out_specs=pl.BlockSpec((tm, tn), lambda i, j: (i, j)),
        out_shape=jax.ShapeDtypeStruct((m, n), out_dtype),
        compiler_params=_cparams(("arbitrary", "arbitrary")),
    )(a, w)


def _mmconv_kernel(a_ref, ap_ref, an_ref, w_ref, cw_ref, cb_ref, post_ref, o_ref, xs0, xs1, *, n_lat, n_tot, nj):
    s = pl.program_id(0)
    tm = a_ref.shape[0]
    n_tiles = pl.num_programs(0) - 1
    row0 = (jnp.minimum(s, n_tiles - 1) // nj) * tm
    keep_prev = jnp.where((row0 == 0) | (row0 == n_lat), 0.0, 1.0)
    keep_next = jnp.where((row0 + tm == n_tot) | (row0 + tm == n_lat), 0.0, 1.0)
    split = n_lat % tm

    def step(xw, xr):
        w = w_ref[...]

        def conv(start, rows, boundary=None):
            acc = cb_ref[...]
            for j in range(CONV_W):
                tap = xr[pl.ds(HALO - CONV_PAD + j + start, rows), :]
                if boundary is not None:
                    r = start + lax.broadcasted_iota(jnp.int32, (rows, 1), 0)
                    side_r = jnp.where(r >= boundary, 1, 0)
                    side_j = jnp.where(r + (j - CONV_PAD) >= boundary, 1, 0)
                    tap = jnp.where(side_r == side_j, tap, 0.0)
                acc = acc + cw_ref[j:j + 1, :] * tap
            return (_silu(acc) * post_ref[...]).astype(o_ref.dtype)

        xw[0:HALO, :] = _dot(ap_ref[...], w) * keep_prev
        xw[HALO + tm:2 * HALO + tm, :] = _dot(an_ref[...], w) * keep_next
        for r0 in range(0, tm, MM_ROWS):
            xw[HALO + r0:HALO + r0 + MM_ROWS, :] = _dot(a_ref[r0:r0 + MM_ROWS, :], w)
            for c0 in range(r0, r0 + MM_ROWS, CONV_ROWS):
                o_ref[c0:c0 + CONV_ROWS, :] = conv(c0, CONV_ROWS)
        if split:
            @pl.when(jnp.maximum(s - 1, 0) // nj == n_lat // tm)
            def _():
                lo = (split // HALO) * HALO - HALO
                o_ref[lo:lo + 2 * HALO, :] = conv(lo, 2 * HALO, boundary=split)

    @pl.when(s == 0)
    def _():
        xs1[...] = jnp.zeros_like(xs1)

    @pl.when(s % 2 == 0)
    def _():
        step(xs0, xs1)

    @pl.when(s % 2 == 1)
    def _():
        step(xs1, xs0)


def _mmconv_call(a, w, col0, conv_w, conv_b, post, n_lat):
    m, k = a.shape
    n = conv_w.shape[1]
    tm = _pick(m, (768, 512, 256))
    tn = _pick(n, (512, 256, 128))
    assert col0 % tn == 0 and tm % HALO == 0
    off = col0 // tn
    hb = tm // HALO
    last_halo = m // HALO - 1
    nj = n // tn
    n_tiles = (m // tm) * nj

    def cur(s):
        t = jnp.minimum(s, n_tiles - 1)
        return t // nj, t % nj

    def prv(s):
        t = jnp.maximum(s - 1, 0)
        return t // nj, t % nj

    return pl.pallas_call(
        functools.partial(_mmconv_kernel, n_lat=n_lat, n_tot=m, nj=nj),
        grid=(n_tiles + 1,),
        in_specs=[pl.BlockSpec((tm, k), lambda s: (cur(s)[0], 0)),
                  pl.BlockSpec((HALO, k), lambda s: (jnp.maximum(cur(s)[0] * hb - 1, 0), 0)),
                  pl.BlockSpec((HALO, k), lambda s: (jnp.minimum((cur(s)[0] + 1) * hb, last_halo), 0)),
                  pl.BlockSpec((k, tn), lambda s: (0, off + cur(s)[1])),
                  pl.BlockSpec((8, tn), lambda s: (0, prv(s)[1])),
                  pl.BlockSpec((1, tn), lambda s: (0, prv(s)[1])),
                  pl.BlockSpec((1, tn), lambda s: (0, prv(s)[1]))],
        out_specs=pl.BlockSpec((tm, tn), lambda s: prv(s)),
        out_shape=jax.ShapeDtypeStruct((m, n), BF16),
        scratch_shapes=[pltpu.VMEM((tm + 2 * HALO, tn), F32), pltpu.VMEM((tm + 2 * HALO, tn), F32)],
        compiler_params=_cparams(("arbitrary",)),
    )(a, a, a, w, conv_w, conv_b, post)


def _chunk_order(c, n_lat_chunks, n_ctx_chunks, reverse):
    if reverse:
        return n_lat_chunks + n_ctx_chunks - 1 - c
    return jnp.where(c < n_ctx_chunks, n_lat_chunks + c, c - n_ctx_chunks)


def _tri(chunk, reverse):
    r = lax.broadcasted_iota(jnp.int32, (chunk, chunk), 0)
    c = lax.broadcasted_iota(jnp.int32, (chunk, chunk), 1)
    return (c >= r) if reverse else (c <= r)


def _mlstm_kernel(q_ref, k_ref, v_ref, g_ref, gn_ref, gb_ref, o_ref, ct_s, n_s, m_s, bc_s, bct_s, gt_s, *, reverse):
    L = q_ref.shape[0]
    dk = q_ref.shape[1] // ML_HEADS
    dv = v_ref.shape[1] // ML_HEADS
    c = pl.program_id(0)

    mask = _tri(L, reverse)

    def prefix(g_blk):
        g = g_blk + gb_ref[...]
        bc = _dot_exact(mask.astype(F32), _log_sigmoid(g))
        bc_s[...] = bc
        bct_s[...] = bc.T
        gt_s[...] = g.T

    @pl.when(c == 0)
    def _():
        ct_s[...] = jnp.zeros_like(ct_s)
        n_s[...] = jnp.zeros_like(n_s)
        m_s[...] = jnp.zeros_like(m_s)
        prefix(g_ref[...])

    def lanes(x, n):
        return jnp.concatenate([x] * (n // LANES), axis=1)

    bc = bc_s[...]
    bc_t = bct_s[...]
    g_t = gt_s[...]
    end = 0 if reverse else L - 1
    base = 2 * ML_HEADS if reverse else 0
    ones = jnp.ones((L, LANES), BF16)

    for h in range(ML_HEADS):
        io = base + h
        fo = base + ML_HEADS + h
        bb = jnp.broadcast_to(bc[:, fo:fo + 1], (L, LANES))
        b_row = bc_t[fo:fo + 1, :]
        ig_row = g_t[io:io + 1, :]
        b_end = bc[end:end + 1, fo:fo + 1]
        m0 = m_s[h:h + 1, :]
        q = q_ref[:, h * dk:(h + 1) * dk]
        k = k_ref[:, h * dk:(h + 1) * dk]
        v = v_ref[:, h * dv:(h + 1) * dv]
        ct0 = ct_s[h]
        n0 = n_s[h]

        logw = jnp.where(mask, lanes(bb, L) + (ig_row - b_row), NEG_INF)
        m_inter = bb + m0
        m_intra = jnp.broadcast_to(jnp.max(logw, axis=1, keepdims=True), (L, LANES))
        m_out = jnp.maximum(m_inter, m_intra)
        p = (_dot_nt(q, k) * jnp.exp(logw - lanes(m_out, L))).astype(BF16)
        s_inter = jnp.exp(m_inter - m_out)
        num = _dot(p, v) + lanes(s_inter, dv) * _dot(q, ct0.astype(BF16))
        den = _dot(p, ones) + s_inter * _dot(q, n0.astype(BF16))
        inv = 1.0 / jnp.maximum(jnp.abs(den), jnp.exp(-m_out))
        o_ref[:, h * dv:(h + 1) * dv] = (num * lanes(inv, dv)).astype(o_ref.dtype)

        a_row = b_end + ig_row - b_row
        m_loc = jnp.max(a_row, axis=1, keepdims=True)
        kw_t = (k.T.astype(F32) * jnp.exp(a_row - m_loc)).astype(BF16)
        m_new = jnp.maximum(b_end + m0, m_loc)
        s_old = jnp.exp(b_end + m0 - m_new)
        s_new = jnp.exp(m_loc - m_new)
        ct_s[h] = lanes(s_old, dv) * ct0 + lanes(s_new, dv) * _dot(kw_t, v)
        n_s[h] = s_old * n0 + s_new * _dot(kw_t, ones)
        m_s[h:h + 1, :] = m_new

    prefix(gn_ref[...])


def _mlstm_call(qk, vo, gates, gate_b, n_lat, reverse):
    rows = qk.shape[0]
    L = ML_L
    nqk = qk.shape[1] // 2
    nv = vo.shape[1] // 2
    dk = nqk // ML_HEADS
    dv = nv // ML_HEADS
    nlc = n_lat // L
    ncc = (rows - n_lat) // L
    order = functools.partial(_chunk_order, n_lat_chunks=nlc, n_ctx_chunks=ncc, reverse=reverse)
    return pl.pallas_call(
        functools.partial(_mlstm_kernel, reverse=reverse),
        grid=(nlc + ncc,),
        in_specs=[pl.BlockSpec((L, nqk), lambda c: (order(c), 0)),
                  pl.BlockSpec((L, nqk), lambda c: (order(c), 1)),
                  pl.BlockSpec((L, nv), lambda c: (order(c), 0)),
                  pl.BlockSpec((L, LANES), lambda c: (order(c), 0)),
                  pl.BlockSpec((L, LANES), lambda c: (order(jnp.minimum(c + 1, nlc + ncc - 1)), 0)),
                  pl.BlockSpec((1, LANES), lambda c: (0, 0))],
        out_specs=pl.BlockSpec((L, nv), lambda c: (order(c), 0)),
        out_shape=jax.ShapeDtypeStruct((rows, nv), BF16),
        scratch_shapes=[pltpu.VMEM((ML_HEADS, dk, dv), F32),
                        pltpu.VMEM((ML_HEADS, dk, LANES), F32),
                        pltpu.VMEM((ML_HEADS, LANES), F32),
                        pltpu.VMEM((L, LANES), F32), pltpu.VMEM((LANES, L), F32), pltpu.VMEM((LANES, L), F32)],
        compiler_params=_cparams(("arbitrary",)),
    )(qk, qk, vo, gates, gates, gate_b)


def _ssd_kernel(x_ref, bm_ref, cm_ref, dt_ref, dtn_ref, dtb_ref, a_ref, dsk_ref, y_ref,
                st_s, bc_s, bct_s, dtt_s, *, reverse):
    L = x_ref.shape[0]
    nx = x_ref.shape[1]
    c = pl.program_id(0)
    mask = _tri(L, reverse)

    def prefix(dt_blk):
        dtv = _softplus(dt_blk + dtb_ref[...])
        bc = _dot_exact(mask.astype(F32), dtv * a_ref[...]) * LOG2E
        bc_s[...] = bc
        bct_s[...] = bc.T
        dtt_s[...] = jnp.log2(dtv).T

    @pl.when(c == 0)
    def _():
        st_s[...] = jnp.zeros_like(st_s)
        prefix(dt_ref[...])

    bc = bc_s[...]
    bc_t = bct_s[...]
    ldt_t = dtt_s[...]
    end = 0 if reverse else L - 1
    nheads = nx // SSM_HEADDIM
    hpg = nheads // SSM_GROUPS
    base = nheads if reverse else 0
    pw = 2 * SSM_HEADDIM
    lo = lax.broadcasted_iota(jnp.int32, (1, pw), 1) < SSM_HEADDIM
    lo_b = jnp.where(lo, 1.0, 0.0).astype(BF16)
    hi_b = jnp.where(lo, 0.0, 1.0).astype(BF16)

    for g in range(SSM_GROUPS):
        bm = bm_ref[:, g * SSM_DSTATE:(g + 1) * SSM_DSTATE]
        cm = cm_ref[:, g * SSM_DSTATE:(g + 1) * SSM_DSTATE]
        cb = _dot_nt(cm, bm)
        bm_t = bm.astype(F32).T
        for pr in range(hpg // 2):
            col = (g * hpg + 2 * pr) * SSM_HEADDIM
            xp = x_ref[:, col:col + pw]
            ms, ebs, bts, decs = [], [], [], []
            for e in range(2):
                co = base + g * hpg + 2 * pr + e
                bb = jnp.broadcast_to(bc[:, co:co + 1], (L, L))
                r_row = bc_t[co:co + 1, :] - ldt_t[co:co + 1, :]
                b_end = bc[end:end + 1, co:co + 1]
                ms.append((cb * jnp.exp2(jnp.where(mask, bb - r_row, NEG_INF))).astype(BF16))
                ebs.append(jnp.exp2(bb[:, 0:pw]))
                bts.append((bm_t * jnp.exp2(b_end - r_row)).astype(BF16))
                decs.append(jnp.exp2(b_end))
            x_lo = xp * lo_b
            x_hi = xp * hi_b
            s0 = st_s[:, col:col + pw]
            y = _dot(ms[0], x_lo) + _dot(ms[1], x_hi)
            y = y + jnp.where(lo, ebs[0], ebs[1]) * _dot(cm, s0.astype(BF16))
            if not reverse:
                y = y + dsk_ref[:, col:col + pw] * xp.astype(F32)
            y_ref[:, col:col + pw] = y.astype(y_ref.dtype)
            st_s[:, col:col + pw] = (s0 * jnp.where(lo, decs[0], decs[1])
                                     + _dot(bts[0], x_lo) + _dot(bts[1], x_hi))

    prefix(dtn_ref[...])


def _ssd_call(xbc, dt, dt_bias, a_neg, d_skip, n_lat, reverse):
    rows = xbc.shape[0]
    L = SSM_L
    assert L >= 2 * SSM_HEADDIM
    nb = SSM_GROUPS * SSM_DSTATE
    nx = xbc.shape[1] - 2 * nb
    nlc = n_lat // L
    ncc = (rows - n_lat) // L
    order = functools.partial(_chunk_order, n_lat_chunks=nlc, n_ctx_chunks=ncc, reverse=reverse)
    xblk = nx // nb
    nc = nlc + ncc
    return pl.pallas_call(
        functools.partial(_ssd_kernel, reverse=reverse),
        grid=(nc,),
        in_specs=[pl.BlockSpec((L, nx), lambda c: (order(c), 0)),
                  pl.BlockSpec((L, nb), lambda c: (order(c), xblk)),
                  pl.BlockSpec((L, nb), lambda c: (order(c), xblk + 1)),
                  pl.BlockSpec((L, LANES), lambda c: (order(c), 0)),
                  pl.BlockSpec((L, LANES), lambda c: (order(jnp.minimum(c + 1, nc - 1)), 0)),
                  pl.BlockSpec((1, LANES), lambda c: (0, 0)),
                  pl.BlockSpec((1, LANES), lambda c: (0, 0)),
                  pl.BlockSpec((1, nx), lambda c: (0, 0))],
        out_specs=pl.BlockSpec((L, nx), lambda c: (order(c), 0)),
        out_shape=jax.ShapeDtypeStruct((rows, nx), BF16),
        scratch_shapes=[pltpu.VMEM((SSM_DSTATE, nx), F32),
                        pltpu.VMEM((L, LANES), F32), pltpu.VMEM((LANES, L), F32), pltpu.VMEM((LANES, L), F32)],
        compiler_params=_cparams(("arbitrary",)),
    )(xbc, xbc, xbc, dt, dt, dt_bias, a_neg, d_skip)


def _mlproj_kernel(hf_ref, hb_ref, o_ref, nw_ref, w_ref, y_ref, a_s):
    @pl.when(pl.program_id(1) == 0)
    def _():
        dv = a_s.shape[1] // ML_HEADS

        def body(rows):
            for h in range(ML_HEADS):
                sl = slice(h * dv, (h + 1) * dv)
                hh = _rms(hf_ref[rows, sl].astype(F32) + hb_ref[rows, sl].astype(F32)) * nw_ref[:, sl]
                a_s[rows, sl] = (_sigmoid(o_ref[rows, sl].astype(F32)) * hh).astype(BF16)
        _row_loop(a_s.shape[0], ROW_BLK, body)
    y_ref[...] = _dot(a_s[...], w_ref[...]).astype(y_ref.dtype)


def _mlproj_call(hf, hb, vo, norm_w, w, n_lat):
    nv = hf.shape[1]
    n = w.shape[1]
    tm = _pick(n_lat, (1024, 512, 256))
    tn = _pick(n, (512, 256, 128))
    row = pl.BlockSpec((tm, nv), lambda i, j: (i, 0))
    return pl.pallas_call(
        _mlproj_kernel,
        grid=(n_lat // tm, n // tn),
        in_specs=[row, row,
                  pl.BlockSpec((tm, nv), lambda i, j: (i, 1)),
                  pl.BlockSpec((1, nv), lambda i, j: (0, 0)),
                  pl.BlockSpec((nv, tn), lambda i, j: (0, j))],
        out_specs=pl.BlockSpec((tm, tn), lambda i, j: (i, j)),
        out_shape=jax.ShapeDtypeStruct((n_lat, n), BF16),
        scratch_shapes=[pltpu.VMEM((tm, nv), BF16)],
        compiler_params=_cparams(("arbitrary", "arbitrary")),
    )(hf, hb, vo, norm_w, w)


def _ssmproj_kernel(yf_ref, yb_ref, z_ref, nw_ref, w_ref, y_ref, a_s):
    @pl.when(pl.program_id(1) == 0)
    def _():
        gw = a_s.shape[1] // SSM_GROUPS

        def body(rows):
            for g in range(SSM_GROUPS):
                sl = slice(g * gw, (g + 1) * gw)
                y = (yf_ref[rows, sl].astype(F32) + yb_ref[rows, sl].astype(F32)) * _silu(z_ref[rows, sl].astype(F32))
                a_s[rows, sl] = (_rms(y) * nw_ref[:, sl]).astype(BF16)
        _row_loop(a_s.shape[0], ROW_BLK, body)
    y_ref[...] = _dot(a_s[...], w_ref[...]).astype(y_ref.dtype)


def _ssmproj_call(yf, yb, z, norm_w, w, n_lat):
    nx = yf.shape[1]
    n = w.shape[1]
    tm = _pick(n_lat, (512, 256))
    tn = _pick(n, (512, 256, 128))
    row = pl.BlockSpec((tm, nx), lambda i, j: (i, 0))
    return pl.pallas_call(
        _ssmproj_kernel,
        grid=(n_lat // tm, n // tn),
        in_specs=[row, row, row,
                  pl.BlockSpec((1, nx), lambda i, j: (0, 0)),
                  pl.BlockSpec((nx, tn), lambda i, j: (0, j))],
        out_specs=pl.BlockSpec((tm, tn), lambda i, j: (i, j)),
        out_shape=jax.ShapeDtypeStruct((n_lat, n), BF16),
        scratch_shapes=[pltpu.VMEM((tm, nx), BF16)],
        compiler_params=_cparams(("arbitrary", "arbitrary")),
    )(yf, yb, z, norm_w, w)


def _gate_kernel(h_ref, w1_ref, w2_ref, b1_ref, b2_ref, yml_ref, yssm_ref, o_ref):
    h = h_ref[...]
    g1 = _sigmoid(_dot(h, w1_ref[...]) + b1_ref[...])
    g2 = _sigmoid(_dot(h, w2_ref[...]) + b2_ref[...])
    o_ref[...] = (g1 * yml_ref[...].astype(F32) + g2 * yssm_ref[...].astype(F32)).astype(o_ref.dtype)


def _gate_call(h, w_gate, b_gate, y_ml, y_ssm, n_lat):
    d = h.shape[1]
    tm = _pick(n_lat, (1024, 512, 256))
    tn = _pick(d, (512, 256, 128))
    nj = d // tn
    tile = pl.BlockSpec((tm, tn), lambda i, j: (i, j))
    return pl.pallas_call(
        _gate_kernel,
        grid=(n_lat // tm, nj),
        in_specs=[pl.BlockSpec((tm, d), lambda i, j: (i, 0)),
                  pl.BlockSpec((d, tn), lambda i, j: (0, j)),
                  pl.BlockSpec((d, tn), lambda i, j: (0, nj + j)),
                  pl.BlockSpec((1, tn), lambda i, j: (0, j)),
                  pl.BlockSpec((1, tn), lambda i, j: (0, nj + j)),
                  tile, tile],
        out_specs=tile,
        out_shape=jax.ShapeDtypeStruct((n_lat, d), BF16),
        compiler_params=_cparams(("arbitrary", "arbitrary")),
    )(h, w_gate, w_gate, b_gate, b_gate, y_ml, y_ssm)


def _out_kernel(a_ref, w_ref, x_ref, g_ref, o_ref):
    o_ref[...] = x_ref[...] + g_ref[...] * _dot(a_ref[...], w_ref[...])


def _out_call(mix, w_out, x1, gate, n_lat):
    d = mix.shape[1]
    tm = _pick(n_lat, (1024, 512, 256))
    tn = _pick(d, (512, 256, 128))
    tile = pl.BlockSpec((tm, tn), lambda i, j: (i, j))
    return pl.pallas_call(
        _out_kernel,
        grid=(n_lat // tm, d // tn),
        in_specs=[pl.BlockSpec((tm, d), lambda i, j: (i, 0)),
                  pl.BlockSpec((d, tn), lambda i, j: (0, j)),
                  tile,
                  pl.BlockSpec((1, tn), lambda i, j: (0, j))],
        out_specs=tile,
        out_shape=jax.ShapeDtypeStruct((n_lat, d), F32),
        compiler_params=_cparams(("arbitrary", "arbitrary")),
    )(mix, w_out, x1, gate)


def _pad_rows(a, rows):
    return jnp.concatenate([a, jnp.zeros((rows - a.shape[0],) + a.shape[1:], a.dtype)], axis=0)


def _pad_cols(a, cols):
    return jnp.concatenate([a, jnp.zeros(a.shape[:-1] + (cols - a.shape[-1],), a.dtype)], axis=-1)


def kernel(x, c, ctx, c_ctx, w_ada, b_ada, norm_w, ffn_w_gate, ffn_w_up, ffn_w_down, w_in, ml_conv_w, ml_conv_b, ml_gate_b, ml_norm_w, w_proj_ml, ssm_conv_w, ssm_conv_b, ssm_dt_bias, ssm_a_log, ssm_d, ssm_norm_w, w_proj_ssm, w_gate, b_gate, w_out, final_norm_w):
    depth = w_ada.shape[0]
    assert depth == 1 and x.shape[0] == 1, "single layer, single batch element"
    t, d = x.shape[1], x.shape[2]
    n_tot = t + ctx.shape[1]
    rows_cm = t // GRID_W
    ml_qk = ml_conv_w.shape[2] // 2
    ml_v = ml_norm_w.shape[1]
    n_ml = 2 * ml_qk + 2 * ml_v
    ml_cols = n_ml + 4 * ML_HEADS
    ssm_inner = ssm_norm_w.shape[1]
    ssm_heads = ssm_d.shape[1]
    ssm_xbc = ssm_conv_w.shape[2]

    cc = _pad_rows(jnp.concatenate([c, c_ctx[None, :]], axis=0), 8)
    mod = _mod_call(cc, w_ada[0], b_ada[0])
    ml, mc = mod[0].reshape(N_MOD, d), mod[1].reshape(N_MOD, d)
    nw = norm_w[0]
    zero = jnp.zeros_like(nw[0])
    vec1 = _pad_rows(jnp.stack([nw[0], ml[0], ml[1], ml[2], mc[0], mc[1], mc[2],
                                nw[1], ml[3], ml[4], zero, mc[3], mc[4]]), 16)
    vec2 = _pad_rows(jnp.stack([nw[2], ml[6], ml[7], ml[8], zero, zero, zero, final_norm_w]), 16)
    wg, wu, wd = ffn_w_gate.astype(BF16), ffn_w_up.astype(BF16), ffn_w_down.astype(BF16)

    xall = jnp.concatenate([x[0], ctx[0]], axis=0)
    x1, h = _ffn_call(xall, vec1, wg, wu, wd, 0, t, final=False)

    h_cm = jnp.concatenate(
        [h[:t].reshape(rows_cm, GRID_W, d).transpose(1, 0, 2).reshape(t, d), h[t:]], axis=0)
    w = w_in[0]
    n_ssm = ssm_inner + ssm_xbc
    w_ml = w[:, :n_ml].astype(BF16)
    w_ssm = w[:, ml_cols:ml_cols + n_ssm].astype(BF16)
    w_gdt = jnp.concatenate([_pad_cols(w[:, n_ml:ml_cols], LANES), w[:, ml_cols + n_ssm:]],
                            axis=1).astype(BF16)
    ml_cw = _pad_rows(ml_conv_w[0], 8)
    ml_post = jnp.concatenate([jnp.full((1, ml_qk), (ml_qk // ML_HEADS) ** -0.5, F32),
                               jnp.ones((1, ml_qk), F32)], axis=1)
    qk = _mmconv_call(h_cm, w_ml, 0, ml_cw, ml_conv_b[0][None, :], ml_post, t)
    vo = _mm_call(h_cm, w_ml, 2 * ml_qk, 2 * ml_v, n_tot, BF16)
    z = _mm_call(h, w_ssm, 0, ssm_inner, t, BF16)
    xbc = _mmconv_call(h, w_ssm, ssm_inner, _pad_rows(ssm_conv_w[0], 8), ssm_conv_b[0][None, :],
                       jnp.ones((1, ssm_xbc), F32), t)
    gates = _mm_call(h_cm, w_gdt, 0, LANES, n_tot, F32)
    dt = _mm_call(h, w_gdt, LANES, 2 * ssm_heads, n_tot, F32)

    gate_b = _pad_cols(ml_gate_b[0].reshape(1, 4 * ML_HEADS), LANES)
    h_f = _mlstm_call(qk, vo, gates, gate_b, t, reverse=False)
    h_b = _mlstm_call(qk, vo, gates, gate_b, t, reverse=True)
    y_ml = _mlproj_call(h_f, h_b, vo, ml_norm_w[0][None, :], w_proj_ml[0].astype(BF16), t)
    y_ml = y_ml.reshape(GRID_W, rows_cm, d).transpose(1, 0, 2).reshape(t, d)

    dt_bias = ssm_dt_bias[0].reshape(1, 2 * ssm_heads)
    a_neg = -jnp.exp(ssm_a_log[0].astype(F32)).reshape(1, 2 * ssm_heads)
    d_skip = jnp.repeat(ssm_d[0], SSM_HEADDIM)[None, :]
    y_f = _ssd_call(xbc, dt, dt_bias, a_neg, d_skip, t, reverse=False)
    y_b = _ssd_call(xbc, dt, dt_bias, a_neg, d_skip, t, reverse=True)
    y_ssm = _ssmproj_call(y_f, y_b, z, ssm_norm_w[0][None, :], w_proj_ssm[0].astype(BF16), t)

    mix = _gate_call(h, w_gate[0].astype(BF16), b_gate[0][None, :], y_ml, y_ssm, t)
    x2 = _out_call(mix, w_out[0].astype(BF16), x1, ml[5][None, :], t)
    out = _ffn_call(x2, vec2, wg, wu, wd, 1, t, final=True)
    return out[None]
```

```python
import functools

import jax
import jax.numpy as jnp
from jax import lax
from jax.experimental import pallas as pl
from jax.experimental.pallas import tpu as pltpu

F32 = jnp.float32
BF16 = jnp.bfloat16

GRID_W = 64
N_MOD = 9
EPS = 1e-6
CONV_W = 5
CONV_PAD = CONV_W // 2
HALO = 16
CONV_UNIT = 256
ROW_BLK = 16
FFN_ROW_BLK = 64
ML_HEADS = 8
ML_L = 256
SSM_HEADDIM = 64
SSM_GROUPS = 8
SSM_DSTATE = 128
SSM_L = 128
LANES = 128
VMEM_LIMIT_BYTES = 56 * 1024 * 1024
NEG_INF = float("-inf")
LOG2E = 1.4426950408889634


def _cparams(sem):
    return pltpu.CompilerParams(dimension_semantics=sem, vmem_limit_bytes=VMEM_LIMIT_BYTES)


def _pick(n, cands):
    for c in cands:
        if n % c == 0:
            return c
    raise ValueError(f"no tile for {n} in {cands}")


def _silu(x):
    return x * (1.0 / (1.0 + jnp.exp(-x)))


def _sigmoid(x):
    return 1.0 / (1.0 + jnp.exp(-x))


def _log_sigmoid(x):
    return jnp.minimum(x, 0.0) - jnp.log1p(jnp.exp(-jnp.abs(x)))


def _softplus(x):
    return jnp.maximum(x, 0.0) + jnp.log1p(jnp.exp(-jnp.abs(x)))


def _rms(x):
    return x * lax.rsqrt(jnp.mean(jnp.square(x), axis=-1, keepdims=True) + EPS)


def _dot(a, b):
    return jnp.dot(a, b, preferred_element_type=F32)


def _dot_nt(a, b):
    return lax.dot_general(a, b, (((1,), (1,)), ((), ())), preferred_element_type=F32)


def _dot_exact(a, b):
    return jnp.dot(a, b, preferred_element_type=F32, precision=lax.Precision.HIGHEST)


def _row_loop(n_rows, blk, body):
    def step(r, carry):
        body(pl.ds(pl.multiple_of(r * blk, blk), blk))
        return carry
    lax.fori_loop(0, n_rows // blk, step, 0, unroll=2)


def _mod_kernel(c_ref, w_ref, b_ref, o_ref):
    a = _silu(c_ref[...]).astype(BF16)
    o_ref[...] = _dot(a, w_ref[...].astype(BF16)) + b_ref[...]


def _mod_call(cc, w_ada, b_ada):
    k, n = w_ada.shape
    tn = _pick(n, (1024, 512, 256, 128))
    return pl.pallas_call(
        _mod_kernel,
        grid=(n // tn,),
        in_specs=[pl.BlockSpec((8, k), lambda j: (0, 0)),
                  pl.BlockSpec((k, tn), lambda j: (0, j)),
                  pl.BlockSpec((1, tn), lambda j: (0, j))],
        out_specs=pl.BlockSpec((8, tn), lambda j: (0, j)),
        out_shape=jax.ShapeDtypeStruct((8, n), F32),
        compiler_params=_cparams(("arbitrary",)),
    )(cc, w_ada, b_ada.reshape(1, n))


def _ffn_kernel(x_ref, vec_ref, wg_ref, wu_ref, wd_ref, *rest, n_lat, final):
    if final:
        o_ref, h_s, acc_s = rest
    else:
        x1_ref, h2_ref, h_s, acc_s = rest
    i = pl.program_id(0)
    f = pl.program_id(1)
    tm = x_ref.shape[0]

    def vec(row, rows):
        off = jnp.where(i * tm + rows.start >= n_lat, 3, 0)
        return vec_ref[pl.ds(row + off, 1), :]

    @pl.when(f == 0)
    def _():
        def body(rows):
            h = _rms(x_ref[rows, :]) * (vec_ref[0:1, :] * (1.0 + vec(2, rows))) + vec(1, rows)
            h_s[rows, :] = h.astype(BF16)
            acc_s[rows, :] = jnp.zeros((FFN_ROW_BLK, acc_s.shape[1]), F32)
        _row_loop(tm, FFN_ROW_BLK, body)

    h = h_s[...]
    a = _silu(_dot(h, wg_ref[...])) * _dot(h, wu_ref[...])
    acc_s[...] += _dot(a.astype(BF16), wd_ref[...])

    @pl.when(f == pl.num_programs(1) - 1)
    def _():
        def body(rows):
            x1 = x_ref[rows, :] + (0.5 * vec(3, rows)) * acc_s[rows, :]
            if final:
                o_ref[rows, :] = _rms(x1) * vec_ref[7:8, :]
            else:
                x1_ref[rows, :] = x1
                h2 = _rms(x1) * (vec_ref[7:8, :] * (1.0 + vec(9, rows))) + vec(8, rows)
                h2_ref[rows, :] = h2.astype(BF16)
        _row_loop(tm, FFN_ROW_BLK, body)


def _ffn_call(x, vecs, wg, wu, wd, which, n_lat, final):
    m, d = x.shape
    dff = wg.shape[3]
    tm = _pick(m, (768, 512, 256))
    tf = _pick(dff, (512, 256, 128))
    row = pl.BlockSpec((tm, d), lambda i, f: (i, 0), pipeline_mode=pl.Buffered(1))
    x_row = pl.BlockSpec((tm, d), lambda i, f: (i, 0))
    if final:
        out_shape = jax.ShapeDtypeStruct((m, d), F32)
        out_specs = row
    else:
        out_shape = (jax.ShapeDtypeStruct((m, d), F32), jax.ShapeDtypeStruct((m, d), BF16))
        out_specs = (row, row)
    return pl.pallas_call(
        functools.partial(_ffn_kernel, n_lat=n_lat, final=final),
        grid=(m // tm, dff // tf),
        in_specs=[x_row,
                  pl.BlockSpec(vecs.shape, lambda i, f: (0, 0)),
                  pl.BlockSpec((None, None, d, tf), lambda i, f: (0, which, 0, f)),
                  pl.BlockSpec((None, None, d, tf), lambda i, f: (0, which, 0, f)),
                  pl.BlockSpec((None, None, tf, d), lambda i, f: (0, which, f, 0))],
        out_specs=out_specs,
        out_shape=out_shape,
        scratch_shapes=[pltpu.VMEM((tm, d), BF16), pltpu.VMEM((tm, d), F32)],
        compiler_params=_cparams(("arbitrary", "arbitrary")),
    )(x, vecs, wg, wu, wd)


def _mm_kernel(a_ref, w_ref, o_ref):
    o_ref[...] = _dot(a_ref[...], w_ref[...].astype(BF16)).astype(o_ref.dtype)


def _mm_call(a, w, col0, n, m, out_dtype):
    k = a.shape[1]
    tm = _pick(m, (1024, 768, 512, 256))
    tn = _pick(n, (1024, 512, 256, 128))
    assert col0 % tn == 0
    off = col0 // tn
    return pl.pallas_call(
        _mm_kernel,
        grid=(m // tm, n // tn),
        in_specs=[pl.BlockSpec((tm, k), lambda i, j: (i, 0)),
                  pl.BlockSpec((k, tn), lambda i, j: (0, off + j))],
        out_specs=pl.BlockSpec((tm, tn), lambda i, j: (i, j)),
        out_shape=jax.ShapeDtypeStruct((m, n), out_dtype),
        compiler_params=_cparams(("arbitrary", "arbitrary")),
    )(a, w)


def _mmconv_kernel(a_ref, ap_ref, an_ref, w_ref, cw_ref, cb_ref, *rest, n_lat, n_tot, scaled):
    if scaled:
        post_ref, o_ref, xs, ys = rest
    else:
        o_ref, xs, ys = rest
    i = pl.program_id(0)
    tm = a_ref.shape[0]
    tn = o_ref.shape[1]
    units = tm // CONV_UNIT
    ug = CONV_UNIT // 8
    span = CONV_UNIT + 4 * 8
    seq_units = n_lat // CONV_UNIT
    last_unit = n_tot // CONV_UNIT - 1

    w = w_ref[...].astype(BF16)
    p = _dot(a_ref[...], w)
    p_prev = _dot(ap_ref[...], w)
    p_next = _dot(an_ref[...], w)
    sub = lax.broadcasted_iota(jnp.int32, (8, 1), 0)

    for u in range(units):
        g = i * units + u
        keep_prev = jnp.where((g == 0) | (g == seq_units), 0.0, 1.0)
        keep_next = jnp.where((g == last_unit) | (g == seq_units - 1), 0.0, 1.0)
        r0 = u * CONV_UNIT
        base = u * span
        xs[base + 16:base + 16 + CONV_UNIT, :] = p[r0:r0 + CONV_UNIT]
        before = p[r0 - 16:r0] if u > 0 else p_prev
        after = p[r0 + CONV_UNIT:r0 + CONV_UNIT + 16] if u < units - 1 else p_next
        for d in range(2):
            own = pltpu.roll(p[r0 + (ug - 2 + d) * 8:r0 + (ug - 1 + d) * 8], 1, 0)
            other = pltpu.roll(before[d * 8:(d + 1) * 8], 1, 0) * keep_prev
            xs[base + d * 8:base + (d + 1) * 8, :] = jnp.where(sub == 0, other, own)
            own = pltpu.roll(p[r0 + d * 8:r0 + (d + 1) * 8], 7, 0)
            other = pltpu.roll(after[d * 8:(d + 1) * 8], 7, 0) * keep_next
            xs[base + 16 + CONV_UNIT + d * 8:base + 16 + CONV_UNIT + (d + 1) * 8, :] = (
                jnp.where(sub == 7, other, own))

        acc = cb_ref[...]
        for j in range(CONV_W):
            acc = acc + cw_ref[j:j + 1, :] * xs[base + 8 * j:base + 8 * j + CONV_UNIT, :]
        y = _silu(acc)
        if scaled:
            y = y * post_ref[...]
        for cblk in range(tn // LANES):
            lanes = slice(cblk * LANES, (cblk + 1) * LANES)
            ys[cblk, r0:r0 + CONV_UNIT, :] = y[:, lanes]
            for b8 in range(8):
                o_ref[r0 + b8 * ug:r0 + (b8 + 1) * ug, lanes] = (
                    ys[cblk, pl.ds(r0 + b8, ug, stride=8), :].astype(o_ref.dtype))


def _mmconv_call(a, w, col0, conv_w, conv_b, post, n_lat):
    m, k = a.shape
    n = conv_w.shape[1]
    tm = _pick(m, (768, 512, 256))
    tn = _pick(n, (512, 256, 128))
    assert col0 % tn == 0 and tm % CONV_UNIT == 0 and n_lat % CONV_UNIT == 0 and m % CONV_UNIT == 0
    off = col0 // tn
    hb = tm // HALO
    last_halo = m // HALO - 1
    vec = pl.BlockSpec((1, tn), lambda i, j: (0, j))
    scaled = post is not None
    return pl.pallas_call(
        functools.partial(_mmconv_kernel, n_lat=n_lat, n_tot=m, scaled=scaled),
        grid=(m // tm, n // tn),
        in_specs=[pl.BlockSpec((tm, k), lambda i, j: (i, 0)),
                  pl.BlockSpec((HALO, k), lambda i, j: (jnp.maximum(i * hb - 1, 0), 0)),
                  pl.BlockSpec((HALO, k), lambda i, j: (jnp.minimum((i + 1) * hb, last_halo), 0)),
                  pl.BlockSpec((k, tn), lambda i, j: (0, off + j)),
                  pl.BlockSpec((8, tn), lambda i, j: (0, j)),
                  vec] + ([vec] if scaled else []),
        out_specs=pl.BlockSpec((tm, tn), lambda i, j: (i, j)),
        out_shape=jax.ShapeDtypeStruct((m, n), BF16),
        scratch_shapes=[pltpu.VMEM(((tm // CONV_UNIT) * (CONV_UNIT + 32), tn), F32),
                        pltpu.VMEM((tn // LANES, tm, LANES), F32)],
        compiler_params=_cparams(("arbitrary", "arbitrary")),
    )(a, a, a, w, conv_w, conv_b, *([post] if scaled else []))


def _unit_permute(x):
    rows, d = x.shape
    ug = CONV_UNIT // 8
    return x.reshape(rows // CONV_UNIT, 8, ug, d).transpose(0, 2, 1, 3).reshape(rows, d)


def _chunk_order(c, n_lat_chunks, n_ctx_chunks, reverse):
    if reverse:
        return n_lat_chunks + n_ctx_chunks - 1 - c
    return jnp.where(c < n_ctx_chunks, n_lat_chunks + c, c - n_ctx_chunks)


def _tri(chunk, reverse):
    r = lax.broadcasted_iota(jnp.int32, (chunk, chunk), 0)
    c = lax.broadcasted_iota(jnp.int32, (chunk, chunk), 1)
    return (c >= r) if reverse else (c <= r)


def _mlstm_kernel(q_ref, k_ref, v_ref, g_ref, gn_ref, gb_ref, o_ref, ct_s, n_s, m_s, bc_s, bct_s, gt_s, *, reverse):
    L = q_ref.shape[0]
    dk = q_ref.shape[1] // ML_HEADS
    dv = v_ref.shape[1] // ML_HEADS
    c = pl.program_id(0)

    mask = _tri(L, reverse)

    def prefix(g_blk):
        g = g_blk + gb_ref[...]
        bc = _dot_exact(mask.astype(F32), _log_sigmoid(g))
        bc_s[...] = bc
        bct_s[...] = bc.T
        gt_s[...] = g.T

    @pl.when(c == 0)
    def _():
        ct_s[...] = jnp.zeros_like(ct_s)
        n_s[...] = jnp.zeros_like(n_s)
        m_s[...] = jnp.zeros_like(m_s)
        prefix(g_ref[...])

    def lanes(x, n):
        return jnp.concatenate([x] * (n // LANES), axis=1)

    bc = bc_s[...]
    bc_t = bct_s[...]
    g_t = gt_s[...]
    end = 0 if reverse else L - 1
    base = 2 * ML_HEADS if reverse else 0
    ones = jnp.ones((L, LANES), BF16)

    for h in range(ML_HEADS):
        io = base + h
        fo = base + ML_HEADS + h
        bb = jnp.broadcast_to(bc[:, fo:fo + 1], (L, LANES))
        b_row = bc_t[fo:fo + 1, :]
        ig_row = g_t[io:io + 1, :]
        b_end = bc[end:end + 1, fo:fo + 1]
        m0 = m_s[h:h + 1, :]
        q = q_ref[:, h * dk:(h + 1) * dk]
        k = k_ref[:, h * dk:(h + 1) * dk]
        v = v_ref[:, h * dv:(h + 1) * dv]
        ct0 = ct_s[h]
        n0 = n_s[h]

        logw = jnp.where(mask, lanes(bb, L) + (ig_row - b_row), NEG_INF)
        m_inter = bb + m0
        m_intra = jnp.broadcast_to(jnp.max(logw, axis=1, keepdims=True), (L, LANES))
        m_out = jnp.maximum(m_inter, m_intra)
        p = (_dot_nt(q, k) * jnp.exp(logw - lanes(m_out, L))).astype(BF16)
        s_inter = jnp.exp(m_inter - m_out)
        num = _dot(p, v) + lanes(s_inter, dv) * _dot(q, ct0.astype(BF16))
        den = _dot(p, ones) + s_inter * _dot(q, n0.astype(BF16))
        inv = 1.0 / jnp.maximum(jnp.abs(den), jnp.exp(-m_out))
        o_ref[:, h * dv:(h + 1) * dv] = (num * lanes(inv, dv)).astype(o_ref.dtype)

        a_row = b_end + ig_row - b_row
        m_loc = jnp.max(a_row, axis=1, keepdims=True)
        kw_t = (k.T.astype(F32) * jnp.exp(a_row - m_loc)).astype(BF16)
        m_new = jnp.maximum(b_end + m0, m_loc)
        s_old = jnp.exp(b_end + m0 - m_new)
        s_new = jnp.exp(m_loc - m_new)
        ct_s[h] = lanes(s_old, dv) * ct0 + lanes(s_new, dv) * _dot(kw_t, v)
        n_s[h] = s_old * n0 + s_new * _dot(kw_t, ones)
        m_s[h:h + 1, :] = m_new

    prefix(gn_ref[...])


def _mlstm_call(qk, vo, gates, gate_b, n_lat, reverse):
    rows = qk.shape[0]
    L = ML_L
    nqk = qk.shape[1] // 2
    nv = vo.shape[1] // 2
    dk = nqk // ML_HEADS
    dv = nv // ML_HEADS
    nlc = n_lat // L
    ncc = (rows - n_lat) // L
    order = functools.partial(_chunk_order, n_lat_chunks=nlc, n_ctx_chunks=ncc, reverse=reverse)
    return pl.pallas_call(
        functools.partial(_mlstm_kernel, reverse=reverse),
        grid=(nlc + ncc,),
        in_specs=[pl.BlockSpec((L, nqk), lambda c: (order(c), 0)),
                  pl.BlockSpec((L, nqk), lambda c: (order(c), 1)),
                  pl.BlockSpec((L, nv), lambda c: (order(c), 0)),
                  pl.BlockSpec((L, LANES), lambda c: (order(c), 0)),
                  pl.BlockSpec((L, LANES), lambda c: (order(jnp.minimum(c + 1, nlc + ncc - 1)), 0)),
                  pl.BlockSpec((1, LANES), lambda c: (0, 0))],
        out_specs=pl.BlockSpec((L, nv), lambda c: (order(c), 0)),
        out_shape=jax.ShapeDtypeStruct((rows, nv), BF16),
        scratch_shapes=[pltpu.VMEM((ML_HEADS, dk, dv), F32),
                        pltpu.VMEM((ML_HEADS, dk, LANES), F32),
                        pltpu.VMEM((ML_HEADS, LANES), F32),
                        pltpu.VMEM((L, LANES), F32), pltpu.VMEM((LANES, L), F32), pltpu.VMEM((LANES, L), F32)],
        compiler_params=_cparams(("arbitrary",)),
    )(qk, qk, vo, gates, gates, gate_b)


def _ssd_kernel(x_ref, bm_ref, cm_ref, dt_ref, dtn_ref, dtb_ref, a_ref, dsk_ref, y_ref,
                st_s, bc_s, bct_s, dtt_s, *, reverse):
    L = x_ref.shape[0]
    nx = x_ref.shape[1]
    c = pl.program_id(0)
    mask = _tri(L, reverse)

    def prefix(dt_blk):
        dtv = _softplus(dt_blk + dtb_ref[...])
        bc = _dot_exact(mask.astype(F32), dtv * a_ref[...]) * LOG2E
        bc_s[...] = bc
        bct_s[...] = bc.T
        dtt_s[...] = jnp.log2(dtv).T

    @pl.when(c == 0)
    def _():
        st_s[...] = jnp.zeros_like(st_s)
        prefix(dt_ref[...])

    bc = bc_s[...]
    bc_t = bct_s[...]
    ldt_t = dtt_s[...]
    end = 0 if reverse else L - 1
    nheads = nx // SSM_HEADDIM
    hpg = nheads // SSM_GROUPS
    base = nheads if reverse else 0
    pw = 2 * SSM_HEADDIM
    lo = lax.broadcasted_iota(jnp.int32, (1, pw), 1) < SSM_HEADDIM
    lo_b = jnp.where(lo, 1.0, 0.0).astype(BF16)
    hi_b = jnp.where(lo, 0.0, 1.0).astype(BF16)

    for g in range(SSM_GROUPS):
        bm = bm_ref[:, g * SSM_DSTATE:(g + 1) * SSM_DSTATE]
        cm = cm_ref[:, g * SSM_DSTATE:(g + 1) * SSM_DSTATE]
        cb = _dot_nt(cm, bm)
        bm_t = bm.astype(F32).T
        for pr in range(hpg // 2):
            col = (g * hpg + 2 * pr) * SSM_HEADDIM
            xp = x_ref[:, col:col + pw]
            ms, ebs, bts, decs = [], [], [], []
            for e in range(2):
                co = base + g * hpg + 2 * pr + e
                bb = jnp.broadcast_to(bc[:, co:co + 1], (L, L))
                r_row = bc_t[co:co + 1, :] - ldt_t[co:co + 1, :]
                b_end = bc[end:end + 1, co:co + 1]
                ms.append((cb * jnp.exp2(jnp.where(mask, bb - r_row, NEG_INF))).astype(BF16))
                ebs.append(jnp.exp2(bb[:, 0:pw]))
                bts.append((bm_t * jnp.exp2(b_end - r_row)).astype(BF16))
                decs.append(jnp.exp2(b_end))
            x_lo = xp * lo_b
            x_hi = xp * hi_b
            s0 = st_s[:, col:col + pw]
            y = _dot(ms[0], x_lo) + _dot(ms[1], x_hi)
            y = y + jnp.where(lo, ebs[0], ebs[1]) * _dot(cm, s0.astype(BF16))
            if not reverse:
                y = y + dsk_ref[:, col:col + pw] * xp.astype(F32)
            y_ref[:, col:col + pw] = y.astype(y_ref.dtype)
            st_s[:, col:col + pw] = (s0 * jnp.where(lo, decs[0], decs[1])
                                     + _dot(bts[0], x_lo) + _dot(bts[1], x_hi))

    prefix(dtn_ref[...])


def _ssd_call(xbc, dt, dt_bias, a_neg, d_skip, n_lat, reverse):
    rows = xbc.shape[0]
    L = SSM_L
    assert L >= 2 * SSM_HEADDIM
    nb = SSM_GROUPS * SSM_DSTATE
    nx = xbc.shape[1] - 2 * nb
    nlc = n_lat // L
    ncc = (rows - n_lat) // L
    order = functools.partial(_chunk_order, n_lat_chunks=nlc, n_ctx_chunks=ncc, reverse=reverse)
    xblk = nx // nb
    nc = nlc + ncc
    return pl.pallas_call(
        functools.partial(_ssd_kernel, reverse=reverse),
        grid=(nc,),
        in_specs=[pl.BlockSpec((L, nx), lambda c: (order(c), 0)),
                  pl.BlockSpec((L, nb), lambda c: (order(c), xblk)),
                  pl.BlockSpec((L, nb), lambda c: (order(c), xblk + 1)),
                  pl.BlockSpec((L, LANES), lambda c: (order(c), 0)),
                  pl.BlockSpec((L, LANES), lambda c: (order(jnp.minimum(c + 1, nc - 1)), 0)),
                  pl.BlockSpec((1, LANES), lambda c: (0, 0)),
                  pl.BlockSpec((1, LANES), lambda c: (0, 0)),
                  pl.BlockSpec((1, nx), lambda c: (0, 0))],
        out_specs=pl.BlockSpec((L, nx), lambda c: (order(c), 0)),
        out_shape=jax.ShapeDtypeStruct((rows, nx), BF16),
        scratch_shapes=[pltpu.VMEM((SSM_DSTATE, nx), F32),
                        pltpu.VMEM((L, LANES), F32), pltpu.VMEM((LANES, L), F32), pltpu.VMEM((LANES, L), F32)],
        compiler_params=_cparams(("arbitrary",)),
    )(xbc, xbc, xbc, dt, dt, dt_bias, a_neg, d_skip)


def _mlproj_kernel(hf_ref, hb_ref, o_ref, nw_ref, w_ref, y_ref, a_s):
    @pl.when(pl.program_id(1) == 0)
    def _():
        dv = a_s.shape[1] // ML_HEADS

        def body(rows):
            for h in range(ML_HEADS):
                sl = slice(h * dv, (h + 1) * dv)
                hh = _rms(hf_ref[rows, sl].astype(F32) + hb_ref[rows, sl].astype(F32)) * nw_ref[:, sl]
                a_s[rows, sl] = (_sigmoid(o_ref[rows, sl].astype(F32)) * hh).astype(BF16)
        _row_loop(a_s.shape[0], ROW_BLK, body)
    y_ref[...] = _dot(a_s[...], w_ref[...]).astype(y_ref.dtype)


def _mlproj_call(hf, hb, vo, norm_w, w, n_lat):
    nv = hf.shape[1]
    n = w.shape[1]
    tm = _pick(n_lat, (1024, 512, 256))
    tn = _pick(n, (512, 256, 128))
    row = pl.BlockSpec((tm, nv), lambda i, j: (i, 0))
    return pl.pallas_call(
        _mlproj_kernel,
        grid=(n_lat // tm, n // tn),
        in_specs=[row, row,
                  pl.BlockSpec((tm, nv), lambda i, j: (i, 1)),
                  pl.BlockSpec((1, nv), lambda i, j: (0, 0)),
                  pl.BlockSpec((nv, tn), lambda i, j: (0, j))],
        out_specs=pl.BlockSpec((tm, tn), lambda i, j: (i, j)),
        out_shape=jax.ShapeDtypeStruct((n_lat, n), BF16),
        scratch_shapes=[pltpu.VMEM((tm, nv), BF16)],
        compiler_params=_cparams(("arbitrary", "arbitrary")),
    )(hf, hb, vo, norm_w, w)


def _ssmproj_kernel(yf_ref, yb_ref, z_ref, nw_ref, w_ref, y_ref, a_s):
    @pl.when(pl.program_id(1) == 0)
    def _():
        gw = a_s.shape[1] // SSM_GROUPS

        def body(rows):
            for g in range(SSM_GROUPS):
                sl = slice(g * gw, (g + 1) * gw)
                y = (yf_ref[rows, sl].astype(F32) + yb_ref[rows, sl].astype(F32)) * _silu(z_ref[rows, sl].astype(F32))
                a_s[rows, sl] = (_rms(y) * nw_ref[:, sl]).astype(BF16)
        _row_loop(a_s.shape[0], ROW_BLK, body)
    y_ref[...] = _dot(a_s[...], w_ref[...]).astype(y_ref.dtype)


def _ssmproj_call(yf, yb, z, norm_w, w, n_lat):
    nx = yf.shape[1]
    n = w.shape[1]
    tm = _pick(n_lat, (512, 256))
    tn = _pick(n, (512, 256, 128))
    row = pl.BlockSpec((tm, nx), lambda i, j: (i, 0))
    return pl.pallas_call(
        _ssmproj_kernel,
        grid=(n_lat // tm, n // tn),
        in_specs=[row, row, row,
                  pl.BlockSpec((1, nx), lambda i, j: (0, 0)),
                  pl.BlockSpec((nx, tn), lambda i, j: (0, j))],
        out_specs=pl.BlockSpec((tm, tn), lambda i, j: (i, j)),
        out_shape=jax.ShapeDtypeStruct((n_lat, n), BF16),
        scratch_shapes=[pltpu.VMEM((tm, nx), BF16)],
        compiler_params=_cparams(("arbitrary", "arbitrary")),
    )(yf, yb, z, norm_w, w)


def _gate_kernel(h_ref, w1_ref, w2_ref, b1_ref, b2_ref, yml_ref, yssm_ref, o_ref):
    h = h_ref[...]
    g1 = _sigmoid(_dot(h, w1_ref[...]) + b1_ref[...])
    g2 = _sigmoid(_dot(h, w2_ref[...]) + b2_ref[...])
    o_ref[...] = (g1 * yml_ref[...].astype(F32) + g2 * yssm_ref[...].astype(F32)).astype(o_ref.dtype)


def _gate_call(h, w_gate, b_gate, y_ml, y_ssm, n_lat):
    d = h.shape[1]
    tm = _pick(n_lat, (1024, 512, 256))
    tn = _pick(d, (512, 256, 128))
    nj = d // tn
    tile = pl.BlockSpec((tm, tn), lambda i, j: (i, j))
    return pl.pallas_call(
        _gate_kernel,
        grid=(n_lat // tm, nj),
        in_specs=[pl.BlockSpec((tm, d), lambda i, j: (i, 0)),
                  pl.BlockSpec((d, tn), lambda i, j: (0, j)),
                  pl.BlockSpec((d, tn), lambda i, j: (0, nj + j)),
                  pl.BlockSpec((1, tn), lambda i, j: (0, j)),
                  pl.BlockSpec((1, tn), lambda i, j: (0, nj + j)),
                  tile, tile],
        out_specs=tile,
        out_shape=jax.ShapeDtypeStruct((n_lat, d), BF16),
        compiler_params=_cparams(("arbitrary", "arbitrary")),
    )(h, w_gate, w_gate, b_gate, b_gate, y_ml, y_ssm)


def _out_kernel(a_ref, w_ref, x_ref, g_ref, o_ref):
    o_ref[...] = x_ref[...] + g_ref[...] * _dot(a_ref[...], w_ref[...])


def _out_call(mix, w_out, x1, gate, n_lat):
    d = mix.shape[1]
    tm = _pick(n_lat, (1024, 512, 256))
    tn = _pick(d, (512, 256, 128))
    tile = pl.BlockSpec((tm, tn), lambda i, j: (i, j))
    return pl.pallas_call(
        _out_kernel,
        grid=(n_lat // tm, d // tn),
        in_specs=[pl.BlockSpec((tm, d), lambda i, j: (i, 0)),
                  pl.BlockSpec((d, tn), lambda i, j: (0, j)),
                  tile,
                  pl.BlockSpec((1, tn), lambda i, j: (0, j))],
        out_specs=tile,
        out_shape=jax.ShapeDtypeStruct((n_lat, d), F32),
        compiler_params=_cparams(("arbitrary", "arbitrary")),
    )(mix, w_out, x1, gate)


def _pad_rows(a, rows):
    return jnp.concatenate([a, jnp.zeros((rows - a.shape[0],) + a.shape[1:], a.dtype)], axis=0)


def _pad_cols(a, cols):
    return jnp.concatenate([a, jnp.zeros(a.shape[:-1] + (cols - a.shape[-1],), a.dtype)], axis=-1)


def kernel(x, c, ctx, c_ctx, w_ada, b_ada, norm_w, ffn_w_gate, ffn_w_up, ffn_w_down, w_in, ml_conv_w, ml_conv_b, ml_gate_b, ml_norm_w, w_proj_ml, ssm_conv_w, ssm_conv_b, ssm_dt_bias, ssm_a_log, ssm_d, ssm_norm_w, w_proj_ssm, w_gate, b_gate, w_out, final_norm_w):
    depth = w_ada.shape[0]
    assert depth == 1 and x.shape[0] == 1, "single layer, single batch element"
    t, d = x.shape[1], x.shape[2]
    n_tot = t + ctx.shape[1]
    rows_cm = t // GRID_W
    ml_qk = ml_conv_w.shape[2] // 2
    ml_v = ml_norm_w.shape[1]
    n_ml = 2 * ml_qk + 2 * ml_v
    ml_cols = n_ml + 4 * ML_HEADS
    ssm_inner = ssm_norm_w.shape[1]
    ssm_heads = ssm_d.shape[1]
    ssm_xbc = ssm_conv_w.shape[2]

    cc = _pad_rows(jnp.concatenate([c, c_ctx[None, :]], axis=0), 8)
    mod = _mod_call(cc, w_ada[0], b_ada[0])
    ml, mc = mod[0].reshape(N_MOD, d), mod[1].reshape(N_MOD, d)
    nw = norm_w[0]
    zero = jnp.zeros_like(nw[0])
    vec1 = _pad_rows(jnp.stack([nw[0], ml[0], ml[1], ml[2], mc[0], mc[1], mc[2],
                                nw[1], ml[3], ml[4], zero, mc[3], mc[4]]), 16)
    vec2 = _pad_rows(jnp.stack([nw[2], ml[6], ml[7], ml[8], zero, zero, zero, final_norm_w]), 16)
    wg, wu, wd = ffn_w_gate.astype(BF16), ffn_w_up.astype(BF16), ffn_w_down.astype(BF16)

    xall = jnp.concatenate([x[0], ctx[0]], axis=0)
    x1, h = _ffn_call(xall, vec1, wg, wu, wd, 0, t, final=False)

    h_cm = jnp.concatenate(
        [h[:t].reshape(rows_cm, GRID_W, d).transpose(1, 0, 2).reshape(t, d), h[t:]], axis=0)
    w = w_in[0]
    n_ssm = ssm_inner + ssm_xbc
    w_ssm = w[:, ml_cols:ml_cols + n_ssm].astype(BF16)
    w_dt = w[:, ml_cols + n_ssm:].astype(BF16)
    ml_cw = _pad_rows(ml_conv_w[0], 8)
    ml_post = jnp.concatenate([jnp.full((1, ml_qk), (ml_qk // ML_HEADS) ** -0.5, F32),
                               jnp.ones((1, ml_qk), F32)], axis=1)
    qk = _mmconv_call(_unit_permute(h_cm), w, 0, ml_cw, ml_conv_b[0][None, :], ml_post, t)
    vo = _mm_call(h_cm, w, 2 * ml_qk, 2 * ml_v, n_tot, BF16)
    gates = _mm_call(h_cm, w, n_ml, LANES, n_tot, F32)
    z = _mm_call(h, w_ssm, 0, ssm_inner, t, BF16)
    xbc = _mmconv_call(_unit_permute(h), w_ssm, ssm_inner, _pad_rows(ssm_conv_w[0], 8), ssm_conv_b[0][None, :], None, t)
    dt = _mm_call(h, w_dt, 0, 2 * ssm_heads, n_tot, F32)

    gate_b = _pad_cols(ml_gate_b[0].reshape(1, 4 * ML_HEADS), LANES)
    h_f = _mlstm_call(qk, vo, gates, gate_b, t, reverse=False)
    h_b = _mlstm_call(qk, vo, gates, gate_b, t, reverse=True)
    y_ml = _mlproj_call(h_f, h_b, vo, ml_norm_w[0][None, :], w_proj_ml[0].astype(BF16), t)
    y_ml = y_ml.reshape(GRID_W, rows_cm, d).transpose(1, 0, 2).reshape(t, d)

    dt_bias = ssm_dt_bias[0].reshape(1, 2 * ssm_heads)
    a_neg = -jnp.exp(ssm_a_log[0].astype(F32)).reshape(1, 2 * ssm_heads)
    d_skip = jnp.repeat(ssm_d[0], SSM_HEADDIM)[None, :]
    y_f = _ssd_call(xbc, dt, dt_bias, a_neg, d_skip, t, reverse=False)
    y_b = _ssd_call(xbc, dt, dt_bias, a_neg, d_skip, t, reverse=True)
    y_ssm = _ssmproj_call(y_f, y_b, z, ssm_norm_w[0][None, :], w_proj_ssm[0].astype(BF16), t)

    mix = _gate_call(h, w_gate[0].astype(BF16), b_gate[0][None, :], y_ml, y_ssm, t)
    x2 = _out_call(mix, w_out[0].astype(BF16), x1, ml[5][None, :], t)
    out = _ffn_call(x2, vec2, wg, wu, wd, 1, t, final=True)
    return out[None]
```

```python
import functools

import jax
import jax.numpy as jnp
from jax import lax
from jax.experimental import pallas as pl
from jax.experimental.pallas import tpu as pltpu

F32 = jnp.float32
BF16 = jnp.bfloat16

GRID_W = 64
N_MOD = 9
EPS = 1e-6
CONV_W = 5
CONV_PAD = CONV_W // 2
HALO = 16
CONV_UNIT = 256
ROW_BLK = 16
FFN_ROW_BLK = 64
ML_HEADS = 8
ML_L = 256
SSM_HEADDIM = 64
SSM_GROUPS = 8
SSM_DSTATE = 128
SSM_L = 128
LANES = 128
VMEM_LIMIT_BYTES = 56 * 1024 * 1024
NEG_INF = float("-inf")
LOG2E = 1.4426950408889634


def _cparams(sem):
    return pltpu.CompilerParams(dimension_semantics=sem, vmem_limit_bytes=VMEM_LIMIT_BYTES)


def _pick(n, cands):
    for c in cands:
        if n % c == 0:
            return c
    raise ValueError(f"no tile for {n} in {cands}")


def _silu(x):
    return x * (1.0 / (1.0 + jnp.exp(-x)))


def _sigmoid(x):
    return 1.0 / (1.0 + jnp.exp(-x))


def _log_sigmoid(x):
    return jnp.minimum(x, 0.0) - jnp.log1p(jnp.exp(-jnp.abs(x)))


def _softplus(x):
    return jnp.maximum(x, 0.0) + jnp.log1p(jnp.exp(-jnp.abs(x)))


def _rms(x):
    return x * lax.rsqrt(jnp.mean(jnp.square(x), axis=-1, keepdims=True) + EPS)


def _dot(a, b):
    return jnp.dot(a, b, preferred_element_type=F32)


def _dot_nt(a, b):
    return lax.dot_general(a, b, (((1,), (1,)), ((), ())), preferred_element_type=F32)


def _dot_exact(a, b):
    return jnp.dot(a, b, preferred_element_type=F32, precision=lax.Precision.HIGHEST)


def _row_loop(n_rows, blk, body):
    def step(r, carry):
        body(pl.ds(pl.multiple_of(r * blk, blk), blk))
        return carry
    lax.fori_loop(0, n_rows // blk, step, 0, unroll=2)


def _mod_kernel(c_ref, w_ref, b_ref, o_ref):
    a = _silu(c_ref[...]).astype(BF16)
    o_ref[...] = _dot(a, w_ref[...].astype(BF16)) + b_ref[...]


def _mod_call(cc, w_ada, b_ada):
    k, n = w_ada.shape
    tn = _pick(n, (1024, 512, 256, 128))
    return pl.pallas_call(
        _mod_kernel,
        grid=(n // tn,),
        in_specs=[pl.BlockSpec((8, k), lambda j: (0, 0)),
                  pl.BlockSpec((k, tn), lambda j: (0, j)),
                  pl.BlockSpec((1, tn), lambda j: (0, j))],
        out_specs=pl.BlockSpec((8, tn), lambda j: (0, j)),
        out_shape=jax.ShapeDtypeStruct((8, n), F32),
        compiler_params=_cparams(("arbitrary",)),
    )(cc, w_ada, b_ada.reshape(1, n))


def _ffn_kernel(x_ref, vec_ref, wg_ref, wu_ref, wd_ref, *rest, n_lat, final):
    if final:
        o_ref, h_s, acc_s = rest
    else:
        x1_ref, h2_ref, h_s, acc_s = rest
    i = pl.program_id(0)
    f = pl.program_id(1)
    tm = x_ref.shape[0]

    def vec(row, rows):
        off = jnp.where(i * tm + rows.start >= n_lat, 3, 0)
        return vec_ref[pl.ds(row + off, 1), :]

    @pl.when(f == 0)
    def _():
        def body(rows):
            h = _rms(x_ref[rows, :]) * (vec_ref[0:1, :] * (1.0 + vec(2, rows))) + vec(1, rows)
            h_s[rows, :] = h.astype(BF16)
            acc_s[rows, :] = jnp.zeros((FFN_ROW_BLK, acc_s.shape[1]), F32)
        _row_loop(tm, FFN_ROW_BLK, body)

    h = h_s[...]
    a = _silu(_dot(h, wg_ref[...])) * _dot(h, wu_ref[...])
    acc_s[...] += _dot(a.astype(BF16), wd_ref[...])

    @pl.when(f == pl.num_programs(1) - 1)
    def _():
        def body(rows):
            x1 = x_ref[rows, :] + (0.5 * vec(3, rows)) * acc_s[rows, :]
            if final:
                o_ref[rows, :] = _rms(x1) * vec_ref[7:8, :]
            else:
                x1_ref[rows, :] = x1
                h2 = _rms(x1) * (vec_ref[7:8, :] * (1.0 + vec(9, rows))) + vec(8, rows)
                h2_ref[rows, :] = h2.astype(BF16)
        _row_loop(tm, FFN_ROW_BLK, body)


def _ffn_call(x, vecs, wg, wu, wd, which, n_lat, final):
    m, d = x.shape
    dff = wg.shape[3]
    tm = _pick(m, (768, 512, 256))
    tf = _pick(dff, (512, 256, 128))
    row = pl.BlockSpec((tm, d), lambda i, f: (i, 0), pipeline_mode=pl.Buffered(1))
    x_row = pl.BlockSpec((tm, d), lambda i, f: (i, 0))
    if final:
        out_shape = jax.ShapeDtypeStruct((m, d), F32)
        out_specs = row
    else:
        out_shape = (jax.ShapeDtypeStruct((m, d), F32), jax.ShapeDtypeStruct((m, d), BF16))
        out_specs = (row, row)
    return pl.pallas_call(
        functools.partial(_ffn_kernel, n_lat=n_lat, final=final),
        grid=(m // tm, dff // tf),
        in_specs=[x_row,
                  pl.BlockSpec(vecs.shape, lambda i, f: (0, 0)),
                  pl.BlockSpec((None, None, d, tf), lambda i, f: (0, which, 0, f)),
                  pl.BlockSpec((None, None, d, tf), lambda i, f: (0, which, 0, f)),
                  pl.BlockSpec((None, None, tf, d), lambda i, f: (0, which, f, 0))],
        out_specs=out_specs,
        out_shape=out_shape,
        scratch_shapes=[pltpu.VMEM((tm, d), BF16), pltpu.VMEM((tm, d), F32)],
        compiler_params=_cparams(("arbitrary", "arbitrary")),
    )(x, vecs, wg, wu, wd)


def _mm_kernel(a_ref, w_ref, o_ref):
    o_ref[...] = _dot_nt(a_ref[...], w_ref[...].astype(BF16)).astype(o_ref.dtype)


def _mm_call(a, w, col0, n, m, out_dtype):
    k = a.shape[1]
    tm = _pick(m, (1024, 768, 512, 256))
    tn = _pick(n, (1024, 512, 256, 128))
    assert col0 % tn == 0
    off = col0 // tn
    return pl.pallas_call(
        _mm_kernel,
        grid=(m // tm, n // tn),
        in_specs=[pl.BlockSpec((tm, k), lambda i, j: (i, 0)),
                  pl.BlockSpec((tn, k), lambda i, j: (off + j, 0))],
        out_specs=pl.BlockSpec((tm, tn), lambda i, j: (i, j)),
        out_shape=jax.ShapeDtypeStruct((m, n), out_dtype),
        compiler_params=_cparams(("arbitrary", "arbitrary")),
    )(a, w)


def _mmconv_kernel(a_ref, ap_ref, an_ref, w_ref, cw_ref, cb_ref, *rest, n_lat, n_tot, scaled):
    if scaled:
        post_ref, o_ref, xs, ys, a_s = rest
    else:
        o_ref, xs, ys, a_s = rest
    i = pl.program_id(0)
    tm = a_ref.shape[0]
    tn = o_ref.shape[1]
    units = tm // CONV_UNIT
    ug = CONV_UNIT // 8
    span = CONV_UNIT + 4 * 8
    seq_units = n_lat // CONV_UNIT
    last_unit = n_tot // CONV_UNIT - 1

    @pl.when(pl.program_id(1) == 0)
    def _():
        a_s[0:HALO, :] = ap_ref[...]
        a_s[HALO:HALO + tm, :] = a_ref[...]
        a_s[HALO + tm:2 * HALO + tm, :] = an_ref[...]

    w = w_ref[...].astype(BF16)
    p_all = _dot_nt(a_s[...], w)
    p_prev = p_all[0:HALO]
    p = p_all[HALO:HALO + tm]
    p_next = p_all[HALO + tm:2 * HALO + tm]
    sub = lax.broadcasted_iota(jnp.int32, (8, 1), 0)

    for u in range(units):
        g = i * units + u
        keep_prev = jnp.where((g == 0) | (g == seq_units), 0.0, 1.0)
        keep_next = jnp.where((g == last_unit) | (g == seq_units - 1), 0.0, 1.0)
        r0 = u * CONV_UNIT
        base = u * span
        xs[base + 16:base + 16 + CONV_UNIT, :] = p[r0:r0 + CONV_UNIT]
        before = p[r0 - 16:r0] if u > 0 else p_prev
        after = p[r0 + CONV_UNIT:r0 + CONV_UNIT + 16] if u < units - 1 else p_next
        for d in range(2):
            own = pltpu.roll(p[r0 + (ug - 2 + d) * 8:r0 + (ug - 1 + d) * 8], 1, 0)
            other = pltpu.roll(before[d * 8:(d + 1) * 8], 1, 0) * keep_prev
            xs[base + d * 8:base + (d + 1) * 8, :] = jnp.where(sub == 0, other, own)
            own = pltpu.roll(p[r0 + d * 8:r0 + (d + 1) * 8], 7, 0)
            other = pltpu.roll(after[d * 8:(d + 1) * 8], 7, 0) * keep_next
            xs[base + 16 + CONV_UNIT + d * 8:base + 16 + CONV_UNIT + (d + 1) * 8, :] = (
                jnp.where(sub == 7, other, own))

        acc = cb_ref[...]
        for j in range(CONV_W):
            acc = acc + cw_ref[j:j + 1, :] * xs[base + 8 * j:base + 8 * j + CONV_UNIT, :]
        y = _silu(acc)
        if scaled:
            y = y * post_ref[...]
        for cblk in range(tn // LANES):
            lanes = slice(cblk * LANES, (cblk + 1) * LANES)
            ys[cblk, r0:r0 + CONV_UNIT, :] = y[:, lanes]
            for b8 in range(8):
                o_ref[r0 + b8 * ug:r0 + (b8 + 1) * ug, lanes] = (
                    ys[cblk, pl.ds(r0 + b8, ug, stride=8), :].astype(o_ref.dtype))


def _mmconv_call(a, w, col0, conv_w, conv_b, post, n_lat):
    m, k = a.shape
    n = conv_w.shape[1]
    tm = _pick(m, (768, 512, 256))
    tn = _pick(n, (512, 256, 128))
    assert col0 % tn == 0 and tm % CONV_UNIT == 0 and n_lat % CONV_UNIT == 0 and m % CONV_UNIT == 0
    off = col0 // tn
    hb = tm // HALO
    last_halo = m // HALO - 1
    vec = pl.BlockSpec((1, tn), lambda i, j: (0, j))
    scaled = post is not None
    return pl.pallas_call(
        functools.partial(_mmconv_kernel, n_lat=n_lat, n_tot=m, scaled=scaled),
        grid=(m // tm, n // tn),
        in_specs=[pl.BlockSpec((tm, k), lambda i, j: (i, 0)),
                  pl.BlockSpec((HALO, k), lambda i, j: (jnp.maximum(i * hb - 1, 0), 0)),
                  pl.BlockSpec((HALO, k), lambda i, j: (jnp.minimum((i + 1) * hb, last_halo), 0)),
                  pl.BlockSpec((tn, k), lambda i, j: (off + j, 0)),
                  pl.BlockSpec((8, tn), lambda i, j: (0, j)),
                  vec] + ([vec] if scaled else []),
        out_specs=pl.BlockSpec((tm, tn), lambda i, j: (i, j)),
        out_shape=jax.ShapeDtypeStruct((m, n), BF16),
        scratch_shapes=[pltpu.VMEM(((tm // CONV_UNIT) * (CONV_UNIT + 32), tn), F32),
                        pltpu.VMEM((tn // LANES, tm, LANES), F32),
                        pltpu.VMEM((tm + 2 * HALO, k), BF16)],
        compiler_params=_cparams(("arbitrary", "arbitrary")),
    )(a, a, a, w, conv_w, conv_b, *([post] if scaled else []))


def _unit_permute(x):
    rows, d = x.shape
    ug = CONV_UNIT // 8
    return x.reshape(rows // CONV_UNIT, 8, ug, d).transpose(0, 2, 1, 3).reshape(rows, d)


def _chunk_order(c, n_lat_chunks, n_ctx_chunks, reverse):
    if reverse:
        return n_lat_chunks + n_ctx_chunks - 1 - c
    return jnp.where(c < n_ctx_chunks, n_lat_chunks + c, c - n_ctx_chunks)


def _tri(chunk, reverse):
    r = lax.broadcasted_iota(jnp.int32, (chunk, chunk), 0)
    c = lax.broadcasted_iota(jnp.int32, (chunk, chunk), 1)
    return (c >= r) if reverse else (c <= r)


def _mlstm_kernel(q_ref, k_ref, v_ref, g_ref, gn_ref, gb_ref, o_ref, ct_s, n_s, m_s, bc_s, bct_s, gt_s, *, reverse):
    L = q_ref.shape[0]
    dk = q_ref.shape[1] // ML_HEADS
    dv = v_ref.shape[1] // ML_HEADS
    c = pl.program_id(0)

    mask = _tri(L, reverse)

    def prefix(g_blk):
        g = g_blk + gb_ref[...]
        bc = _dot_exact(mask.astype(F32), _log_sigmoid(g))
        bc_s[...] = bc
        bct_s[...] = bc.T
        gt_s[...] = g.T

    @pl.when(c == 0)
    def _():
        ct_s[...] = jnp.zeros_like(ct_s)
        n_s[...] = jnp.zeros_like(n_s)
        m_s[...] = jnp.zeros_like(m_s)
        prefix(g_ref[...])

    def lanes(x, n):
        return jnp.concatenate([x] * (n // LANES), axis=1)

    bc = bc_s[...]
    bc_t = bct_s[...]
    g_t = gt_s[...]
    end = 0 if reverse else L - 1
    base = 2 * ML_HEADS if reverse else 0
    ones = jnp.ones((L, LANES), BF16)

    for h in range(ML_HEADS):
        io = base + h
        fo = base + ML_HEADS + h
        bb = jnp.broadcast_to(bc[:, fo:fo + 1], (L, LANES))
        b_row = bc_t[fo:fo + 1, :]
        ig_row = g_t[io:io + 1, :]
        b_end = bc[end:end + 1, fo:fo + 1]
        m0 = m_s[h:h + 1, :]
        q = q_ref[:, h * dk:(h + 1) * dk]
        k = k_ref[:, h * dk:(h + 1) * dk]
        v = v_ref[:, h * dv:(h + 1) * dv]
        ct0 = ct_s[h]
        n0 = n_s[h]

        logw = jnp.where(mask, lanes(bb, L) + (ig_row - b_row), NEG_INF)
        m_inter = bb + m0
        m_intra = jnp.broadcast_to(jnp.max(logw, axis=1, keepdims=True), (L, LANES))
        m_out = jnp.maximum(m_inter, m_intra)
        p = (_dot_nt(q, k) * jnp.exp(logw - lanes(m_out, L))).astype(BF16)
        s_inter = jnp.exp(m_inter - m_out)
        num = _dot(p, v) + lanes(s_inter, dv) * _dot(q, ct0.astype(BF16))
        den = _dot(p, ones) + s_inter * _dot(q, n0.astype(BF16))
        inv = 1.0 / jnp.maximum(jnp.abs(den), jnp.exp(-m_out))
        o_ref[:, h * dv:(h + 1) * dv] = (num * lanes(inv, dv)).astype(o_ref.dtype)

        a_row = b_end + ig_row - b_row
        m_loc = jnp.max(a_row, axis=1, keepdims=True)
        kw_t = (k.T.astype(F32) * jnp.exp(a_row - m_loc)).astype(BF16)
        m_new = jnp.maximum(b_end + m0, m_loc)
        s_old = jnp.exp(b_end + m0 - m_new)
        s_new = jnp.exp(m_loc - m_new)
        ct_s[h] = lanes(s_old, dv) * ct0 + lanes(s_new, dv) * _dot(kw_t, v)
        n_s[h] = s_old * n0 + s_new * _dot(kw_t, ones)
        m_s[h:h + 1, :] = m_new

    prefix(gn_ref[...])


def _mlstm_call(qk, vo, gates, gate_b, n_lat, reverse):
    rows = qk.shape[0]
    L = ML_L
    nqk = qk.shape[1] // 2
    nv = vo.shape[1] // 2
    dk = nqk // ML_HEADS
    dv = nv // ML_HEADS
    nlc = n_lat // L
    ncc = (rows - n_lat) // L
    order = functools.partial(_chunk_order, n_lat_chunks=nlc, n_ctx_chunks=ncc, reverse=reverse)
    return pl.pallas_call(
        functools.partial(_mlstm_kernel, reverse=reverse),
        grid=(nlc + ncc,),
        in_specs=[pl.BlockSpec((L, nqk), lambda c: (order(c), 0)),
                  pl.BlockSpec((L, nqk), lambda c: (order(c), 1)),
                  pl.BlockSpec((L, nv), lambda c: (order(c), 0)),
                  pl.BlockSpec((L, LANES), lambda c: (order(c), 0)),
                  pl.BlockSpec((L, LANES), lambda c: (order(jnp.minimum(c + 1, nlc + ncc - 1)), 0)),
                  pl.BlockSpec((1, LANES), lambda c: (0, 0))],
        out_specs=pl.BlockSpec((L, nv), lambda c: (order(c), 0)),
        out_shape=jax.ShapeDtypeStruct((rows, nv), BF16),
        scratch_shapes=[pltpu.VMEM((ML_HEADS, dk, dv), F32),
                        pltpu.VMEM((ML_HEADS, dk, LANES), F32),
                        pltpu.VMEM((ML_HEADS, LANES), F32),
                        pltpu.VMEM((L, LANES), F32), pltpu.VMEM((LANES, L), F32), pltpu.VMEM((LANES, L), F32)],
        compiler_params=_cparams(("arbitrary",)),
    )(qk, qk, vo, gates, gates, gate_b)


def _ssd_kernel(x_ref, bm_ref, cm_ref, dt_ref, dtn_ref, dtb_ref, a_ref, dsk_ref, y_ref,
                st_s, bc_s, bct_s, dtt_s, *, reverse):
    L = x_ref.shape[0]
    nx = x_ref.shape[1]
    c = pl.program_id(0)
    mask = _tri(L, reverse)

    def prefix(dt_blk):
        dtv = _softplus(dt_blk + dtb_ref[...])
        bc = _dot_exact(mask.astype(F32), dtv * a_ref[...]) * LOG2E
        bc_s[...] = bc
        bct_s[...] = bc.T
        dtt_s[...] = jnp.log2(dtv).T

    @pl.when(c == 0)
    def _():
        st_s[...] = jnp.zeros_like(st_s)
        prefix(dt_ref[...])

    bc = bc_s[...]
    bc_t = bct_s[...]
    ldt_t = dtt_s[...]
    end = 0 if reverse else L - 1
    nheads = nx // SSM_HEADDIM
    hpg = nheads // SSM_GROUPS
    base = nheads if reverse else 0
    pw = 2 * SSM_HEADDIM
    lo = lax.broadcasted_iota(jnp.int32, (1, pw), 1) < SSM_HEADDIM
    lo_b = jnp.where(lo, 1.0, 0.0).astype(BF16)
    hi_b = jnp.where(lo, 0.0, 1.0).astype(BF16)

    for g in range(SSM_GROUPS):
        bm = bm_ref[:, g * SSM_DSTATE:(g + 1) * SSM_DSTATE]
        cm = cm_ref[:, g * SSM_DSTATE:(g + 1) * SSM_DSTATE]
        cb = _dot_nt(cm, bm)
        bm_t = bm.astype(F32).T
        for pr in range(hpg // 2):
            col = (g * hpg + 2 * pr) * SSM_HEADDIM
            xp = x_ref[:, col:col + pw]
            ms, ebs, bts, decs = [], [], [], []
            for e in range(2):
                co = base + g * hpg + 2 * pr + e
                bb = jnp.broadcast_to(bc[:, co:co + 1], (L, L))
                r_row = bc_t[co:co + 1, :] - ldt_t[co:co + 1, :]
                b_end = bc[end:end + 1, co:co + 1]
                ms.append((cb * jnp.exp2(jnp.where(mask, bb - r_row, NEG_INF))).astype(BF16))
                ebs.append(jnp.exp2(bb[:, 0:pw]))
                bts.append((bm_t * jnp.exp2(b_end - r_row)).astype(BF16))
                decs.append(jnp.exp2(b_end))
            x_lo = xp * lo_b
            x_hi = xp * hi_b
            s0 = st_s[:, col:col + pw]
            y = _dot(ms[0], x_lo) + _dot(ms[1], x_hi)
            y = y + jnp.where(lo, ebs[0], ebs[1]) * _dot(cm, s0.astype(BF16))
            if not reverse:
                y = y + dsk_ref[:, col:col + pw] * xp.astype(F32)
            y_ref[:, col:col + pw] = y.astype(y_ref.dtype)
            st_s[:, col:col + pw] = (s0 * jnp.where(lo, decs[0], decs[1])
                                     + _dot(bts[0], x_lo) + _dot(bts[1], x_hi))

    prefix(dtn_ref[...])


def _ssd_call(xbc, dt, dt_bias, a_neg, d_skip, n_lat, reverse):
    rows = xbc.shape[0]
    L = SSM_L
    assert L >= 2 * SSM_HEADDIM
    nb = SSM_GROUPS * SSM_DSTATE
    nx = xbc.shape[1] - 2 * nb
    nlc = n_lat // L
    ncc = (rows - n_lat) // L
    order = functools.partial(_chunk_order, n_lat_chunks=nlc, n_ctx_chunks=ncc, reverse=reverse)
    xblk = nx // nb
    nc = nlc + ncc
    return pl.pallas_call(
        functools.partial(_ssd_kernel, reverse=reverse),
        grid=(nc,),
        in_specs=[pl.BlockSpec((L, nx), lambda c: (order(c), 0)),
                  pl.BlockSpec((L, nb), lambda c: (order(c), xblk)),
                  pl.BlockSpec((L, nb), lambda c: (order(c), xblk + 1)),
                  pl.BlockSpec((L, LANES), lambda c: (order(c), 0)),
                  pl.BlockSpec((L, LANES), lambda c: (order(jnp.minimum(c + 1, nc - 1)), 0)),
                  pl.BlockSpec((1, LANES), lambda c: (0, 0)),
                  pl.BlockSpec((1, LANES), lambda c: (0, 0)),
                  pl.BlockSpec((1, nx), lambda c: (0, 0))],
        out_specs=pl.BlockSpec((L, nx), lambda c: (order(c), 0)),
        out_shape=jax.ShapeDtypeStruct((rows, nx), BF16),
        scratch_shapes=[pltpu.VMEM((SSM_DSTATE, nx), F32),
                        pltpu.VMEM((L, LANES), F32), pltpu.VMEM((LANES, L), F32), pltpu.VMEM((LANES, L), F32)],
        compiler_params=_cparams(("arbitrary",)),
    )(xbc, xbc, xbc, dt, dt, dt_bias, a_neg, d_skip)


def _mlproj_kernel(hf_ref, hb_ref, o_ref, nw_ref, w_ref, y_ref, a_s):
    @pl.when(pl.program_id(1) == 0)
    def _():
        dv = a_s.shape[1] // ML_HEADS

        def body(rows):
            for h in range(ML_HEADS):
                sl = slice(h * dv, (h + 1) * dv)
                hh = _rms(hf_ref[rows, sl].astype(F32) + hb_ref[rows, sl].astype(F32)) * nw_ref[:, sl]
                a_s[rows, sl] = (_sigmoid(o_ref[rows, sl].astype(F32)) * hh).astype(BF16)
        _row_loop(a_s.shape[0], ROW_BLK, body)
    y_ref[...] = _dot(a_s[...], w_ref[...]).astype(y_ref.dtype)


def _mlproj_call(hf, hb, vo, norm_w, w, n_lat):
    nv = hf.shape[1]
    n = w.shape[1]
    tm = _pick(n_lat, (1024, 512, 256))
    tn = _pick(n, (512, 256, 128))
    row = pl.BlockSpec((tm, nv), lambda i, j: (i, 0))
    return pl.pallas_call(
        _mlproj_kernel,
        grid=(n_lat // tm, n // tn),
        in_specs=[row, row,
                  pl.BlockSpec((tm, nv), lambda i, j: (i, 1)),
                  pl.BlockSpec((1, nv), lambda i, j: (0, 0)),
                  pl.BlockSpec((nv, tn), lambda i, j: (0, j))],
        out_specs=pl.BlockSpec((tm, tn), lambda i, j: (i, j)),
        out_shape=jax.ShapeDtypeStruct((n_lat, n), BF16),
        scratch_shapes=[pltpu.VMEM((tm, nv), BF16)],
        compiler_params=_cparams(("arbitrary", "arbitrary")),
    )(hf, hb, vo, norm_w, w)


def _ssmproj_kernel(yf_ref, yb_ref, z_ref, nw_ref, w_ref, y_ref, a_s):
    @pl.when(pl.program_id(1) == 0)
    def _():
        gw = a_s.shape[1] // SSM_GROUPS

        def body(rows):
            for g in range(SSM_GROUPS):
                sl = slice(g * gw, (g + 1) * gw)
                y = (yf_ref[rows, sl].astype(F32) + yb_ref[rows, sl].astype(F32)) * _silu(z_ref[rows, sl].astype(F32))
                a_s[rows, sl] = (_rms(y) * nw_ref[:, sl]).astype(BF16)
        _row_loop(a_s.shape[0], ROW_BLK, body)
    y_ref[...] = _dot(a_s[...], w_ref[...]).astype(y_ref.dtype)


def _ssmproj_call(yf, yb, z, norm_w, w, n_lat):
    nx = yf.shape[1]
    n = w.shape[1]
    tm = _pick(n_lat, (512, 256))
    tn = _pick(n, (512, 256, 128))
    row = pl.BlockSpec((tm, nx), lambda i, j: (i, 0))
    return pl.pallas_call(
        _ssmproj_kernel,
        grid=(n_lat // tm, n // tn),
        in_specs=[row, row, row,
                  pl.BlockSpec((1, nx), lambda i, j: (0, 0)),
                  pl.BlockSpec((nx, tn), lambda i, j: (0, j))],
        out_specs=pl.BlockSpec((tm, tn), lambda i, j: (i, j)),
        out_shape=jax.ShapeDtypeStruct((n_lat, n), BF16),
        scratch_shapes=[pltpu.VMEM((tm, nx), BF16)],
        compiler_params=_cparams(("arbitrary", "arbitrary")),
    )(yf, yb, z, norm_w, w)


def _gate_kernel(h_ref, w1_ref, w2_ref, b1_ref, b2_ref, yml_ref, yssm_ref, o_ref):
    h = h_ref[...]
    g1 = _sigmoid(_dot(h, w1_ref[...]) + b1_ref[...])
    g2 = _sigmoid(_dot(h, w2_ref[...]) + b2_ref[...])
    o_ref[...] = (g1 * yml_ref[...].astype(F32) + g2 * yssm_ref[...].astype(F32)).astype(o_ref.dtype)


def _gate_call(h, w_gate, b_gate, y_ml, y_ssm, n_lat):
    d = h.shape[1]
    tm = _pick(n_lat, (1024, 512, 256))
    tn = _pick(d, (512, 256, 128))
    nj = d // tn
    tile = pl.BlockSpec((tm, tn), lambda i, j: (i, j))
    return pl.pallas_call(
        _gate_kernel,
        grid=(n_lat // tm, nj),
        in_specs=[pl.BlockSpec((tm, d), lambda i, j: (i, 0)),
                  pl.BlockSpec((d, tn), lambda i, j: (0, j)),
                  pl.BlockSpec((d, tn), lambda i, j: (0, nj + j)),
                  pl.BlockSpec((1, tn), lambda i, j: (0, j)),
                  pl.BlockSpec((1, tn), lambda i, j: (0, nj + j)),
                  tile, tile],
        out_specs=tile,
        out_shape=jax.ShapeDtypeStruct((n_lat, d), BF16),
        compiler_params=_cparams(("arbitrary", "arbitrary")),
    )(h, w_gate, w_gate, b_gate, b_gate, y_ml, y_ssm)


def _out_kernel(a_ref, w_ref, x_ref, g_ref, o_ref):
    o_ref[...] = x_ref[...] + g_ref[...] * _dot(a_ref[...], w_ref[...])


def _out_call(mix, w_out, x1, gate, n_lat):
    d = mix.shape[1]
    tm = _pick(n_lat, (1024, 512, 256))
    tn = _pick(d, (512, 256, 128))
    tile = pl.BlockSpec((tm, tn), lambda i, j: (i, j))
    return pl.pallas_call(
        _out_kernel,
        grid=(n_lat // tm, d // tn),
        in_specs=[pl.BlockSpec((tm, d), lambda i, j: (i, 0)),
                  pl.BlockSpec((d, tn), lambda i, j: (0, j)),
                  tile,
                  pl.BlockSpec((1, tn), lambda i, j: (0, j))],
        out_specs=tile,
        out_shape=jax.ShapeDtypeStruct((n_lat, d), F32),
        compiler_params=_cparams(("arbitrary", "arbitrary")),
    )(mix, w_out, x1, gate)


def _pad_rows(a, rows):
    return jnp.concatenate([a, jnp.zeros((rows - a.shape[0],) + a.shape[1:], a.dtype)], axis=0)


def _pad_cols(a, cols):
    return jnp.concatenate([a, jnp.zeros(a.shape[:-1] + (cols - a.shape[-1],), a.dtype)], axis=-1)


def kernel(x, c, ctx, c_ctx, w_ada, b_ada, norm_w, ffn_w_gate, ffn_w_up, ffn_w_down, w_in, ml_conv_w, ml_conv_b, ml_gate_b, ml_norm_w, w_proj_ml, ssm_conv_w, ssm_conv_b, ssm_dt_bias, ssm_a_log, ssm_d, ssm_norm_w, w_proj_ssm, w_gate, b_gate, w_out, final_norm_w):
    depth = w_ada.shape[0]
    assert depth == 1 and x.shape[0] == 1, "single layer, single batch element"
    t, d = x.shape[1], x.shape[2]
    n_tot = t + ctx.shape[1]
    rows_cm = t // GRID_W
    ml_qk = ml_conv_w.shape[2] // 2
    ml_v = ml_norm_w.shape[1]
    n_ml = 2 * ml_qk + 2 * ml_v
    ml_cols = n_ml + 4 * ML_HEADS
    ssm_inner = ssm_norm_w.shape[1]
    ssm_heads = ssm_d.shape[1]
    ssm_xbc = ssm_conv_w.shape[2]

    cc = _pad_rows(jnp.concatenate([c, c_ctx[None, :]], axis=0), 8)
    mod = _mod_call(cc, w_ada[0], b_ada[0])
    ml, mc = mod[0].reshape(N_MOD, d), mod[1].reshape(N_MOD, d)
    nw = norm_w[0]
    zero = jnp.zeros_like(nw[0])
    vec1 = _pad_rows(jnp.stack([nw[0], ml[0], ml[1], ml[2], mc[0], mc[1], mc[2],
                                nw[1], ml[3], ml[4], zero, mc[3], mc[4]]), 16)
    vec2 = _pad_rows(jnp.stack([nw[2], ml[6], ml[7], ml[8], zero, zero, zero, final_norm_w]), 16)
    wg, wu, wd = ffn_w_gate.astype(BF16), ffn_w_up.astype(BF16), ffn_w_down.astype(BF16)

    xall = jnp.concatenate([x[0], ctx[0]], axis=0)
    x1, h = _ffn_call(xall, vec1, wg, wu, wd, 0, t, final=False)

    h_cm = jnp.concatenate(
        [h[:t].reshape(rows_cm, GRID_W, d).transpose(1, 0, 2).reshape(t, d), h[t:]], axis=0)
    wt = w_in[0].T
    n_ssm = ssm_inner + ssm_xbc
    wt_ssm = wt[ml_cols:ml_cols + n_ssm].astype(BF16)
    wt_dt = wt[ml_cols + n_ssm:].astype(BF16)
    ml_cw = _pad_rows(ml_conv_w[0], 8)
    ml_post = jnp.concatenate([jnp.full((1, ml_qk), (ml_qk // ML_HEADS) ** -0.5, F32),
                               jnp.ones((1, ml_qk), F32)], axis=1)
    qk = _mmconv_call(_unit_permute(h_cm), wt, 0, ml_cw, ml_conv_b[0][None, :], ml_post, t)
    vo = _mm_call(h_cm, wt, 2 * ml_qk, 2 * ml_v, n_tot, BF16)
    gates = _mm_call(h_cm, wt, n_ml, LANES, n_tot, F32)
    z = _mm_call(h, wt_ssm, 0, ssm_inner, t, BF16)
    xbc = _mmconv_call(_unit_permute(h), wt_ssm, ssm_inner, _pad_rows(ssm_conv_w[0], 8), ssm_conv_b[0][None, :], None, t)
    dt = _mm_call(h, wt_dt, 0, 2 * ssm_heads, n_tot, F32)

    gate_b = _pad_cols(ml_gate_b[0].reshape(1, 4 * ML_HEADS), LANES)
    h_f = _mlstm_call(qk, vo, gates, gate_b, t, reverse=False)
    h_b = _mlstm_call(qk, vo, gates, gate_b, t, reverse=True)
    y_ml = _mlproj_call(h_f, h_b, vo, ml_norm_w[0][None, :], w_proj_ml[0].astype(BF16), t)
    y_ml = y_ml.reshape(GRID_W, rows_cm, d).transpose(1, 0, 2).reshape(t, d)

    dt_bias = ssm_dt_bias[0].reshape(1, 2 * ssm_heads)
    a_neg = -jnp.exp(ssm_a_log[0].astype(F32)).reshape(1, 2 * ssm_heads)
    d_skip = jnp.repeat(ssm_d[0], SSM_HEADDIM)[None, :]
    y_f = _ssd_call(xbc, dt, dt_bias, a_neg, d_skip, t, reverse=False)
    y_b = _ssd_call(xbc, dt, dt_bias, a_neg, d_skip, t, reverse=True)
    y_ssm = _ssmproj_call(y_f, y_b, z, ssm_norm_w[0][None, :], w_proj_ssm[0].astype(BF16), t)

    mix = _gate_call(h, w_gate[0].astype(BF16), b_gate[0][None, :], y_ml, y_ssm, t)
    x2 = _out_call(mix, w_out[0].astype(BF16), x1, ml[5][None, :], t)
    out = _ffn_call(x2, vec2, wg, wu, wd, 1, t, final=True)
    return out[None]
```

```python
import functools

import jax
import jax.numpy as jnp
from jax import lax
from jax.experimental import pallas as pl
from jax.experimental.pallas import tpu as pltpu

F32 = jnp.float32
BF16 = jnp.bfloat16

GRID_W = 64
N_MOD = 9
EPS = 1e-6
CONV_W = 5
CONV_PAD = CONV_W // 2
HALO = 16
CONV_UNIT = 256
ROW_BLK = 16
FFN_ROW_BLK = 64
ML_HEADS = 8
ML_L = 256
SSM_HEADDIM = 64
SSM_GROUPS = 8
SSM_DSTATE = 128
SSM_L = 128
LANES = 128
VMEM_LIMIT_BYTES = 56 * 1024 * 1024
NEG_INF = float("-inf")
LOG2E = 1.4426950408889634


def _cparams(sem):
    return pltpu.CompilerParams(dimension_semantics=sem, vmem_limit_bytes=VMEM_LIMIT_BYTES)


def _pick(n, cands):
    for c in cands:
        if n % c == 0:
            return c
    raise ValueError(f"no tile for {n} in {cands}")


def _silu(x):
    return x * (1.0 / (1.0 + jnp.exp(-x)))


def _sigmoid(x):
    return 1.0 / (1.0 + jnp.exp(-x))


def _log_sigmoid(x):
    return jnp.minimum(x, 0.0) - jnp.log1p(jnp.exp(-jnp.abs(x)))


def _softplus(x):
    return jnp.maximum(x, 0.0) + jnp.log1p(jnp.exp(-jnp.abs(x)))


def _rms(x):
    return x * lax.rsqrt(jnp.mean(jnp.square(x), axis=-1, keepdims=True) + EPS)


def _dot(a, b):
    return jnp.dot(a, b, preferred_element_type=F32)


def _dot_nt(a, b):
    return lax.dot_general(a, b, (((1,), (1,)), ((), ())), preferred_element_type=F32)


def _dot_exact(a, b):
    return jnp.dot(a, b, preferred_element_type=F32, precision=lax.Precision.HIGHEST)


def _row_loop(n_rows, blk, body):
    def step(r, carry):
        body(pl.ds(pl.multiple_of(r * blk, blk), blk))
        return carry
    lax.fori_loop(0, n_rows // blk, step, 0, unroll=2)


def _mod_kernel(c_ref, w_ref, b_ref, o_ref):
    a = _silu(c_ref[...]).astype(BF16)
    o_ref[...] = _dot(a, w_ref[...].astype(BF16)) + b_ref[...]


def _mod_call(cc, w_ada, b_ada):
    k, n = w_ada.shape
    tn = _pick(n, (1024, 512, 256, 128))
    return pl.pallas_call(
        _mod_kernel,
        grid=(n // tn,),
        in_specs=[pl.BlockSpec((8, k), lambda j: (0, 0)),
                  pl.BlockSpec((k, tn), lambda j: (0, j)),
                  pl.BlockSpec((1, tn), lambda j: (0, j))],
        out_specs=pl.BlockSpec((8, tn), lambda j: (0, j)),
        out_shape=jax.ShapeDtypeStruct((8, n), F32),
        compiler_params=_cparams(("arbitrary",)),
    )(cc, w_ada, b_ada.reshape(1, n))


def _ffn_kernel(x_ref, vec_ref, wg_ref, wu_ref, wd_ref, *rest, n_lat, final):
    if final:
        o_ref, h_s, acc_s = rest
    else:
        x1_ref, h2_ref, h_s, acc_s = rest
    i = pl.program_id(0)
    f = pl.program_id(1)
    tm = x_ref.shape[0]

    def vec(row, rows):
        off = jnp.where(i * tm + rows.start >= n_lat, 3, 0)
        return vec_ref[pl.ds(row + off, 1), :]

    @pl.when(f == 0)
    def _():
        def body(rows):
            h = _rms(x_ref[rows, :]) * (vec_ref[0:1, :] * (1.0 + vec(2, rows))) + vec(1, rows)
            h_s[rows, :] = h.astype(BF16)
            acc_s[rows, :] = jnp.zeros((FFN_ROW_BLK, acc_s.shape[1]), F32)
        _row_loop(tm, FFN_ROW_BLK, body)

    h = h_s[...]
    a = _silu(_dot(h, wg_ref[...])) * _dot(h, wu_ref[...])
    acc_s[...] += _dot(a.astype(BF16), wd_ref[...])

    @pl.when(f == pl.num_programs(1) - 1)
    def _():
        def body(rows):
            x1 = x_ref[rows, :] + (0.5 * vec(3, rows)) * acc_s[rows, :]
            if final:
                o_ref[rows, :] = _rms(x1) * vec_ref[7:8, :]
            else:
                x1_ref[rows, :] = x1
                h2 = _rms(x1) * (vec_ref[7:8, :] * (1.0 + vec(9, rows))) + vec(8, rows)
                h2_ref[rows, :] = h2.astype(BF16)
        _row_loop(tm, FFN_ROW_BLK, body)


def _ffn_call(x, vecs, wg, wu, wd, which, n_lat, final):
    m, d = x.shape
    dff = wg.shape[3]
    tm = _pick(m, (768, 512, 256))
    tf = _pick(dff, (512, 256, 128))
    row = pl.BlockSpec((tm, d), lambda i, f: (i, 0), pipeline_mode=pl.Buffered(1))
    x_row = pl.BlockSpec((tm, d), lambda i, f: (i, 0))
    if final:
        out_shape = jax.ShapeDtypeStruct((m, d), F32)
        out_specs = row
    else:
        out_shape = (jax.ShapeDtypeStruct((m, d), F32), jax.ShapeDtypeStruct((m, d), BF16))
        out_specs = (row, row)
    return pl.pallas_call(
        functools.partial(_ffn_kernel, n_lat=n_lat, final=final),
        grid=(m // tm, dff // tf),
        in_specs=[x_row,
                  pl.BlockSpec(vecs.shape, lambda i, f: (0, 0)),
                  pl.BlockSpec((None, None, d, tf), lambda i, f: (0, which, 0, f)),
                  pl.BlockSpec((None, None, d, tf), lambda i, f: (0, which, 0, f)),
                  pl.BlockSpec((None, None, tf, d), lambda i, f: (0, which, f, 0))],
        out_specs=out_specs,
        out_shape=out_shape,
        scratch_shapes=[pltpu.VMEM((tm, d), BF16), pltpu.VMEM((tm, d), F32)],
        compiler_params=_cparams(("arbitrary", "arbitrary")),
    )(x, vecs, wg, wu, wd)


def _mm_kernel(a_ref, w_ref, o_ref):
    o_ref[...] = _dot_nt(a_ref[...], w_ref[...].astype(BF16)).astype(o_ref.dtype)


def _mm_call(a, wt, row0, n, m, out_dtype):
    k = a.shape[1]
    tm = _pick(m, (1408, 1024, 768, 512, 256))
    tn = _pick(n, (1024, 512, 256, 128))
    assert row0 % 8 == 0
    return pl.pallas_call(
        _mm_kernel,
        grid=(m // tm, n // tn),
        in_specs=[pl.BlockSpec((tm, k), lambda i, j: (i, 0)),
                  pl.BlockSpec((pl.Element(tn), pl.Element(k)),
                               lambda i, j: (pl.multiple_of(row0 + j * tn, 8), 0))],
        out_specs=pl.BlockSpec((tm, tn), lambda i, j: (i, j)),
        out_shape=jax.ShapeDtypeStruct((m, n), out_dtype),
        compiler_params=_cparams(("arbitrary", "arbitrary")),
    )(a, wt)


def _mmconv_kernel(a_ref, ap_ref, an_ref, w_ref, cw_ref, cb_ref, *rest, n_lat, n_tot, scaled):
    if scaled:
        post_ref, o_ref, xs, ys, a_s = rest
    else:
        o_ref, xs, ys, a_s = rest
    i = pl.program_id(0)
    tm = a_ref.shape[0]
    tn = o_ref.shape[1]
    units = tm // CONV_UNIT
    ug = CONV_UNIT // 8
    span = CONV_UNIT + 4 * 8
    seq_units = n_lat // CONV_UNIT
    last_unit = n_tot // CONV_UNIT - 1

    @pl.when(pl.program_id(1) == 0)
    def _():
        a_s[0:HALO, :] = ap_ref[...]
        a_s[HALO:HALO + tm, :] = a_ref[...]
        a_s[HALO + tm:2 * HALO + tm, :] = an_ref[...]

    w = w_ref[...].astype(BF16)
    p_all = _dot_nt(a_s[...], w)
    p_prev = p_all[0:HALO]
    p = p_all[HALO:HALO + tm]
    p_next = p_all[HALO + tm:2 * HALO + tm]
    sub = lax.broadcasted_iota(jnp.int32, (8, 1), 0)

    for u in range(units):
        g = i * units + u
        keep_prev = jnp.where((g == 0) | (g == seq_units), 0.0, 1.0)
        keep_next = jnp.where((g == last_unit) | (g == seq_units - 1), 0.0, 1.0)
        r0 = u * CONV_UNIT
        base = u * span
        xs[base + 16:base + 16 + CONV_UNIT, :] = p[r0:r0 + CONV_UNIT]
        before = p[r0 - 16:r0] if u > 0 else p_prev
        after = p[r0 + CONV_UNIT:r0 + CONV_UNIT + 16] if u < units - 1 else p_next
        for d in range(2):
            own = pltpu.roll(p[r0 + (ug - 2 + d) * 8:r0 + (ug - 1 + d) * 8], 1, 0)
            other = pltpu.roll(before[d * 8:(d + 1) * 8], 1, 0) * keep_prev
            xs[base + d * 8:base + (d + 1) * 8, :] = jnp.where(sub == 0, other, own)
            own = pltpu.roll(p[r0 + d * 8:r0 + (d + 1) * 8], 7, 0)
            other = pltpu.roll(after[d * 8:(d + 1) * 8], 7, 0) * keep_next
            xs[base + 16 + CONV_UNIT + d * 8:base + 16 + CONV_UNIT + (d + 1) * 8, :] = (
                jnp.where(sub == 7, other, own))

        acc = cb_ref[...]
        for j in range(CONV_W):
            acc = acc + cw_ref[j:j + 1, :] * xs[base + 8 * j:base + 8 * j + CONV_UNIT, :]
        y = _silu(acc)
        if scaled:
            y = y * post_ref[...]
        for cblk in range(tn // LANES):
            lanes = slice(cblk * LANES, (cblk + 1) * LANES)
            ys[cblk, r0:r0 + CONV_UNIT, :] = y[:, lanes]
            for b8 in range(8):
                o_ref[r0 + b8 * ug:r0 + (b8 + 1) * ug, lanes] = (
                    ys[cblk, pl.ds(r0 + b8, ug, stride=8), :].astype(o_ref.dtype))


def _mmconv_call(a, w, col0, conv_w, conv_b, post, n_lat):
    m, k = a.shape
    n = conv_w.shape[1]
    tm = _pick(m, (768, 512, 256))
    tn = _pick(n, (512, 256, 128))
    assert col0 % 8 == 0 and tm % CONV_UNIT == 0 and n_lat % CONV_UNIT == 0 and m % CONV_UNIT == 0
    hb = tm // HALO
    last_halo = m // HALO - 1
    vec = pl.BlockSpec((1, tn), lambda i, j: (0, j))
    scaled = post is not None
    return pl.pallas_call(
        functools.partial(_mmconv_kernel, n_lat=n_lat, n_tot=m, scaled=scaled),
        grid=(m // tm, n // tn),
        in_specs=[pl.BlockSpec((tm, k), lambda i, j: (i, 0)),
                  pl.BlockSpec((HALO, k), lambda i, j: (jnp.maximum(i * hb - 1, 0), 0)),
                  pl.BlockSpec((HALO, k), lambda i, j: (jnp.minimum((i + 1) * hb, last_halo), 0)),
                  pl.BlockSpec((pl.Element(tn), pl.Element(k)),
                               lambda i, j: (pl.multiple_of(col0 + j * tn, 8), 0)),
                  pl.BlockSpec((8, tn), lambda i, j: (0, j)),
                  vec] + ([vec] if scaled else []),
        out_specs=pl.BlockSpec((tm, tn), lambda i, j: (i, j)),
        out_shape=jax.ShapeDtypeStruct((m, n), BF16),
        scratch_shapes=[pltpu.VMEM(((tm // CONV_UNIT) * (CONV_UNIT + 32), tn), F32),
                        pltpu.VMEM((tn // LANES, tm, LANES), F32),
                        pltpu.VMEM((tm + 2 * HALO, k), BF16)],
        compiler_params=_cparams(("arbitrary", "arbitrary")),
    )(a, a, a, w, conv_w, conv_b, *([post] if scaled else []))


def _unit_permute(x):
    rows, d = x.shape
    ug = CONV_UNIT // 8
    return x.reshape(rows // CONV_UNIT, 8, ug, d).transpose(0, 2, 1, 3).reshape(rows, d)


def _chunk_order(c, n_lat_chunks, n_ctx_chunks, reverse):
    if reverse:
        return n_lat_chunks + n_ctx_chunks - 1 - c
    return jnp.where(c < n_ctx_chunks, n_lat_chunks + c, c - n_ctx_chunks)


def _tri(chunk, reverse):
    r = lax.broadcasted_iota(jnp.int32, (chunk, chunk), 0)
    c = lax.broadcasted_iota(jnp.int32, (chunk, chunk), 1)
    return (c >= r) if reverse else (c <= r)


def _mlstm_kernel(q_ref, k_ref, v_ref, g_ref, gn_ref, gb_ref, o_ref, ct_s, n_s, m_s, bc_s, bct_s, gt_s, *, reverse):
    L = q_ref.shape[0]
    dk = q_ref.shape[1] // ML_HEADS
    dv = v_ref.shape[1] // ML_HEADS
    c = pl.program_id(0)

    mask = _tri(L, reverse)

    def prefix(g_blk):
        g = g_blk + gb_ref[...]
        bc = _dot_exact(mask.astype(F32), _log_sigmoid(g))
        bc_s[...] = bc
        bct_s[...] = bc.T
        gt_s[...] = g.T

    @pl.when(c == 0)
    def _():
        ct_s[...] = jnp.zeros_like(ct_s)
        n_s[...] = jnp.zeros_like(n_s)
        m_s[...] = jnp.zeros_like(m_s)
        prefix(g_ref[...])

    def lanes(x, n):
        return jnp.concatenate([x] * (n // LANES), axis=1)

    bc = bc_s[...]
    bc_t = bct_s[...]
    g_t = gt_s[...]
    end = 0 if reverse else L - 1
    base = 2 * ML_HEADS if reverse else 0
    ones = jnp.ones((L, LANES), BF16)

    for h in range(ML_HEADS):
        io = base + h
        fo = base + ML_HEADS + h
        bb = jnp.broadcast_to(bc[:, fo:fo + 1], (L, LANES))
        b_row = bc_t[fo:fo + 1, :]
        ig_row = g_t[io:io + 1, :]
        b_end = bc[end:end + 1, fo:fo + 1]
        m0 = m_s[h:h + 1, :]
        q = q_ref[:, h * dk:(h + 1) * dk]
        k = k_ref[:, h * dk:(h + 1) * dk]
        v = v_ref[:, h * dv:(h + 1) * dv]
        ct0 = ct_s[h]
        n0 = n_s[h]

        logw = jnp.where(mask, lanes(bb, L) + (ig_row - b_row), NEG_INF)
        m_inter = bb + m0
        m_intra = jnp.broadcast_to(jnp.max(logw, axis=1, keepdims=True), (L, LANES))
        m_out = jnp.maximum(m_inter, m_intra)
        p = (_dot_nt(q, k) * jnp.exp(logw - lanes(m_out, L))).astype(BF16)
        s_inter = jnp.exp(m_inter - m_out)
        num = _dot(p, v) + lanes(s_inter, dv) * _dot(q, ct0.astype(BF16))
        den = _dot(p, ones) + s_inter * _dot(q, n0.astype(BF16))
        inv = 1.0 / jnp.maximum(jnp.abs(den), jnp.exp(-m_out))
        o_ref[:, h * dv:(h + 1) * dv] = (num * lanes(inv, dv)).astype(o_ref.dtype)

        a_row = b_end + ig_row - b_row
        m_loc = jnp.max(a_row, axis=1, keepdims=True)
        kw_t = (k.T.astype(F32) * jnp.exp(a_row - m_loc)).astype(BF16)
        m_new = jnp.maximum(b_end + m0, m_loc)
        s_old = jnp.exp(b_end + m0 - m_new)
        s_new = jnp.exp(m_loc - m_new)
        ct_s[h] = lanes(s_old, dv) * ct0 + lanes(s_new, dv) * _dot(kw_t, v)
        n_s[h] = s_old * n0 + s_new * _dot(kw_t, ones)
        m_s[h:h + 1, :] = m_new

    prefix(gn_ref[...])


def _mlstm_call(qk, vo, gates, gate_b, n_lat, reverse):
    rows = qk.shape[0]
    L = ML_L
    nqk = qk.shape[1] // 2
    nv = vo.shape[1] // 2
    dk = nqk // ML_HEADS
    dv = nv // ML_HEADS
    nlc = n_lat // L
    ncc = (rows - n_lat) // L
    order = functools.partial(_chunk_order, n_lat_chunks=nlc, n_ctx_chunks=ncc, reverse=reverse)
    return pl.pallas_call(
        functools.partial(_mlstm_kernel, reverse=reverse),
        grid=(nlc + ncc,),
        in_specs=[pl.BlockSpec((L, nqk), lambda c: (order(c), 0)),
                  pl.BlockSpec((L, nqk), lambda c: (order(c), 1)),
                  pl.BlockSpec((L, nv), lambda c: (order(c), 0)),
                  pl.BlockSpec((L, LANES), lambda c: (order(c), 0)),
                  pl.BlockSpec((L, LANES), lambda c: (order(jnp.minimum(c + 1, nlc + ncc - 1)), 0)),
                  pl.BlockSpec((1, LANES), lambda c: (0, 0))],
        out_specs=pl.BlockSpec((L, nv), lambda c: (order(c), 0)),
        out_shape=jax.ShapeDtypeStruct((rows, nv), BF16),
        scratch_shapes=[pltpu.VMEM((ML_HEADS, dk, dv), F32),
                        pltpu.VMEM((ML_HEADS, dk, LANES), F32),
                        pltpu.VMEM((ML_HEADS, LANES), F32),
                        pltpu.VMEM((L, LANES), F32), pltpu.VMEM((LANES, L), F32), pltpu.VMEM((LANES, L), F32)],
        compiler_params=_cparams(("arbitrary",)),
    )(qk, qk, vo, gates, gates, gate_b)


def _ssd_kernel(x_ref, bm_ref, cm_ref, dt_ref, dtn_ref, dtb_ref, a_ref, dsk_ref, y_ref,
                st_s, bc_s, bct_s, dtt_s, *, reverse):
    L = x_ref.shape[0]
    nx = x_ref.shape[1]
    c = pl.program_id(0)
    mask = _tri(L, reverse)

    def prefix(dt_blk):
        dtv = _softplus(dt_blk + dtb_ref[...])
        bc = _dot_exact(mask.astype(F32), dtv * a_ref[...]) * LOG2E
        bc_s[...] = bc
        bct_s[...] = bc.T
        dtt_s[...] = jnp.log2(dtv).T

    @pl.when(c == 0)
    def _():
        st_s[...] = jnp.zeros_like(st_s)
        prefix(dt_ref[...])

    bc = bc_s[...]
    bc_t = bct_s[...]
    ldt_t = dtt_s[...]
    end = 0 if reverse else L - 1
    nheads = nx // SSM_HEADDIM
    hpg = nheads // SSM_GROUPS
    base = nheads if reverse else 0
    pw = 2 * SSM_HEADDIM
    lo = lax.broadcasted_iota(jnp.int32, (1, pw), 1) < SSM_HEADDIM
    lo_b = jnp.where(lo, 1.0, 0.0).astype(BF16)
    hi_b = jnp.where(lo, 0.0, 1.0).astype(BF16)

    for g in range(SSM_GROUPS):
        bm = bm_ref[:, g * SSM_DSTATE:(g + 1) * SSM_DSTATE]
        cm = cm_ref[:, g * SSM_DSTATE:(g + 1) * SSM_DSTATE]
        cb = _dot_nt(cm, bm)
        bm_t = bm.astype(F32).T
        for pr in range(hpg // 2):
            col = (g * hpg + 2 * pr) * SSM_HEADDIM
            xp = x_ref[:, col:col + pw]
            ms, ebs, bts, decs = [], [], [], []
            for e in range(2):
                co = base + g * hpg + 2 * pr + e
                bb = jnp.broadcast_to(bc[:, co:co + 1], (L, L))
                r_row = bc_t[co:co + 1, :] - ldt_t[co:co + 1, :]
                b_end = bc[end:end + 1, co:co + 1]
                ms.append((cb * jnp.exp2(jnp.where(mask, bb - r_row, NEG_INF))).astype(BF16))
                ebs.append(jnp.exp2(bb[:, 0:pw]))
                bts.append((bm_t * jnp.exp2(b_end - r_row)).astype(BF16))
                decs.append(jnp.exp2(b_end))
            x_lo = xp * lo_b
            x_hi = xp * hi_b
            s0 = st_s[:, col:col + pw]
            y = _dot(ms[0], x_lo) + _dot(ms[1], x_hi)
            y = y + jnp.where(lo, ebs[0], ebs[1]) * _dot(cm, s0.astype(BF16))
            if not reverse:
                y = y + dsk_ref[:, col:col + pw] * xp.astype(F32)
            y_ref[:, col:col + pw] = y.astype(y_ref.dtype)
            st_s[:, col:col + pw] = (s0 * jnp.where(lo, decs[0], decs[1])
                                     + _dot(bts[0], x_lo) + _dot(bts[1], x_hi))

    prefix(dtn_ref[...])


def _ssd_call(xbc, dt, dt_bias, a_neg, d_skip, n_lat, reverse):
    rows = xbc.shape[0]
    L = SSM_L
    assert L >= 2 * SSM_HEADDIM
    nb = SSM_GROUPS * SSM_DSTATE
    nx = xbc.shape[1] - 2 * nb
    nlc = n_lat // L
    ncc = (rows - n_lat) // L
    order = functools.partial(_chunk_order, n_lat_chunks=nlc, n_ctx_chunks=ncc, reverse=reverse)
    xblk = nx // nb
    nc = nlc + ncc
    return pl.pallas_call(
        functools.partial(_ssd_kernel, reverse=reverse),
        grid=(nc,),
        in_specs=[pl.BlockSpec((L, nx), lambda c: (order(c), 0)),
                  pl.BlockSpec((L, nb), lambda c: (order(c), xblk)),
                  pl.BlockSpec((L, nb), lambda c: (order(c), xblk + 1)),
                  pl.BlockSpec((L, LANES), lambda c: (order(c), 0)),
                  pl.BlockSpec((L, LANES), lambda c: (order(jnp.minimum(c + 1, nc - 1)), 0)),
                  pl.BlockSpec((1, LANES), lambda c: (0, 0)),
                  pl.BlockSpec((1, LANES), lambda c: (0, 0)),
                  pl.BlockSpec((1, nx), lambda c: (0, 0))],
        out_specs=pl.BlockSpec((L, nx), lambda c: (order(c), 0)),
        out_shape=jax.ShapeDtypeStruct((rows, nx), BF16),
        scratch_shapes=[pltpu.VMEM((SSM_DSTATE, nx), F32),
                        pltpu.VMEM((L, LANES), F32), pltpu.VMEM((LANES, L), F32), pltpu.VMEM((LANES, L), F32)],
        compiler_params=_cparams(("arbitrary",)),
    )(xbc, xbc, xbc, dt, dt, dt_bias, a_neg, d_skip)


def _mlproj_kernel(hf_ref, hb_ref, o_ref, nw_ref, w_ref, y_ref, a_s):
    @pl.when(pl.program_id(1) == 0)
    def _():
        dv = a_s.shape[1] // ML_HEADS

        def body(rows):
            for h in range(ML_HEADS):
                sl = slice(h * dv, (h + 1) * dv)
                hh = _rms(hf_ref[rows, sl].astype(F32) + hb_ref[rows, sl].astype(F32)) * nw_ref[:, sl]
                a_s[rows, sl] = (_sigmoid(o_ref[rows, sl].astype(F32)) * hh).astype(BF16)
        _row_loop(a_s.shape[0], ROW_BLK, body)
    y_ref[...] = _dot(a_s[...], w_ref[...]).astype(y_ref.dtype)


def _mlproj_call(hf, hb, vo, norm_w, w, n_lat):
    nv = hf.shape[1]
    n = w.shape[1]
    tm = _pick(n_lat, (1024, 512, 256))
    tn = _pick(n, (1024, 512, 256, 128))
    row = pl.BlockSpec((tm, nv), lambda i, j: (i, 0))
    return pl.pallas_call(
        _mlproj_kernel,
        grid=(n_lat // tm, n // tn),
        in_specs=[row, row,
                  pl.BlockSpec((tm, nv), lambda i, j: (i, 1)),
                  pl.BlockSpec((1, nv), lambda i, j: (0, 0)),
                  pl.BlockSpec((nv, tn), lambda i, j: (0, j))],
        out_specs=pl.BlockSpec((tm, tn), lambda i, j: (i, j)),
        out_shape=jax.ShapeDtypeStruct((n_lat, n), BF16),
        scratch_shapes=[pltpu.VMEM((tm, nv), BF16)],
        compiler_params=_cparams(("arbitrary", "arbitrary")),
    )(hf, hb, vo, norm_w, w)


def _ssmproj_kernel(yf_ref, yb_ref, z_ref, nw_ref, w_ref, y_ref, a_s):
    @pl.when(pl.program_id(1) == 0)
    def _():
        gw = a_s.shape[1] // SSM_GROUPS

        def body(rows):
            for g in range(SSM_GROUPS):
                sl = slice(g * gw, (g + 1) * gw)
                y = (yf_ref[rows, sl].astype(F32) + yb_ref[rows, sl].astype(F32)) * _silu(z_ref[rows, sl].astype(F32))
                a_s[rows, sl] = (_rms(y) * nw_ref[:, sl]).astype(BF16)
        _row_loop(a_s.shape[0], ROW_BLK, body)
    y_ref[...] = _dot(a_s[...], w_ref[...]).astype(y_ref.dtype)


def _ssmproj_call(yf, yb, z, norm_w, w, n_lat):
    nx = yf.shape[1]
    n = w.shape[1]
    tm = _pick(n_lat, (512, 256))
    tn = _pick(n, (1024, 512, 256, 128))
    row = pl.BlockSpec((tm, nx), lambda i, j: (i, 0))
    return pl.pallas_call(
        _ssmproj_kernel,
        grid=(n_lat // tm, n // tn),
        in_specs=[row, row, row,
                  pl.BlockSpec((1, nx), lambda i, j: (0, 0)),
                  pl.BlockSpec((nx, tn), lambda i, j: (0, j))],
        out_specs=pl.BlockSpec((tm, tn), lambda i, j: (i, j)),
        out_shape=jax.ShapeDtypeStruct((n_lat, n), BF16),
        scratch_shapes=[pltpu.VMEM((tm, nx), BF16)],
        compiler_params=_cparams(("arbitrary", "arbitrary")),
    )(yf, yb, z, norm_w, w)


def _gate_kernel(h_ref, w1_ref, w2_ref, b1_ref, b2_ref, yml_ref, yssm_ref, o_ref):
    h = h_ref[...]
    g1 = _sigmoid(_dot(h, w1_ref[...]) + b1_ref[...])
    g2 = _sigmoid(_dot(h, w2_ref[...]) + b2_ref[...])
    o_ref[...] = (g1 * yml_ref[...].astype(F32) + g2 * yssm_ref[...].astype(F32)).astype(o_ref.dtype)


def _gate_call(h, w_gate, b_gate, y_ml, y_ssm, n_lat):
    d = h.shape[1]
    tm = _pick(n_lat, (1024, 512, 256))
    tn = _pick(d, (1024, 512, 256, 128))
    nj = d // tn
    tile = pl.BlockSpec((tm, tn), lambda i, j: (i, j))
    return pl.pallas_call(
        _gate_kernel,
        grid=(n_lat // tm, nj),
        in_specs=[pl.BlockSpec((tm, d), lambda i, j: (i, 0)),
                  pl.BlockSpec((d, tn), lambda i, j: (0, j)),
                  pl.BlockSpec((d, tn), lambda i, j: (0, nj + j)),
                  pl.BlockSpec((1, tn), lambda i, j: (0, j)),
                  pl.BlockSpec((1, tn), lambda i, j: (0, nj + j)),
                  tile, tile],
        out_specs=tile,
        out_shape=jax.ShapeDtypeStruct((n_lat, d), BF16),
        compiler_params=_cparams(("arbitrary", "arbitrary")),
    )(h, w_gate, w_gate, b_gate, b_gate, y_ml, y_ssm)


def _out_kernel(a_ref, w_ref, x_ref, g_ref, o_ref):
    o_ref[...] = x_ref[...] + g_ref[...] * _dot(a_ref[...], w_ref[...])


def _out_call(mix, w_out, x1, gate, n_lat):
    d = mix.shape[1]
    tm = _pick(n_lat, (1024, 512, 256))
    tn = _pick(d, (1024, 512, 256, 128))
    tile = pl.BlockSpec((tm, tn), lambda i, j: (i, j))
    return pl.pallas_call(
        _out_kernel,
        grid=(n_lat // tm, d // tn),
        in_specs=[pl.BlockSpec((tm, d), lambda i, j: (i, 0)),
                  pl.BlockSpec((d, tn), lambda i, j: (0, j)),
                  tile,
                  pl.BlockSpec((1, tn), lambda i, j: (0, j))],
        out_specs=tile,
        out_shape=jax.ShapeDtypeStruct((n_lat, d), F32),
        compiler_params=_cparams(("arbitrary", "arbitrary")),
    )(mix, w_out, x1, gate)


def _pad_rows(a, rows):
    return jnp.concatenate([a, jnp.zeros((rows - a.shape[0],) + a.shape[1:], a.dtype)], axis=0)


def _pad_cols(a, cols):
    return jnp.concatenate([a, jnp.zeros(a.shape[:-1] + (cols - a.shape[-1],), a.dtype)], axis=-1)


def kernel(x, c, ctx, c_ctx, w_ada, b_ada, norm_w, ffn_w_gate, ffn_w_up, ffn_w_down, w_in, ml_conv_w, ml_conv_b, ml_gate_b, ml_norm_w, w_proj_ml, ssm_conv_w, ssm_conv_b, ssm_dt_bias, ssm_a_log, ssm_d, ssm_norm_w, w_proj_ssm, w_gate, b_gate, w_out, final_norm_w):
    depth = w_ada.shape[0]
    assert depth == 1 and x.shape[0] == 1, "single layer, single batch element"
    t, d = x.shape[1], x.shape[2]
    n_tot = t + ctx.shape[1]
    rows_cm = t // GRID_W
    ml_qk = ml_conv_w.shape[2] // 2
    ml_v = ml_norm_w.shape[1]
    n_ml = 2 * ml_qk + 2 * ml_v
    ml_cols = n_ml + 4 * ML_HEADS
    ssm_inner = ssm_norm_w.shape[1]
    ssm_heads = ssm_d.shape[1]
    ssm_xbc = ssm_conv_w.shape[2]

    cc = _pad_rows(jnp.concatenate([c, c_ctx[None, :]], axis=0), 8)
    mod = _mod_call(cc, w_ada[0], b_ada[0])
    ml, mc = mod[0].reshape(N_MOD, d), mod[1].reshape(N_MOD, d)
    nw = norm_w[0]
    zero = jnp.zeros_like(nw[0])
    vec1 = _pad_rows(jnp.stack([nw[0], ml[0], ml[1], ml[2], mc[0], mc[1], mc[2],
                                nw[1], ml[3], ml[4], zero, mc[3], mc[4]]), 16)
    vec2 = _pad_rows(jnp.stack([nw[2], ml[6], ml[7], ml[8], zero, zero, zero, final_norm_w]), 16)
    wg, wu, wd = ffn_w_gate.astype(BF16), ffn_w_up.astype(BF16), ffn_w_down.astype(BF16)

    xall = jnp.concatenate([x[0], ctx[0]], axis=0)
    x1, h = _ffn_call(xall, vec1, wg, wu, wd, 0, t, final=False)

    h_cm = jnp.concatenate(
        [h[:t].reshape(rows_cm, GRID_W, d).transpose(1, 0, 2).reshape(t, d), h[t:]], axis=0)
    wt = w_in[0].T
    n_ssm = ssm_inner + ssm_xbc
    ml_cw = _pad_rows(ml_conv_w[0], 8)
    ml_post = jnp.concatenate([jnp.full((1, ml_qk), (ml_qk // ML_HEADS) ** -0.5, F32),
                               jnp.ones((1, ml_qk), F32)], axis=1)
    qk = _mmconv_call(_unit_permute(h_cm), wt, 0, ml_cw, ml_conv_b[0][None, :], ml_post, t)
    vo = _mm_call(h_cm, wt, 2 * ml_qk, 2 * ml_v, n_tot, BF16)
    gates = _mm_call(h_cm, wt, n_ml, LANES, n_tot, F32)
    z = _mm_call(h, wt, ml_cols, ssm_inner, t, BF16)
    xbc = _mmconv_call(_unit_permute(h), wt, ml_cols + ssm_inner, _pad_rows(ssm_conv_w[0], 8),
                       ssm_conv_b[0][None, :], None, t)
    dt = _mm_call(h, wt, ml_cols + n_ssm, 2 * ssm_heads, n_tot, F32)

    gate_b = _pad_cols(ml_gate_b[0].reshape(1, 4 * ML_HEADS), LANES)
    h_f = _mlstm_call(qk, vo, gates, gate_b, t, reverse=False)
    h_b = _mlstm_call(qk, vo, gates, gate_b, t, reverse=True)
    y_ml = _mlproj_call(h_f, h_b, vo, ml_norm_w[0][None, :], w_proj_ml[0].astype(BF16), t)
    y_ml = y_ml.reshape(GRID_W, rows_cm, d).transpose(1, 0, 2).reshape(t, d)

    dt_bias = ssm_dt_bias[0].reshape(1, 2 * ssm_heads)
    a_neg = -jnp.exp(ssm_a_log[0].astype(F32)).reshape(1, 2 * ssm_heads)
    d_skip = jnp.repeat(ssm_d[0], SSM_HEADDIM)[None, :]
    y_f = _ssd_call(xbc, dt, dt_bias, a_neg, d_skip, t, reverse=False)
    y_b = _ssd_call(xbc, dt, dt_bias, a_neg, d_skip, t, reverse=True)
    y_ssm = _ssmproj_call(y_f, y_b, z, ssm_norm_w[0][None, :], w_proj_ssm[0].astype(BF16), t)

    mix = _gate_call(h, w_gate[0].astype(BF16), b_gate[0][None, :], y_ml, y_ssm, t)
    x2 = _out_call(mix, w_out[0].astype(BF16), x1, ml[5][None, :], t)
    out = _ffn_call(x2, vec2, wg, wu, wd, 1, t, final=True)
    return out[None]
```

```python
import functools

import jax
import jax.numpy as jnp
from jax import lax
from jax.experimental import pallas as pl
from jax.experimental.pallas import tpu as pltpu

F32 = jnp.float32
BF16 = jnp.bfloat16

GRID_W = 64
N_MOD = 9
EPS = 1e-6
CONV_W = 5
CONV_PAD = CONV_W // 2
HALO = 16
CONV_UNIT = 256
ROW_BLK = 16
FFN_ROW_BLK = 32
FFN_PREFETCH_MAX_ROWS = 768
ML_HEADS = 8
ML_L = 256
SSM_HEADDIM = 64
SSM_GROUPS = 8
SSM_DSTATE = 128
SSM_L = 128
LANES = 128
VMEM_LIMIT_BYTES = 56 * 1024 * 1024
NEG_INF = float("-inf")
LOG2E = 1.4426950408889634


def _cparams(sem):
    return pltpu.CompilerParams(dimension_semantics=sem, vmem_limit_bytes=VMEM_LIMIT_BYTES)


def _pick(n, cands):
    for c in cands:
        if n % c == 0:
            return c
    raise ValueError(f"no tile for {n} in {cands}")


def _silu(x):
    return x * (1.0 / (1.0 + jnp.exp(-x)))


def _sigmoid(x):
    return 1.0 / (1.0 + jnp.exp(-x))


def _log_sigmoid(x):
    return jnp.minimum(x, 0.0) - jnp.log1p(jnp.exp(-jnp.abs(x)))


def _softplus(x):
    return jnp.maximum(x, 0.0) + jnp.log1p(jnp.exp(-jnp.abs(x)))


def _rms(x):
    return x * lax.rsqrt(jnp.mean(jnp.square(x), axis=-1, keepdims=True) + EPS)


def _dot(a, b):
    return jnp.dot(a, b, preferred_element_type=F32)


def _dot_nt(a, b):
    return lax.dot_general(a, b, (((1,), (1,)), ((), ())), preferred_element_type=F32)


def _dot_exact(a, b):
    return jnp.dot(a, b, preferred_element_type=F32, precision=lax.Precision.HIGHEST)


def _row_loop(n_rows, blk, body):
    def step(r, carry):
        body(pl.ds(pl.multiple_of(r * blk, blk), blk))
        return carry
    lax.fori_loop(0, n_rows // blk, step, 0, unroll=2)


def _mod_kernel(c_ref, w_ref, b_ref, o_ref):
    a = _silu(c_ref[...]).astype(BF16)
    o_ref[...] = _dot(a, w_ref[...].astype(BF16)) + b_ref[...]


def _mod_call(cc, w_ada, b_ada):
    k, n = w_ada.shape
    tn = _pick(n, (1024, 512, 256, 128))
    return pl.pallas_call(
        _mod_kernel,
        grid=(n // tn,),
        in_specs=[pl.BlockSpec((8, k), lambda j: (0, 0)),
                  pl.BlockSpec((k, tn), lambda j: (0, j)),
                  pl.BlockSpec((1, tn), lambda j: (0, j))],
        out_specs=pl.BlockSpec((8, tn), lambda j: (0, j)),
        out_shape=jax.ShapeDtypeStruct((8, n), F32),
        compiler_params=_cparams(("arbitrary",)),
    )(cc, w_ada, b_ada.reshape(1, n))


def _ffn_kernel(x_ref, vec_ref, wg_ref, wu_ref, wd_ref, *rest, n_lat, final):
    if final:
        o_ref, h_s, acc_s = rest
    else:
        x1_ref, h2_ref, h_s, acc_s = rest
    i = pl.program_id(0)
    f = pl.program_id(1)
    tm = x_ref.shape[0]

    def vec(row, rows):
        off = jnp.where(i * tm + rows.start >= n_lat, 3, 0)
        return vec_ref[pl.ds(row + off, 1), :]

    @pl.when(f == 0)
    def _():
        def body(rows):
            h = _rms(x_ref[rows, :]) * (vec_ref[0:1, :] * (1.0 + vec(2, rows))) + vec(1, rows)
            h_s[rows, :] = h.astype(BF16)
            acc_s[rows, :] = jnp.zeros((FFN_ROW_BLK, acc_s.shape[1]), F32)
        _row_loop(tm, FFN_ROW_BLK, body)

    h = h_s[...]
    a = _silu(_dot(h, wg_ref[...].astype(BF16))) * _dot(h, wu_ref[...].astype(BF16))
    acc_s[...] += _dot(a.astype(BF16), wd_ref[...])

    @pl.when(f == pl.num_programs(1) - 1)
    def _():
        def body(rows):
            x1 = x_ref[rows, :] + (0.5 * vec(3, rows)) * acc_s[rows, :]
            if final:
                o_ref[rows, :] = _rms(x1) * vec_ref[7:8, :]
            else:
                x1_ref[rows, :] = x1
                h2 = _rms(x1) * (vec_ref[7:8, :] * (1.0 + vec(9, rows))) + vec(8, rows)
                h2_ref[rows, :] = h2.astype(BF16)
        _row_loop(tm, FFN_ROW_BLK, body)


def _ffn_call(x, vecs, wg, wu, wd, which, n_lat, final):
    m, d = x.shape
    dff = wg.shape[3]
    tm = _pick(m, (1024, 768, 512, 256))
    tf = _pick(dff, (512, 256, 128))
    row = pl.BlockSpec((tm, d), lambda i, f: (i, 0), pipeline_mode=pl.Buffered(1))
    x_row = pl.BlockSpec((tm, d), lambda i, f: (i, 0)) if tm <= FFN_PREFETCH_MAX_ROWS else row
    if final:
        out_shape = jax.ShapeDtypeStruct((m, d), F32)
        out_specs = row
    else:
        out_shape = (jax.ShapeDtypeStruct((m, d), F32), jax.ShapeDtypeStruct((m, d), BF16))
        out_specs = (row, row)
    return pl.pallas_call(
        functools.partial(_ffn_kernel, n_lat=n_lat, final=final),
        grid=(m // tm, dff // tf),
        in_specs=[x_row,
                  pl.BlockSpec(vecs.shape, lambda i, f: (0, 0)),
                  pl.BlockSpec((None, None, d, tf), lambda i, f: (0, which, 0, f)),
                  pl.BlockSpec((None, None, d, tf), lambda i, f: (0, which, 0, f)),
                  pl.BlockSpec((None, None, tf, d), lambda i, f: (0, which, f, 0))],
        out_specs=out_specs,
        out_shape=out_shape,
        scratch_shapes=[pltpu.VMEM((tm, d), BF16), pltpu.VMEM((tm, d), F32)],
        compiler_params=_cparams(("arbitrary", "arbitrary")),
    )(x, vecs, wg, wu, wd)


def _mm_kernel(a_ref, w_ref, o_ref):
    o_ref[...] = _dot_nt(a_ref[...], w_ref[...].astype(BF16)).astype(o_ref.dtype)


def _mm_call(a, wt, row0, n, m, out_dtype):
    k = a.shape[1]
    tm = _pick(m, (1408, 1024, 768, 512, 256))
    tn = _pick(n, (1024, 512, 256, 128))
    assert row0 % 8 == 0
    return pl.pallas_call(
        _mm_kernel,
        grid=(m // tm, n // tn),
        in_specs=[pl.BlockSpec((tm, k), lambda i, j: (i, 0)),
                  pl.BlockSpec((pl.Element(tn), pl.Element(k)),
                               lambda i, j: (pl.multiple_of(row0 + j * tn, 8), 0))],
        out_specs=pl.BlockSpec((tm, tn), lambda i, j: (i, j)),
        out_shape=jax.ShapeDtypeStruct((m, n), out_dtype),
        compiler_params=_cparams(("arbitrary", "arbitrary")),
    )(a, wt)


def _mmconv_kernel(a_ref, ap_ref, an_ref, w_ref, cw_ref, cb_ref, *rest, n_lat, n_tot, scaled):
    if scaled:
        post_ref, o_ref, xs, ys, a_s = rest
    else:
        o_ref, xs, ys, a_s = rest
    i = pl.program_id(0)
    tm = a_ref.shape[0]
    tn = o_ref.shape[1]
    units = tm // CONV_UNIT
    ug = CONV_UNIT // 8
    span = CONV_UNIT + 4 * 8
    seq_units = n_lat // CONV_UNIT
    last_unit = n_tot // CONV_UNIT - 1

    @pl.when(pl.program_id(1) == 0)
    def _():
        a_s[0:HALO, :] = ap_ref[...]
        a_s[HALO:HALO + tm, :] = a_ref[...]
        a_s[HALO + tm:2 * HALO + tm, :] = an_ref[...]

    w = w_ref[...].astype(BF16)
    p_all = _dot_nt(a_s[...], w)
    p_prev = p_all[0:HALO]
    p = p_all[HALO:HALO + tm]
    p_next = p_all[HALO + tm:2 * HALO + tm]
    sub = lax.broadcasted_iota(jnp.int32, (8, 1), 0)

    for u in range(units):
        g = i * units + u
        keep_prev = jnp.where((g == 0) | (g == seq_units), 0.0, 1.0)
        keep_next = jnp.where((g == last_unit) | (g == seq_units - 1), 0.0, 1.0)
        r0 = u * CONV_UNIT
        base = u * span
        xs[base + 16:base + 16 + CONV_UNIT, :] = p[r0:r0 + CONV_UNIT]
        before = p[r0 - 16:r0] if u > 0 else p_prev
        after = p[r0 + CONV_UNIT:r0 + CONV_UNIT + 16] if u < units - 1 else p_next
        for d in range(2):
            own = pltpu.roll(p[r0 + (ug - 2 + d) * 8:r0 + (ug - 1 + d) * 8], 1, 0)
            other = pltpu.roll(before[d * 8:(d + 1) * 8], 1, 0) * keep_prev
            xs[base + d * 8:base + (d + 1) * 8, :] = jnp.where(sub == 0, other, own)
            own = pltpu.roll(p[r0 + d * 8:r0 + (d + 1) * 8], 7, 0)
            other = pltpu.roll(after[d * 8:(d + 1) * 8], 7, 0) * keep_next
            xs[base + 16 + CONV_UNIT + d * 8:base + 16 + CONV_UNIT + (d + 1) * 8, :] = (
                jnp.where(sub == 7, other, own))

        acc = cb_ref[...]
        for j in range(CONV_W):
            acc = acc + cw_ref[j:j + 1, :] * xs[base + 8 * j:base + 8 * j + CONV_UNIT, :]
        y = _silu(acc)
        if scaled:
            y = y * post_ref[...]
        for cblk in range(tn // LANES):
            lanes = slice(cblk * LANES, (cblk + 1) * LANES)
            ys[cblk, r0:r0 + CONV_UNIT, :] = y[:, lanes]
            for b8 in range(8):
                o_ref[r0 + b8 * ug:r0 + (b8 + 1) * ug, lanes] = (
                    ys[cblk, pl.ds(r0 + b8, ug, stride=8), :].astype(o_ref.dtype))


def _mmconv_call(a, w, col0, conv_w, conv_b, post, n_lat):
    m, k = a.shape
    n = conv_w.shape[1]
    tm = _pick(m, (768, 512, 256))
    tn = _pick(n, (512, 256, 128))
    assert col0 % 8 == 0 and tm % CONV_UNIT == 0 and n_lat % CONV_UNIT == 0 and m % CONV_UNIT == 0
    hb = tm // HALO
    last_halo = m // HALO - 1
    vec = pl.BlockSpec((1, tn), lambda i, j: (0, j))
    scaled = post is not None
    return pl.pallas_call(
        functools.partial(_mmconv_kernel, n_lat=n_lat, n_tot=m, scaled=scaled),
        grid=(m // tm, n // tn),
        in_specs=[pl.BlockSpec((tm, k), lambda i, j: (i, 0)),
                  pl.BlockSpec((HALO, k), lambda i, j: (jnp.maximum(i * hb - 1, 0), 0)),
                  pl.BlockSpec((HALO, k), lambda i, j: (jnp.minimum((i + 1) * hb, last_halo), 0)),
                  pl.BlockSpec((pl.Element(tn), pl.Element(k)),
                               lambda i, j: (pl.multiple_of(col0 + j * tn, 8), 0)),
                  pl.BlockSpec((8, tn), lambda i, j: (0, j)),
                  vec] + ([vec] if scaled else []),
        out_specs=pl.BlockSpec((tm, tn), lambda i, j: (i, j)),
        out_shape=jax.ShapeDtypeStruct((m, n), BF16),
        scratch_shapes=[pltpu.VMEM(((tm // CONV_UNIT) * (CONV_UNIT + 32), tn), F32),
                        pltpu.VMEM((tn // LANES, tm, LANES), F32),
                        pltpu.VMEM((tm + 2 * HALO, k), BF16)],
        compiler_params=_cparams(("arbitrary", "arbitrary")),
    )(a, a, a, w, conv_w, conv_b, *([post] if scaled else []))


def _unit_permute(x):
    rows, d = x.shape
    ug = CONV_UNIT // 8
    return x.reshape(rows // CONV_UNIT, 8, ug, d).transpose(0, 2, 1, 3).reshape(rows, d)


def _chunk_order(c, n_lat_chunks, n_ctx_chunks, reverse):
    if reverse:
        return n_lat_chunks + n_ctx_chunks - 1 - c
    return jnp.where(c < n_ctx_chunks, n_lat_chunks + c, c - n_ctx_chunks)


def _tri(chunk, reverse):
    r = lax.broadcasted_iota(jnp.int32, (chunk, chunk), 0)
    c = lax.broadcasted_iota(jnp.int32, (chunk, chunk), 1)
    return (c >= r) if reverse else (c <= r)


def _mlstm_kernel(q_ref, k_ref, v_ref, g_ref, gn_ref, gb_ref, o_ref, ct_s, n_s, m_s, bc_s, bct_s, gt_s,
                  bcn_s, bctn_s, gtn_s, *, reverse):
    L = q_ref.shape[0]
    dk = q_ref.shape[1] // ML_HEADS
    dv = v_ref.shape[1] // ML_HEADS
    c = pl.program_id(0)

    mask = _tri(L, reverse)

    def prefix(g_blk, bc_d, bct_d, gt_d):
        g = g_blk + gb_ref[...]
        bc = _dot_exact(mask.astype(F32), _log_sigmoid(g))
        bc_d[...] = bc
        bct_d[...] = bc.T
        gt_d[...] = g.T

    @pl.when(c == 0)
    def _():
        ct_s[...] = jnp.zeros_like(ct_s)
        n_s[...] = jnp.zeros_like(n_s)
        m_s[...] = jnp.zeros_like(m_s)
        prefix(g_ref[...], bc_s, bct_s, gt_s)

    prefix(gn_ref[...], bcn_s, bctn_s, gtn_s)

    def lanes(x, n):
        return jnp.concatenate([x] * (n // LANES), axis=1)

    bc = bc_s[...]
    bc_t = bct_s[...]
    g_t = gt_s[...]
    end = 0 if reverse else L - 1
    base = 2 * ML_HEADS if reverse else 0
    ones = jnp.ones((L, LANES), BF16)

    for h in range(ML_HEADS):
        io = base + h
        fo = base + ML_HEADS + h
        bb = jnp.broadcast_to(bc[:, fo:fo + 1], (L, LANES))
        b_row = bc_t[fo:fo + 1, :]
        ig_row = g_t[io:io + 1, :]
        b_end = bc[end:end + 1, fo:fo + 1]
        m0 = m_s[h:h + 1, :]
        q = q_ref[:, h * dk:(h + 1) * dk]
        k = k_ref[:, h * dk:(h + 1) * dk]
        v = v_ref[:, h * dv:(h + 1) * dv]
        ct0 = ct_s[h]
        n0 = n_s[h]

        logw = jnp.where(mask, lanes(bb, L) + (ig_row - b_row), NEG_INF)
        m_inter = bb + m0
        m_intra = jnp.broadcast_to(jnp.max(logw, axis=1, keepdims=True), (L, LANES))
        m_out = jnp.maximum(m_inter, m_intra)
        p = (_dot_nt(q, k) * jnp.exp(logw - lanes(m_out, L))).astype(BF16)
        s_inter = jnp.exp(m_inter - m_out)
        num = _dot(p, v) + lanes(s_inter, dv) * _dot(q, ct0.astype(BF16))
        den = _dot(p, ones) + s_inter * _dot(q, n0.astype(BF16))
        inv = 1.0 / jnp.maximum(jnp.abs(den), jnp.exp(-m_out))
        o_ref[:, h * dv:(h + 1) * dv] = (num * lanes(inv, dv)).astype(o_ref.dtype)

        a_row = b_end + ig_row - b_row
        m_loc = jnp.max(a_row, axis=1, keepdims=True)
        kw_t = (k.T.astype(F32) * jnp.exp(a_row - m_loc)).astype(BF16)
        m_new = jnp.maximum(b_end + m0, m_loc)
        s_old = jnp.exp(b_end + m0 - m_new)
        s_new = jnp.exp(m_loc - m_new)
        ct_s[h] = lanes(s_old, dv) * ct0 + lanes(s_new, dv) * _dot(kw_t, v)
        n_s[h] = s_old * n0 + s_new * _dot(kw_t, ones)
        m_s[h:h + 1, :] = m_new

    bc_s[...] = bcn_s[...]
    bct_s[...] = bctn_s[...]
    gt_s[...] = gtn_s[...]


def _mlstm_call(qk, vo, gates, gate_b, n_lat, reverse):
    rows = qk.shape[0]
    L = ML_L
    nqk = qk.shape[1] // 2
    nv = vo.shape[1] // 2
    dk = nqk // ML_HEADS
    dv = nv // ML_HEADS
    nlc = n_lat // L
    ncc = (rows - n_lat) // L
    order = functools.partial(_chunk_order, n_lat_chunks=nlc, n_ctx_chunks=ncc, reverse=reverse)
    return pl.pallas_call(
        functools.partial(_mlstm_kernel, reverse=reverse),
        grid=(nlc + ncc,),
        in_specs=[pl.BlockSpec((L, nqk), lambda c: (order(c), 0)),
                  pl.BlockSpec((L, nqk), lambda c: (order(c), 1)),
                  pl.BlockSpec((L, nv), lambda c: (order(c), 0)),
                  pl.BlockSpec((L, LANES), lambda c: (order(c), 0)),
                  pl.BlockSpec((L, LANES), lambda c: (order(jnp.minimum(c + 1, nlc + ncc - 1)), 0)),
                  pl.BlockSpec((1, LANES), lambda c: (0, 0))],
        out_specs=pl.BlockSpec((L, nv), lambda c: (order(c), 0)),
        out_shape=jax.ShapeDtypeStruct((rows, nv), BF16),
        scratch_shapes=[pltpu.VMEM((ML_HEADS, dk, dv), F32),
                        pltpu.VMEM((ML_HEADS, dk, LANES), F32),
                        pltpu.VMEM((ML_HEADS, LANES), F32)]
        + 2 * [pltpu.VMEM((L, LANES), F32), pltpu.VMEM((LANES, L), F32), pltpu.VMEM((LANES, L), F32)],
        compiler_params=_cparams(("arbitrary",)),
    )(qk, qk, vo, gates, gates, gate_b)


def _ssd_kernel(x_ref, bm_ref, cm_ref, dt_ref, dtn_ref, dtb_ref, a_ref, dsk_ref, y_ref,
                st_s, bc_s, bct_s, dtt_s, bcn_s, bctn_s, dttn_s, *, reverse):
    L = x_ref.shape[0]
    nx = x_ref.shape[1]
    c = pl.program_id(0)
    mask = _tri(L, reverse)

    def prefix(dt_blk, bc_d, bct_d, dtt_d):
        dtv = _softplus(dt_blk + dtb_ref[...])
        bc = _dot_exact(mask.astype(F32), dtv * a_ref[...]) * LOG2E
        bc_d[...] = bc
        bct_d[...] = bc.T
        dtt_d[...] = jnp.log2(dtv).T

    @pl.when(c == 0)
    def _():
        st_s[...] = jnp.zeros_like(st_s)
        prefix(dt_ref[...], bc_s, bct_s, dtt_s)

    prefix(dtn_ref[...], bcn_s, bctn_s, dttn_s)

    bc = bc_s[...]
    bc_t = bct_s[...]
    ldt_t = dtt_s[...]
    end = 0 if reverse else L - 1
    nheads = nx // SSM_HEADDIM
    hpg = nheads // SSM_GROUPS
    base = nheads if reverse else 0
    pw = 2 * SSM_HEADDIM
    lo = lax.broadcasted_iota(jnp.int32, (1, pw), 1) < SSM_HEADDIM
    lo_b = jnp.where(lo, 1.0, 0.0).astype(BF16)
    hi_b = jnp.where(lo, 0.0, 1.0).astype(BF16)

    for g in range(SSM_GROUPS):
        bm = bm_ref[:, g * SSM_DSTATE:(g + 1) * SSM_DSTATE]
        cm = cm_ref[:, g * SSM_DSTATE:(g + 1) * SSM_DSTATE]
        cb = _dot_nt(cm, bm)
        bm_t = bm.astype(F32).T
        for pr in range(hpg // 2):
            col = (g * hpg + 2 * pr) * SSM_HEADDIM
            xp = x_ref[:, col:col + pw]
            ms, ebs, bts, decs = [], [], [], []
            for e in range(2):
                co = base + g * hpg + 2 * pr + e
                bb = jnp.broadcast_to(bc[:, co:co + 1], (L, L))
                r_row = bc_t[co:co + 1, :] - ldt_t[co:co + 1, :]
                b_end = bc[end:end + 1, co:co + 1]
                ms.append((cb * jnp.exp2(jnp.where(mask, bb - r_row, NEG_INF))).astype(BF16))
                ebs.append(jnp.exp2(bb[:, 0:pw]))
                bts.append((bm_t * jnp.exp2(b_end - r_row)).astype(BF16))
                decs.append(jnp.exp2(b_end))
            x_lo = xp * lo_b
            x_hi = xp * hi_b
            s0 = st_s[:, col:col + pw]
            y = _dot(ms[0], x_lo) + _dot(ms[1], x_hi)
            y = y + jnp.where(lo, ebs[0], ebs[1]) * _dot(cm, s0.astype(BF16))
            if not reverse:
                y = y + dsk_ref[:, col:col + pw] * xp.astype(F32)
            y_ref[:, col:col + pw] = y.astype(y_ref.dtype)
            st_s[:, col:col + pw] = (s0 * jnp.where(lo, decs[0], decs[1])
                                     + _dot(bts[0], x_lo) + _dot(bts[1], x_hi))

    bc_s[...] = bcn_s[...]
    bct_s[...] = bctn_s[...]
    dtt_s[...] = dttn_s[...]


def _ssd_call(xbc, dt, dt_bias, a_neg, d_skip, n_lat, reverse):
    rows = xbc.shape[0]
    L = SSM_L
    assert L >= 2 * SSM_HEADDIM
    nb = SSM_GROUPS * SSM_DSTATE
    nx = xbc.shape[1] - 2 * nb
    nlc = n_lat // L
    ncc = (rows - n_lat) // L
    order = functools.partial(_chunk_order, n_lat_chunks=nlc, n_ctx_chunks=ncc, reverse=reverse)
    xblk = nx // nb
    nc = nlc + ncc
    return pl.pallas_call(
        functools.partial(_ssd_kernel, reverse=reverse),
        grid=(nc,),
        in_specs=[pl.BlockSpec((L, nx), lambda c: (order(c), 0)),
                  pl.BlockSpec((L, nb), lambda c: (order(c), xblk)),
                  pl.BlockSpec((L, nb), lambda c: (order(c), xblk + 1)),
                  pl.BlockSpec((L, LANES), lambda c: (order(c), 0)),
                  pl.BlockSpec((L, LANES), lambda c: (order(jnp.minimum(c + 1, nc - 1)), 0)),
                  pl.BlockSpec((1, LANES), lambda c: (0, 0)),
                  pl.BlockSpec((1, LANES), lambda c: (0, 0)),
                  pl.BlockSpec((1, nx), lambda c: (0, 0))],
        out_specs=pl.BlockSpec((L, nx), lambda c: (order(c), 0)),
        out_shape=jax.ShapeDtypeStruct((rows, nx), BF16),
        scratch_shapes=[pltpu.VMEM((SSM_DSTATE, nx), F32)]
        + 2 * [pltpu.VMEM((L, LANES), F32), pltpu.VMEM((LANES, L), F32), pltpu.VMEM((LANES, L), F32)],
        compiler_params=_cparams(("arbitrary",)),
    )(xbc, xbc, xbc, dt, dt, dt_bias, a_neg, d_skip)


def _mlproj_kernel(hf_ref, hb_ref, o_ref, nw_ref, w_ref, y_ref, a_s):
    @pl.when(pl.program_id(1) == 0)
    def _():
        dv = a_s.shape[1] // ML_HEADS

        def body(rows):
            for h in range(ML_HEADS):
                sl = slice(h * dv, (h + 1) * dv)
                hh = _rms(hf_ref[rows, sl].astype(F32) + hb_ref[rows, sl].astype(F32)) * nw_ref[:, sl]
                a_s[rows, sl] = (_sigmoid(o_ref[rows, sl].astype(F32)) * hh).astype(BF16)
        _row_loop(a_s.shape[0], ROW_BLK, body)
    y_ref[...] = _dot(a_s[...], w_ref[...]).astype(y_ref.dtype)


def _mlproj_call(hf, hb, vo, norm_w, w, n_lat):
    nv = hf.shape[1]
    n = w.shape[1]
    tm = _pick(n_lat, (1024, 512, 256))
    tn = _pick(n, (1024, 512, 256, 128))
    row = pl.BlockSpec((tm, nv), lambda i, j: (i, 0))
    return pl.pallas_call(
        _mlproj_kernel,
        grid=(n_lat // tm, n // tn),
        in_specs=[row, row,
                  pl.BlockSpec((tm, nv), lambda i, j: (i, 1)),
                  pl.BlockSpec((1, nv), lambda i, j: (0, 0)),
                  pl.BlockSpec((nv, tn), lambda i, j: (0, j))],
        out_specs=pl.BlockSpec((tm, tn), lambda i, j: (i, j)),
        out_shape=jax.ShapeDtypeStruct((n_lat, n), BF16),
        scratch_shapes=[pltpu.VMEM((tm, nv), BF16)],
        compiler_params=_cparams(("arbitrary", "arbitrary")),
    )(hf, hb, vo, norm_w, w)


def _ssmproj_kernel(yf_ref, yb_ref, z_ref, nw_ref, w_ref, y_ref, a_s):
    @pl.when(pl.program_id(1) == 0)
    def _():
        gw = a_s.shape[1] // SSM_GROUPS

        def body(rows):
            for g in range(SSM_GROUPS):
                sl = slice(g * gw, (g + 1) * gw)
                y = (yf_ref[rows, sl].astype(F32) + yb_ref[rows, sl].astype(F32)) * _silu(z_ref[rows, sl].astype(F32))
                a_s[rows, sl] = (_rms(y) * nw_ref[:, sl]).astype(BF16)
        _row_loop(a_s.shape[0], ROW_BLK, body)
    y_ref[...] = _dot(a_s[...], w_ref[...]).astype(y_ref.dtype)


def _ssmproj_call(yf, yb, z, norm_w, w, n_lat):
    nx = yf.shape[1]
    n = w.shape[1]
    tm = _pick(n_lat, (512, 256))
    tn = _pick(n, (1024, 512, 256, 128))
    row = pl.BlockSpec((tm, nx), lambda i, j: (i, 0))
    return pl.pallas_call(
        _ssmproj_kernel,
        grid=(n_lat // tm, n // tn),
        in_specs=[row, row, row,
                  pl.BlockSpec((1, nx), lambda i, j: (0, 0)),
                  pl.BlockSpec((nx, tn), lambda i, j: (0, j))],
        out_specs=pl.BlockSpec((tm, tn), lambda i, j: (i, j)),
        out_shape=jax.ShapeDtypeStruct((n_lat, n), BF16),
        scratch_shapes=[pltpu.VMEM((tm, nx), BF16)],
        compiler_params=_cparams(("arbitrary", "arbitrary")),
    )(yf, yb, z, norm_w, w)


def _gate_kernel(h_ref, w1_ref, w2_ref, b1_ref, b2_ref, yml_ref, yssm_ref, o_ref):
    h = h_ref[...]
    g1 = _sigmoid(_dot(h, w1_ref[...]) + b1_ref[...])
    g2 = _sigmoid(_dot(h, w2_ref[...]) + b2_ref[...])
    o_ref[...] = (g1 * yml_ref[...].astype(F32) + g2 * yssm_ref[...].astype(F32)).astype(o_ref.dtype)


def _gate_call(h, w_gate, b_gate, y_ml, y_ssm, n_lat):
    d = h.shape[1]
    tm = _pick(n_lat, (1024, 512, 256))
    tn = _pick(d, (1024, 512, 256, 128))
    nj = d // tn
    tile = pl.BlockSpec((tm, tn), lambda i, j: (i, j))
    return pl.pallas_call(
        _gate_kernel,
        grid=(n_lat // tm, nj),
        in_specs=[pl.BlockSpec((tm, d), lambda i, j: (i, 0)),
                  pl.BlockSpec((d, tn), lambda i, j: (0, j)),
                  pl.BlockSpec((d, tn), lambda i, j: (0, nj + j)),
                  pl.BlockSpec((1, tn), lambda i, j: (0, j)),
                  pl.BlockSpec((1, tn), lambda i, j: (0, nj + j)),
                  tile, tile],
        out_specs=tile,
        out_shape=jax.ShapeDtypeStruct((n_lat, d), BF16),
        compiler_params=_cparams(("arbitrary", "arbitrary")),
    )(h, w_gate, w_gate, b_gate, b_gate, y_ml, y_ssm)


def _out_kernel(a_ref, w_ref, x_ref, g_ref, o_ref):
    o_ref[...] = x_ref[...] + g_ref[...] * _dot(a_ref[...], w_ref[...])


def _out_call(mix, w_out, x1, gate, n_lat):
    d = mix.shape[1]
    tm = _pick(n_lat, (1024, 512, 256))
    tn = _pick(d, (1024, 512, 256, 128))
    tile = pl.BlockSpec((tm, tn), lambda i, j: (i, j))
    return pl.pallas_call(
        _out_kernel,
        grid=(n_lat // tm, d // tn),
        in_specs=[pl.BlockSpec((tm, d), lambda i, j: (i, 0)),
                  pl.BlockSpec((d, tn), lambda i, j: (0, j)),
                  tile,
                  pl.BlockSpec((1, tn), lambda i, j: (0, j))],
        out_specs=tile,
        out_shape=jax.ShapeDtypeStruct((n_lat, d), F32),
        compiler_params=_cparams(("arbitrary", "arbitrary")),
    )(mix, w_out, x1, gate)


def _pad_rows(a, rows):
    return jnp.concatenate([a, jnp.zeros((rows - a.shape[0],) + a.shape[1:], a.dtype)], axis=0)


def _pad_cols(a, cols):
    return jnp.concatenate([a, jnp.zeros(a.shape[:-1] + (cols - a.shape[-1],), a.dtype)], axis=-1)


def kernel(x, c, ctx, c_ctx, w_ada, b_ada, norm_w, ffn_w_gate, ffn_w_up, ffn_w_down, w_in, ml_conv_w, ml_conv_b, ml_gate_b, ml_norm_w, w_proj_ml, ssm_conv_w, ssm_conv_b, ssm_dt_bias, ssm_a_log, ssm_d, ssm_norm_w, w_proj_ssm, w_gate, b_gate, w_out, final_norm_w):
    depth = w_ada.shape[0]
    assert depth == 1 and x.shape[0] == 1, "single layer, single batch element"
    t, d = x.shape[1], x.shape[2]
    n_tot = t + ctx.shape[1]
    rows_cm = t // GRID_W
    ml_qk = ml_conv_w.shape[2] // 2
    ml_v = ml_norm_w.shape[1]
    n_ml = 2 * ml_qk + 2 * ml_v
    ml_cols = n_ml + 4 * ML_HEADS
    ssm_inner = ssm_norm_w.shape[1]
    ssm_heads = ssm_d.shape[1]
    ssm_xbc = ssm_conv_w.shape[2]

    cc = _pad_rows(jnp.concatenate([c, c_ctx[None, :]], axis=0), 8)
    mod = _mod_call(cc, w_ada[0], b_ada[0])
    ml, mc = mod[0].reshape(N_MOD, d), mod[1].reshape(N_MOD, d)
    nw = norm_w[0]
    zero = jnp.zeros_like(nw[0])
    vec1 = _pad_rows(jnp.stack([nw[0], ml[0], ml[1], ml[2], mc[0], mc[1], mc[2],
                                nw[1], ml[3], ml[4], zero, mc[3], mc[4]]), 16)
    vec2 = _pad_rows(jnp.stack([nw[2], ml[6], ml[7], ml[8], zero, zero, zero, final_norm_w]), 16)
    wg, wu, wd = ffn_w_gate, ffn_w_up, ffn_w_down.astype(BF16)

    xall = jnp.concatenate([x[0], ctx[0]], axis=0)
    x1, h = _ffn_call(xall, vec1, wg, wu, wd, 0, t, final=False)

    h_cm = jnp.concatenate(
        [h[:t].reshape(rows_cm, GRID_W, d).transpose(1, 0, 2).reshape(t, d), h[t:]], axis=0)
    wt = w_in[0].T
    n_ssm = ssm_inner + ssm_xbc
    ml_cw = _pad_rows(ml_conv_w[0], 8)
    ml_post = jnp.concatenate([jnp.full((1, ml_qk), (ml_qk // ML_HEADS) ** -0.5, F32),
                               jnp.ones((1, ml_qk), F32)], axis=1)
    qk = _mmconv_call(_unit_permute(h_cm), wt, 0, ml_cw, ml_conv_b[0][None, :], ml_post, t)
    vo = _mm_call(h_cm, wt, 2 * ml_qk, 2 * ml_v, n_tot, BF16)
    gates = _mm_call(h_cm, wt, n_ml, LANES, n_tot, F32)
    z = _mm_call(h, wt, ml_cols, ssm_inner, t, BF16)
    xbc = _mmconv_call(_unit_permute(h), wt, ml_cols + ssm_inner, _pad_rows(ssm_conv_w[0], 8),
                       ssm_conv_b[0][None, :], None, t)
    dt = _mm_call(h, wt, ml_cols + n_ssm, 2 * ssm_heads, n_tot, F32)

    gate_b = _pad_cols(ml_gate_b[0].reshape(1, 4 * ML_HEADS), LANES)
    h_f = _mlstm_call(qk, vo, gates, gate_b, t, reverse=False)
    h_b = _mlstm_call(qk, vo, gates, gate_b, t, reverse=True)
    y_ml = _mlproj_call(h_f, h_b, vo, ml_norm_w[0][None, :], w_proj_ml[0].astype(BF16), t)
    y_ml = y_ml.reshape(GRID_W, rows_cm, d).transpose(1, 0, 2).reshape(t, d)

    dt_bias = ssm_dt_bias[0].reshape(1, 2 * ssm_heads)
    a_neg = -jnp.exp(ssm_a_log[0].astype(F32)).reshape(1, 2 * ssm_heads)
    d_skip = jnp.repeat(ssm_d[0], SSM_HEADDIM)[None, :]
    y_f = _ssd_call(xbc, dt, dt_bias, a_neg, d_skip, t, reverse=False)
    y_b = _ssd_call(xbc, dt, dt_bias, a_neg, d_skip, t, reverse=True)
    y_ssm = _ssmproj_call(y_f, y_b, z, ssm_norm_w[0][None, :], w_proj_ssm[0].astype(BF16), t)

    mix = _gate_call(h, w_gate[0].astype(BF16), b_gate[0][None, :], y_ml, y_ssm, t)
    x2 = _out_call(mix, w_out[0].astype(BF16), x1, ml[5][None, :], t)
    out = _ffn_call(x2, vec2, wg, wu, wd, 1, t, final=True)
    return out[None]
```

```python
import functools

import jax
import jax.numpy as jnp
from jax import lax
from jax.experimental import pallas as pl
from jax.experimental.pallas import tpu as pltpu

F32 = jnp.float32
BF16 = jnp.bfloat16

GRID_W = 64
N_MOD = 9
EPS = 1e-6
CONV_W = 5
CONV_PAD = CONV_W // 2
HALO = 16
CONV_UNIT = 256
ROW_BLK = 16
FFN_ROW_BLK = 32
FFN_PREFETCH_MAX_ROWS = 768
ML_HEADS = 8
ML_L = 256
SSM_HEADDIM = 64
SSM_GROUPS = 8
SSM_DSTATE = 128
SSM_L = 128
LANES = 128
VMEM_LIMIT_BYTES = 56 * 1024 * 1024
NEG_INF = float("-inf")
LOG2E = 1.4426950408889634


def _cparams(sem):
    return pltpu.CompilerParams(dimension_semantics=sem, vmem_limit_bytes=VMEM_LIMIT_BYTES)


def _pick(n, cands):
    for c in cands:
        if n % c == 0:
            return c
    raise ValueError(f"no tile for {n} in {cands}")


def _silu(x):
    return x * (1.0 / (1.0 + jnp.exp(-x)))


def _sigmoid(x):
    return 1.0 / (1.0 + jnp.exp(-x))


def _log_sigmoid(x):
    return jnp.minimum(x, 0.0) - jnp.log1p(jnp.exp(-jnp.abs(x)))


def _softplus(x):
    return jnp.maximum(x, 0.0) + jnp.log1p(jnp.exp(-jnp.abs(x)))


def _rms(x):
    return x * lax.rsqrt(jnp.mean(jnp.square(x), axis=-1, keepdims=True) + EPS)


def _dot(a, b):
    return jnp.dot(a, b, preferred_element_type=F32)


def _dot_nt(a, b):
    return lax.dot_general(a, b, (((1,), (1,)), ((), ())), preferred_element_type=F32)


def _dot_exact(a, b):
    return jnp.dot(a, b, preferred_element_type=F32, precision=lax.Precision.HIGHEST)


def _row_loop(n_rows, blk, body):
    def step(r, carry):
        body(pl.ds(pl.multiple_of(r * blk, blk), blk))
        return carry
    lax.fori_loop(0, n_rows // blk, step, 0, unroll=2)


def _mod_kernel(c_ref, w_ref, b_ref, o_ref):
    a = _silu(c_ref[...]).astype(BF16)
    o_ref[...] = _dot(a, w_ref[...].astype(BF16)) + b_ref[...]


def _mod_call(cc, w_ada, b_ada):
    k, n = w_ada.shape
    tn = _pick(n, (1024, 512, 256, 128))
    return pl.pallas_call(
        _mod_kernel,
        grid=(n // tn,),
        in_specs=[pl.BlockSpec((8, k), lambda j: (0, 0)),
                  pl.BlockSpec((k, tn), lambda j: (0, j)),
                  pl.BlockSpec((1, tn), lambda j: (0, j))],
        out_specs=pl.BlockSpec((8, tn), lambda j: (0, j)),
        out_shape=jax.ShapeDtypeStruct((8, n), F32),
        compiler_params=_cparams(("arbitrary",)),
    )(cc, w_ada, b_ada.reshape(1, n))


def _ffn_kernel(x_ref, vec_ref, wg_ref, wu_ref, wd_ref, *rest, n_lat, final):
    if final:
        o_ref, h_s, acc_s = rest
    else:
        x1_ref, h2_ref, h_s, acc_s = rest
    i = pl.program_id(0)
    f = pl.program_id(1)
    tm = x_ref.shape[0]

    def vec(row, rows):
        off = jnp.where(i * tm + rows.start >= n_lat, 3, 0)
        return vec_ref[pl.ds(row + off, 1), :]

    @pl.when(f == 0)
    def _():
        def body(rows):
            h = _rms(x_ref[rows, :]) * (vec_ref[0:1, :] * (1.0 + vec(2, rows))) + vec(1, rows)
            h_s[rows, :] = h.astype(BF16)
            acc_s[rows, :] = jnp.zeros((FFN_ROW_BLK, acc_s.shape[1]), F32)
        _row_loop(tm, FFN_ROW_BLK, body)

    h = h_s[...]
    a = _silu(_dot(h, wg_ref[...].astype(BF16))) * _dot(h, wu_ref[...].astype(BF16))
    acc_s[...] += _dot(a.astype(BF16), wd_ref[...])

    @pl.when(f == pl.num_programs(1) - 1)
    def _():
        def body(rows):
            x1 = x_ref[rows, :] + (0.5 * vec(3, rows)) * acc_s[rows, :]
            if final:
                o_ref[rows, :] = _rms(x1) * vec_ref[7:8, :]
            else:
                x1_ref[rows, :] = x1
                h2 = _rms(x1) * (vec_ref[7:8, :] * (1.0 + vec(9, rows))) + vec(8, rows)
                h2_ref[rows, :] = h2.astype(BF16)
        _row_loop(tm, FFN_ROW_BLK, body)


def _ffn_call(x, vecs, wg, wu, wd, which, n_lat, final):
    m, d = x.shape
    dff = wg.shape[3]
    tm = _pick(m, (1024, 768, 512, 256))
    tf = _pick(dff, (512, 256, 128))
    row = pl.BlockSpec((tm, d), lambda i, f: (i, 0), pipeline_mode=pl.Buffered(1))
    x_row = pl.BlockSpec((tm, d), lambda i, f: (i, 0)) if tm <= FFN_PREFETCH_MAX_ROWS else row
    if final:
        out_shape = jax.ShapeDtypeStruct((m, d), F32)
        out_specs = row
    else:
        out_shape = (jax.ShapeDtypeStruct((m, d), F32), jax.ShapeDtypeStruct((m, d), BF16))
        out_specs = (row, row)
    return pl.pallas_call(
        functools.partial(_ffn_kernel, n_lat=n_lat, final=final),
        grid=(m // tm, dff // tf),
        in_specs=[x_row,
                  pl.BlockSpec(vecs.shape, lambda i, f: (0, 0)),
                  pl.BlockSpec((None, None, d, tf), lambda i, f: (0, which, 0, f)),
                  pl.BlockSpec((None, None, d, tf), lambda i, f: (0, which, 0, f)),
                  pl.BlockSpec((None, None, tf, d), lambda i, f: (0, which, f, 0))],
        out_specs=out_specs,
        out_shape=out_shape,
        scratch_shapes=[pltpu.VMEM((tm, d), BF16), pltpu.VMEM((tm, d), F32)],
        compiler_params=_cparams(("arbitrary", "arbitrary")),
    )(x, vecs, wg, wu, wd)


def _mm_kernel(a_ref, w_ref, o_ref):
    o_ref[...] = _dot_nt(a_ref[...], w_ref[...].astype(BF16)).astype(o_ref.dtype)


def _mm_call(a, wt, row0, n, m, out_dtype):
    k = a.shape[1]
    tm = _pick(m, (1408, 1024, 768, 512, 256))
    tn = _pick(n, (1024, 512, 256, 128))
    assert row0 % 8 == 0
    return pl.pallas_call(
        _mm_kernel,
        grid=(m // tm, n // tn),
        in_specs=[pl.BlockSpec((tm, k), lambda i, j: (i, 0)),
                  pl.BlockSpec((pl.Element(tn), pl.Element(k)),
                               lambda i, j: (pl.multiple_of(row0 + j * tn, 8), 0))],
        out_specs=pl.BlockSpec((tm, tn), lambda i, j: (i, j)),
        out_shape=jax.ShapeDtypeStruct((m, n), out_dtype),
        compiler_params=_cparams(("arbitrary", "arbitrary")),
    )(a, wt)


def _mmconv_kernel(a_ref, ap_ref, an_ref, w_ref, cw_ref, cb_ref, *rest, n_lat, n_tot, scaled):
    if scaled:
        post_ref, o_ref, xs, ys, a_s = rest
    else:
        o_ref, xs, ys, a_s = rest
    i = pl.program_id(0)
    tm = a_ref.shape[0]
    tn = o_ref.shape[1]
    units = tm // CONV_UNIT
    ug = CONV_UNIT // 8
    span = CONV_UNIT + 4 * 8
    seq_units = n_lat // CONV_UNIT
    last_unit = n_tot // CONV_UNIT - 1

    @pl.when(pl.program_id(1) == 0)
    def _():
        a_s[0:HALO, :] = ap_ref[...]
        a_s[HALO:HALO + tm, :] = a_ref[...]
        a_s[HALO + tm:2 * HALO + tm, :] = an_ref[...]

    w = w_ref[...].astype(BF16)
    p_all = _dot_nt(a_s[...], w)
    p_prev = p_all[0:HALO]
    p = p_all[HALO:HALO + tm]
    p_next = p_all[HALO + tm:2 * HALO + tm]
    sub = lax.broadcasted_iota(jnp.int32, (8, 1), 0)

    for u in range(units):
        g = i * units + u
        keep_prev = jnp.where((g == 0) | (g == seq_units), 0.0, 1.0)
        keep_next = jnp.where((g == last_unit) | (g == seq_units - 1), 0.0, 1.0)
        r0 = u * CONV_UNIT
        base = u * span
        xs[base + 16:base + 16 + CONV_UNIT, :] = p[r0:r0 + CONV_UNIT]
        before = p[r0 - 16:r0] if u > 0 else p_prev
        after = p[r0 + CONV_UNIT:r0 + CONV_UNIT + 16] if u < units - 1 else p_next
        for d in range(2):
            own = pltpu.roll(p[r0 + (ug - 2 + d) * 8:r0 + (ug - 1 + d) * 8], 1, 0)
            other = pltpu.roll(before[d * 8:(d + 1) * 8], 1, 0) * keep_prev
            xs[base + d * 8:base + (d + 1) * 8, :] = jnp.where(sub == 0, other, own)
            own = pltpu.roll(p[r0 + d * 8:r0 + (d + 1) * 8], 7, 0)
            other = pltpu.roll(after[d * 8:(d + 1) * 8], 7, 0) * keep_next
            xs[base + 16 + CONV_UNIT + d * 8:base + 16 + CONV_UNIT + (d + 1) * 8, :] = (
                jnp.where(sub == 7, other, own))

        acc = cb_ref[...]
        for j in range(CONV_W):
            acc = acc + cw_ref[j:j + 1, :] * xs[base + 8 * j:base + 8 * j + CONV_UNIT, :]
        y = _silu(acc)
        if scaled:
            y = y * post_ref[...]
        for cblk in range(tn // LANES):
            lanes = slice(cblk * LANES, (cblk + 1) * LANES)
            ys[cblk, r0:r0 + CONV_UNIT, :] = y[:, lanes]
            for b8 in range(8):
                o_ref[r0 + b8 * ug:r0 + (b8 + 1) * ug, lanes] = (
                    ys[cblk, pl.ds(r0 + b8, ug, stride=8), :].astype(o_ref.dtype))


def _mmconv_call(a, w, col0, conv_w, conv_b, post, n_lat):
    m, k = a.shape
    n = conv_w.shape[1]
    tm = _pick(m, (768, 512, 256))
    tn = _pick(n, (512, 256, 128))
    assert col0 % 8 == 0 and tm % CONV_UNIT == 0 and n_lat % CONV_UNIT == 0 and m % CONV_UNIT == 0
    hb = tm // HALO
    last_halo = m // HALO - 1
    vec = pl.BlockSpec((1, tn), lambda i, j: (0, j))
    scaled = post is not None
    return pl.pallas_call(
        functools.partial(_mmconv_kernel, n_lat=n_lat, n_tot=m, scaled=scaled),
        grid=(m // tm, n // tn),
        in_specs=[pl.BlockSpec((tm, k), lambda i, j: (i, 0)),
                  pl.BlockSpec((HALO, k), lambda i, j: (jnp.maximum(i * hb - 1, 0), 0)),
                  pl.BlockSpec((HALO, k), lambda i, j: (jnp.minimum((i + 1) * hb, last_halo), 0)),
                  pl.BlockSpec((pl.Element(tn), pl.Element(k)),
                               lambda i, j: (pl.multiple_of(col0 + j * tn, 8), 0)),
                  pl.BlockSpec((8, tn), lambda i, j: (0, j)),
                  vec] + ([vec] if scaled else []),
        out_specs=pl.BlockSpec((tm, tn), lambda i, j: (i, j)),
        out_shape=jax.ShapeDtypeStruct((m, n), BF16),
        scratch_shapes=[pltpu.VMEM(((tm // CONV_UNIT) * (CONV_UNIT + 32), tn), F32),
                        pltpu.VMEM((tn // LANES, tm, LANES), F32),
                        pltpu.VMEM((tm + 2 * HALO, k), BF16)],
        compiler_params=_cparams(("arbitrary", "arbitrary")),
    )(a, a, a, w, conv_w, conv_b, *([post] if scaled else []))


def _unit_permute(x):
    rows, d = x.shape
    ug = CONV_UNIT // 8
    return x.reshape(rows // CONV_UNIT, 8, ug, d).transpose(0, 2, 1, 3).reshape(rows, d)


def _chunk_order(c, n_lat_chunks, n_ctx_chunks, reverse):
    if reverse:
        return n_lat_chunks + n_ctx_chunks - 1 - c
    return jnp.where(c < n_ctx_chunks, n_lat_chunks + c, c - n_ctx_chunks)


def _tri(chunk, reverse):
    r = lax.broadcasted_iota(jnp.int32, (chunk, chunk), 0)
    c = lax.broadcasted_iota(jnp.int32, (chunk, chunk), 1)
    return (c >= r) if reverse else (c <= r)


def _mlstm_kernel(q_ref, k_ref, v_ref, g_ref, gn_ref, gb_ref, o_ref, ct_s, n_s, m_s, bc_s, bct_s, gt_s,
                  bcn_s, bctn_s, gtn_s, *, reverse):
    L = q_ref.shape[0]
    dk = q_ref.shape[1] // ML_HEADS
    dv = v_ref.shape[1] // ML_HEADS
    c = pl.program_id(0)

    mask = _tri(L, reverse)

    def prefix(g_blk, bc_d, bct_d, gt_d):
        g = g_blk + gb_ref[...]
        bc = _dot_exact(mask.astype(F32), _log_sigmoid(g))
        bc_d[...] = bc
        bct_d[...] = bc.T
        gt_d[...] = g.T

    @pl.when(c == 0)
    def _():
        ct_s[...] = jnp.zeros_like(ct_s)
        n_s[...] = jnp.zeros_like(n_s)
        m_s[...] = jnp.zeros_like(m_s)
        prefix(g_ref[...], bc_s, bct_s, gt_s)

    prefix(gn_ref[...], bcn_s, bctn_s, gtn_s)

    def lanes(x, n):
        return jnp.concatenate([x] * (n // LANES), axis=1)

    bc = bc_s[...]
    bc_t = bct_s[...]
    g_t = gt_s[...]
    end = 0 if reverse else L - 1
    base = 2 * ML_HEADS if reverse else 0
    ones = jnp.ones((L, LANES), BF16)

    for h in range(ML_HEADS):
        io = base + h
        fo = base + ML_HEADS + h
        bb = jnp.broadcast_to(bc[:, fo:fo + 1], (L, LANES))
        b_row = bc_t[fo:fo + 1, :]
        ig_row = g_t[io:io + 1, :]
        b_end = bc[end:end + 1, fo:fo + 1]
        m0 = m_s[h:h + 1, :]
        q = q_ref[:, h * dk:(h + 1) * dk]
        k = k_ref[:, h * dk:(h + 1) * dk]
        v = v_ref[:, h * dv:(h + 1) * dv]
        ct0 = ct_s[h]
        n0 = n_s[h]

        logw = jnp.where(mask, lanes(bb, L) + (ig_row - b_row), NEG_INF)
        m_inter = bb + m0
        m_intra = jnp.broadcast_to(jnp.max(logw, axis=1, keepdims=True), (L, LANES))
        m_out = jnp.maximum(m_inter, m_intra)
        p = (_dot_nt(q, k) * jnp.exp(logw - lanes(m_out, L))).astype(BF16)
        s_inter = jnp.exp(m_inter - m_out)
        num = _dot(p, v) + lanes(s_inter, dv) * _dot(q, ct0.astype(BF16))
        den = _dot(p, ones) + s_inter * _dot(q, n0.astype(BF16))
        inv = 1.0 / jnp.maximum(jnp.abs(den), jnp.exp(-m_out))
        o_ref[:, h * dv:(h + 1) * dv] = (num * lanes(inv, dv)).astype(o_ref.dtype)

        a_row = b_end + ig_row - b_row
        m_loc = jnp.max(a_row, axis=1, keepdims=True)
        kw_t = (k.T.astype(F32) * jnp.exp(a_row - m_loc)).astype(BF16)
        m_new = jnp.maximum(b_end + m0, m_loc)
        s_old = jnp.exp(b_end + m0 - m_new)
        s_new = jnp.exp(m_loc - m_new)
        ct_s[h] = lanes(s_old, dv) * ct0 + lanes(s_new, dv) * _dot(kw_t, v)
        n_s[h] = s_old * n0 + s_new * _dot(kw_t, ones)
        m_s[h:h + 1, :] = m_new

    bc_s[...] = bcn_s[...]
    bct_s[...] = bctn_s[...]
    gt_s[...] = gtn_s[...]


def _mlstm_call(qk, vo, gates, gate_b, n_lat, reverse):
    rows = qk.shape[0]
    L = ML_L
    nqk = qk.shape[1] // 2
    nv = vo.shape[1] // 2
    dk = nqk // ML_HEADS
    dv = nv // ML_HEADS
    nlc = n_lat // L
    ncc = (rows - n_lat) // L
    order = functools.partial(_chunk_order, n_lat_chunks=nlc, n_ctx_chunks=ncc, reverse=reverse)
    return pl.pallas_call(
        functools.partial(_mlstm_kernel, reverse=reverse),
        grid=(nlc + ncc,),
        in_specs=[pl.BlockSpec((L, nqk), lambda c: (order(c), 0)),
                  pl.BlockSpec((L, nqk), lambda c: (order(c), 1)),
                  pl.BlockSpec((L, nv), lambda c: (order(c), 0)),
                  pl.BlockSpec((L, LANES), lambda c: (order(c), 0)),
                  pl.BlockSpec((L, LANES), lambda c: (order(jnp.minimum(c + 1, nlc + ncc - 1)), 0)),
                  pl.BlockSpec((1, LANES), lambda c: (0, 0))],
        out_specs=pl.BlockSpec((L, nv), lambda c: (order(c), 0)),
        out_shape=jax.ShapeDtypeStruct((rows, nv), BF16),
        scratch_shapes=[pltpu.VMEM((ML_HEADS, dk, dv), F32),
                        pltpu.VMEM((ML_HEADS, dk, LANES), F32),
                        pltpu.VMEM((ML_HEADS, LANES), F32)]
        + 2 * [pltpu.VMEM((L, LANES), F32), pltpu.VMEM((LANES, L), F32), pltpu.VMEM((LANES, L), F32)],
        compiler_params=_cparams(("arbitrary",)),
    )(qk, qk, vo, gates, gates, gate_b)


def _ssd_kernel(x_ref, bm_ref, cm_ref, dt_ref, dtn_ref, dtb_ref, a_ref, dsk_ref, y_ref,
                st_s, bc_s, bct_s, dtt_s, bcn_s, bctn_s, dttn_s, *, reverse):
    L = x_ref.shape[0]
    nx = x_ref.shape[1]
    c = pl.program_id(0)
    mask = _tri(L, reverse)

    def prefix(dt_blk, bc_d, bct_d, dtt_d):
        dtv = _softplus(dt_blk + dtb_ref[...])
        bc = _dot_exact(mask.astype(F32), dtv * a_ref[...]) * LOG2E
        bc_d[...] = bc
        bct_d[...] = bc.T
        dtt_d[...] = jnp.log2(dtv).T

    @pl.when(c == 0)
    def _():
        st_s[...] = jnp.zeros_like(st_s)
        prefix(dt_ref[...], bc_s, bct_s, dtt_s)

    prefix(dtn_ref[...], bcn_s, bctn_s, dttn_s)

    bc = bc_s[...]
    bc_t = bct_s[...]
    ldt_t = dtt_s[...]
    end = 0 if reverse else L - 1
    nheads = nx // SSM_HEADDIM
    hpg = nheads // SSM_GROUPS
    base = nheads if reverse else 0
    pw = 2 * SSM_HEADDIM
    lo = lax.broadcasted_iota(jnp.int32, (1, pw), 1) < SSM_HEADDIM
    lo_b = jnp.where(lo, 1.0, 0.0).astype(BF16)
    hi_b = jnp.where(lo, 0.0, 1.0).astype(BF16)

    for g in range(SSM_GROUPS):
        bm = bm_ref[:, g * SSM_DSTATE:(g + 1) * SSM_DSTATE]
        cm = cm_ref[:, g * SSM_DSTATE:(g + 1) * SSM_DSTATE]
        cb = _dot_nt(cm, bm)
        bm_t = bm.astype(F32).T
        for pr in range(hpg // 2):
            col = (g * hpg + 2 * pr) * SSM_HEADDIM
            xp = x_ref[:, col:col + pw]
            ms, ebs, bts, decs = [], [], [], []
            for e in range(2):
                co = base + g * hpg + 2 * pr + e
                bb = jnp.broadcast_to(bc[:, co:co + 1], (L, L))
                r_row = bc_t[co:co + 1, :] - ldt_t[co:co + 1, :]
                b_end = bc[end:end + 1, co:co + 1]
                ms.append((cb * jnp.exp2(jnp.where(mask, bb - r_row, NEG_INF))).astype(BF16))
                ebs.append(jnp.exp2(bb[:, 0:pw]))
                bts.append((bm_t * jnp.exp2(b_end - r_row)).astype(BF16))
                decs.append(jnp.exp2(b_end))
            x_lo = xp * lo_b
            x_hi = xp * hi_b
            s0 = st_s[:, col:col + pw]
            y = _dot(ms[0], x_lo) + _dot(ms[1], x_hi)
            y = y + jnp.where(lo, ebs[0], ebs[1]) * _dot(cm, s0.astype(BF16))
            if not reverse:
                y = y + dsk_ref[:, col:col + pw] * xp.astype(F32)
            y_ref[:, col:col + pw] = y.astype(y_ref.dtype)
            st_s[:, col:col + pw] = (s0 * jnp.where(lo, decs[0], decs[1])
                                     + _dot(bts[0], x_lo) + _dot(bts[1], x_hi))

    bc_s[...] = bcn_s[...]
    bct_s[...] = bctn_s[...]
    dtt_s[...] = dttn_s[...]


def _ssd_call(xbc, dt, dt_bias, a_neg, d_skip, n_lat, reverse):
    rows = xbc.shape[0]
    L = SSM_L
    assert L >= 2 * SSM_HEADDIM
    nb = SSM_GROUPS * SSM_DSTATE
    nx = xbc.shape[1] - 2 * nb
    nlc = n_lat // L
    ncc = (rows - n_lat) // L
    order = functools.partial(_chunk_order, n_lat_chunks=nlc, n_ctx_chunks=ncc, reverse=reverse)
    xblk = nx // nb
    nc = nlc + ncc
    return pl.pallas_call(
        functools.partial(_ssd_kernel, reverse=reverse),
        grid=(nc,),
        in_specs=[pl.BlockSpec((L, nx), lambda c: (order(c), 0)),
                  pl.BlockSpec((L, nb), lambda c: (order(c), xblk)),
                  pl.BlockSpec((L, nb), lambda c: (order(c), xblk + 1)),
                  pl.BlockSpec((L, LANES), lambda c: (order(c), 0)),
                  pl.BlockSpec((L, LANES), lambda c: (order(jnp.minimum(c + 1, nc - 1)), 0)),
                  pl.BlockSpec((1, LANES), lambda c: (0, 0)),
                  pl.BlockSpec((1, LANES), lambda c: (0, 0)),
                  pl.BlockSpec((1, nx), lambda c: (0, 0))],
        out_specs=pl.BlockSpec((L, nx), lambda c: (order(c), 0)),
        out_shape=jax.ShapeDtypeStruct((rows, nx), BF16),
        scratch_shapes=[pltpu.VMEM((SSM_DSTATE, nx), F32)]
        + 2 * [pltpu.VMEM((L, LANES), F32), pltpu.VMEM((LANES, L), F32), pltpu.VMEM((LANES, L), F32)],
        compiler_params=_cparams(("arbitrary",)),
    )(xbc, xbc, xbc, dt, dt, dt_bias, a_neg, d_skip)


def _mlgate_kernel(hf_ref, hb_ref, o_ref, nw_ref, a_ref):
    dv = a_ref.shape[1] // ML_HEADS

    def body(rows):
        for h in range(ML_HEADS):
            sl = slice(h * dv, (h + 1) * dv)
            hh = _rms(hf_ref[rows, sl].astype(F32) + hb_ref[rows, sl].astype(F32)) * nw_ref[:, sl]
            a_ref[rows, sl] = (_sigmoid(o_ref[rows, sl].astype(F32)) * hh).astype(BF16)
    _row_loop(a_ref.shape[0], ROW_BLK, body)


def _mlgate_call(hf, hb, vo, norm_w, n_lat):
    nv = hf.shape[1]
    tm = _pick(n_lat, (512, 256))
    row = pl.BlockSpec((tm, nv), lambda i: (i, 0))
    return pl.pallas_call(
        _mlgate_kernel,
        grid=(n_lat // tm,),
        in_specs=[row, row, pl.BlockSpec((tm, nv), lambda i: (i, 1)), pl.BlockSpec((1, nv), lambda i: (0, 0))],
        out_specs=row,
        out_shape=jax.ShapeDtypeStruct((n_lat, nv), BF16),
        compiler_params=_cparams(("arbitrary",)),
    )(hf, hb, vo, norm_w)


def _ssmgate_kernel(yf_ref, yb_ref, z_ref, nw_ref, a_ref):
    gw = a_ref.shape[1] // SSM_GROUPS

    def body(rows):
        for g in range(SSM_GROUPS):
            sl = slice(g * gw, (g + 1) * gw)
            y = (yf_ref[rows, sl].astype(F32) + yb_ref[rows, sl].astype(F32)) * _silu(z_ref[rows, sl].astype(F32))
            a_ref[rows, sl] = (_rms(y) * nw_ref[:, sl]).astype(BF16)
    _row_loop(a_ref.shape[0], ROW_BLK, body)


def _ssmgate_call(yf, yb, z, norm_w, n_lat):
    nx = yf.shape[1]
    tm = _pick(n_lat, (512, 256))
    row = pl.BlockSpec((tm, nx), lambda i: (i, 0))
    return pl.pallas_call(
        _ssmgate_kernel,
        grid=(n_lat // tm,),
        in_specs=[row, row, row, pl.BlockSpec((1, nx), lambda i: (0, 0))],
        out_specs=row,
        out_shape=jax.ShapeDtypeStruct((n_lat, nx), BF16),
        compiler_params=_cparams(("arbitrary",)),
    )(yf, yb, z, norm_w)


def _proj_kernel(a_ref, w_ref, y_ref):
    y_ref[...] = _dot(a_ref[...], w_ref[...]).astype(y_ref.dtype)


def _proj_call(a, w):
    m, k = a.shape
    n = w.shape[1]
    tm = _pick(m, (1024, 512, 256))
    tn = _pick(n, (1024, 512, 256, 128))
    return pl.pallas_call(
        _proj_kernel,
        grid=(m // tm, n // tn),
        in_specs=[pl.BlockSpec((tm, k), lambda i, j: (i, 0)),
                  pl.BlockSpec((k, tn), lambda i, j: (0, j))],
        out_specs=pl.BlockSpec((tm, tn), lambda i, j: (i, j)),
        out_shape=jax.ShapeDtypeStruct((m, n), BF16),
        compiler_params=_cparams(("arbitrary", "arbitrary")),
    )(a, w)


def _gate_kernel(h_ref, w1_ref, w2_ref, b1_ref, b2_ref, yml_ref, yssm_ref, o_ref):
    h = h_ref[...]
    g1 = _sigmoid(_dot(h, w1_ref[...]) + b1_ref[...])
    g2 = _sigmoid(_dot(h, w2_ref[...]) + b2_ref[...])
    o_ref[...] = (g1 * yml_ref[...].astype(F32) + g2 * yssm_ref[...].astype(F32)).astype(o_ref.dtype)


def _gate_call(h, w_gate, b_gate, y_ml, y_ssm, n_lat):
    d = h.shape[1]
    tm = _pick(n_lat, (1024, 512, 256))
    tn = _pick(d, (1024, 512, 256, 128))
    nj = d // tn
    tile = pl.BlockSpec((tm, tn), lambda i, j: (i, j))
    return pl.pallas_call(
        _gate_kernel,
        grid=(n_lat // tm, nj),
        in_specs=[pl.BlockSpec((tm, d), lambda i, j: (i, 0)),
                  pl.BlockSpec((d, tn), lambda i, j: (0, j)),
                  pl.BlockSpec((d, tn), lambda i, j: (0, nj + j)),
                  pl.BlockSpec((1, tn), lambda i, j: (0, j)),
                  pl.BlockSpec((1, tn), lambda i, j: (0, nj + j)),
                  tile, tile],
        out_specs=tile,
        out_shape=jax.ShapeDtypeStruct((n_lat, d), BF16),
        compiler_params=_cparams(("arbitrary", "arbitrary")),
    )(h, w_gate, w_gate, b_gate, b_gate, y_ml, y_ssm)


def _out_kernel(a_ref, w_ref, x_ref, g_ref, o_ref):
    o_ref[...] = x_ref[...] + g_ref[...] * _dot(a_ref[...], w_ref[...])


def _out_call(mix, w_out, x1, gate, n_lat):
    d = mix.shape[1]
    tm = _pick(n_lat, (1024, 512, 256))
    tn = _pick(d, (1024, 512, 256, 128))
    tile = pl.BlockSpec((tm, tn), lambda i, j: (i, j))
    return pl.pallas_call(
        _out_kernel,
        grid=(n_lat // tm, d // tn),
        in_specs=[pl.BlockSpec((tm, d), lambda i, j: (i, 0)),
                  pl.BlockSpec((d, tn), lambda i, j: (0, j)),
                  tile,
                  pl.BlockSpec((1, tn), lambda i, j: (0, j))],
        out_specs=tile,
        out_shape=jax.ShapeDtypeStruct((n_lat, d), F32),
        compiler_params=_cparams(("arbitrary", "arbitrary")),
    )(mix, w_out, x1, gate)


def _pad_rows(a, rows):
    return jnp.concatenate([a, jnp.zeros((rows - a.shape[0],) + a.shape[1:], a.dtype)], axis=0)


def _pad_cols(a, cols):
    return jnp.concatenate([a, jnp.zeros(a.shape[:-1] + (cols - a.shape[-1],), a.dtype)], axis=-1)


def kernel(x, c, ctx, c_ctx, w_ada, b_ada, norm_w, ffn_w_gate, ffn_w_up, ffn_w_down, w_in, ml_conv_w, ml_conv_b, ml_gate_b, ml_norm_w, w_proj_ml, ssm_conv_w, ssm_conv_b, ssm_dt_bias, ssm_a_log, ssm_d, ssm_norm_w, w_proj_ssm, w_gate, b_gate, w_out, final_norm_w):
    depth = w_ada.shape[0]
    assert depth == 1 and x.shape[0] == 1, "single layer, single batch element"
    t, d = x.shape[1], x.shape[2]
    n_tot = t + ctx.shape[1]
    rows_cm = t // GRID_W
    ml_qk = ml_conv_w.shape[2] // 2
    ml_v = ml_norm_w.shape[1]
    n_ml = 2 * ml_qk + 2 * ml_v
    ml_cols = n_ml + 4 * ML_HEADS
    ssm_inner = ssm_norm_w.shape[1]
    ssm_heads = ssm_d.shape[1]
    ssm_xbc = ssm_conv_w.shape[2]

    cc = _pad_rows(jnp.concatenate([c, c_ctx[None, :]], axis=0), 8)
    mod = _mod_call(cc, w_ada[0], b_ada[0])
    ml, mc = mod[0].reshape(N_MOD, d), mod[1].reshape(N_MOD, d)
    nw = norm_w[0]
    zero = jnp.zeros_like(nw[0])
    vec1 = _pad_rows(jnp.stack([nw[0], ml[0], ml[1], ml[2], mc[0], mc[1], mc[2],
                                nw[1], ml[3], ml[4], zero, mc[3], mc[4]]), 16)
    vec2 = _pad_rows(jnp.stack([nw[2], ml[6], ml[7], ml[8], zero, zero, zero, final_norm_w]), 16)
    wg, wu, wd = ffn_w_gate, ffn_w_up, ffn_w_down.astype(BF16)

    xall = jnp.concatenate([x[0], ctx[0]], axis=0)
    x1, h = _ffn_call(xall, vec1, wg, wu, wd, 0, t, final=False)

    h_cm = jnp.concatenate(
        [h[:t].reshape(rows_cm, GRID_W, d).transpose(1, 0, 2).reshape(t, d), h[t:]], axis=0)
    wt = w_in[0].T
    n_ssm = ssm_inner + ssm_xbc
    ml_cw = _pad_rows(ml_conv_w[0], 8)
    ml_post = jnp.concatenate([jnp.full((1, ml_qk), (ml_qk // ML_HEADS) ** -0.5, F32),
                               jnp.ones((1, ml_qk), F32)], axis=1)
    qk = _mmconv_call(_unit_permute(h_cm), wt, 0, ml_cw, ml_conv_b[0][None, :], ml_post, t)
    vo = _mm_call(h_cm, wt, 2 * ml_qk, 2 * ml_v, n_tot, BF16)
    gates = _mm_call(h_cm, wt, n_ml, LANES, n_tot, F32)
    z = _mm_call(h, wt, ml_cols, ssm_inner, t, BF16)
    xbc = _mmconv_call(_unit_permute(h), wt, ml_cols + ssm_inner, _pad_rows(ssm_conv_w[0], 8),
                       ssm_conv_b[0][None, :], None, t)
    dt = _mm_call(h, wt, ml_cols + n_ssm, 2 * ssm_heads, n_tot, F32)

    gate_b = _pad_cols(ml_gate_b[0].reshape(1, 4 * ML_HEADS), LANES)
    h_f = _mlstm_call(qk, vo, gates, gate_b, t, reverse=False)
    h_b = _mlstm_call(qk, vo, gates, gate_b, t, reverse=True)
    y_ml = _proj_call(_mlgate_call(h_f, h_b, vo, ml_norm_w[0][None, :], t), w_proj_ml[0].astype(BF16))
    y_ml = y_ml.reshape(GRID_W, rows_cm, d).transpose(1, 0, 2).reshape(t, d)

    dt_bias = ssm_dt_bias[0].reshape(1, 2 * ssm_heads)
    a_neg = -jnp.exp(ssm_a_log[0].astype(F32)).reshape(1, 2 * ssm_heads)
    d_skip = jnp.repeat(ssm_d[0], SSM_HEADDIM)[None, :]
    y_f = _ssd_call(xbc, dt, dt_bias, a_neg, d_skip, t, reverse=False)
    y_b = _ssd_call(xbc, dt, dt_bias, a_neg, d_skip, t, reverse=True)
    y_ssm = _proj_call(_ssmgate_call(y_f, y_b, z, ssm_norm_w[0][None, :], t), w_proj_ssm[0].astype(BF16))

    mix = _gate_call(h, w_gate[0].astype(BF16), b_gate[0][None, :], y_ml, y_ssm, t)
    x2 = _out_call(mix, w_out[0].astype(BF16), x1, ml[5][None, :], t)
    out = _ffn_call(x2, vec2, wg, wu, wd, 1, t, final=True)
    return out[None]
```

```python
import functools

import jax
import jax.numpy as jnp
from jax import lax
from jax.experimental import pallas as pl
from jax.experimental.pallas import tpu as pltpu

F32 = jnp.float32
BF16 = jnp.bfloat16

GRID_W = 64
N_MOD = 9
EPS = 1e-6
CONV_W = 5
CONV_PAD = CONV_W // 2
HALO = 16
CONV_UNIT = 256
ROW_BLK = 16
FFN_ROW_BLK = 32
FFN_PREFETCH_MAX_ROWS = 768
W_TILE_MAX_BYTES = 8 * 1024 * 1024
ML_HEADS = 8
ML_L = 256
SSM_HEADDIM = 64
SSM_GROUPS = 8
SSM_DSTATE = 128
SSM_L = 128
LANES = 128
VMEM_LIMIT_BYTES = 56 * 1024 * 1024
NEG_INF = float("-inf")
LOG2E = 1.4426950408889634


def _cparams(sem):
    return pltpu.CompilerParams(dimension_semantics=sem, vmem_limit_bytes=VMEM_LIMIT_BYTES)


def _pick(n, cands):
    for c in cands:
        if n % c == 0:
            return c
    raise ValueError(f"no tile for {n} in {cands}")


def _silu(x):
    return x * (1.0 / (1.0 + jnp.exp(-x)))


def _sigmoid(x):
    return 1.0 / (1.0 + jnp.exp(-x))


def _log_sigmoid(x):
    return jnp.minimum(x, 0.0) - jnp.log1p(jnp.exp(-jnp.abs(x)))


def _softplus(x):
    return jnp.maximum(x, 0.0) + jnp.log1p(jnp.exp(-jnp.abs(x)))


def _rms(x):
    return x * lax.rsqrt(jnp.mean(jnp.square(x), axis=-1, keepdims=True) + EPS)


def _dot(a, b):
    return jnp.dot(a, b, preferred_element_type=F32)


def _dot_nt(a, b):
    return lax.dot_general(a, b, (((1,), (1,)), ((), ())), preferred_element_type=F32)


def _dot_exact(a, b):
    return jnp.dot(a, b, preferred_element_type=F32, precision=lax.Precision.HIGHEST)


def _row_loop(n_rows, blk, body):
    def step(r, carry):
        body(pl.ds(pl.multiple_of(r * blk, blk), blk))
        return carry
    lax.fori_loop(0, n_rows // blk, step, 0, unroll=2)


def _mod_kernel(c_ref, w_ref, b_ref, o_ref):
    a = _silu(c_ref[...]).astype(BF16)
    o_ref[...] = _dot(a, w_ref[...].astype(BF16)) + b_ref[...]


def _mod_call(cc, w_ada, b_ada):
    k, n = w_ada.shape
    tn = _pick(n, (1024, 512, 256, 128))
    return pl.pallas_call(
        _mod_kernel,
        grid=(n // tn,),
        in_specs=[pl.BlockSpec((8, k), lambda j: (0, 0)),
                  pl.BlockSpec((k, tn), lambda j: (0, j)),
                  pl.BlockSpec((1, tn), lambda j: (0, j))],
        out_specs=pl.BlockSpec((8, tn), lambda j: (0, j)),
        out_shape=jax.ShapeDtypeStruct((8, n), F32),
        compiler_params=_cparams(("arbitrary",)),
    )(cc, w_ada, b_ada.reshape(1, n))


def _ffn_kernel(x_ref, vec_ref, wg_ref, wu_ref, wd_ref, *rest, n_lat, final):
    if final:
        o_ref, h_s, acc_s = rest
    else:
        x1_ref, h2_ref, h_s, acc_s = rest
    i = pl.program_id(0)
    f = pl.program_id(1)
    tm = x_ref.shape[0]

    def vec(row, rows):
        off = jnp.where(i * tm + rows.start >= n_lat, 3, 0)
        return vec_ref[pl.ds(row + off, 1), :]

    @pl.when(f == 0)
    def _():
        def body(rows):
            h = _rms(x_ref[rows, :]) * (vec_ref[0:1, :] * (1.0 + vec(2, rows))) + vec(1, rows)
            h_s[rows, :] = h.astype(BF16)
            acc_s[rows, :] = jnp.zeros((FFN_ROW_BLK, acc_s.shape[1]), F32)
        _row_loop(tm, FFN_ROW_BLK, body)

    h = h_s[...]
    a = _silu(_dot(h, wg_ref[...].astype(BF16))) * _dot(h, wu_ref[...].astype(BF16))
    acc_s[...] += _dot(a.astype(BF16), wd_ref[...])

    @pl.when(f == pl.num_programs(1) - 1)
    def _():
        def body(rows):
            x1 = x_ref[rows, :] + (0.5 * vec(3, rows)) * acc_s[rows, :]
            if final:
                o_ref[rows, :] = _rms(x1) * vec_ref[7:8, :]
            else:
                x1_ref[rows, :] = x1
                h2 = _rms(x1) * (vec_ref[7:8, :] * (1.0 + vec(9, rows))) + vec(8, rows)
                h2_ref[rows, :] = h2.astype(BF16)
        _row_loop(tm, FFN_ROW_BLK, body)


def _ffn_call(x, vecs, wg, wu, wd, which, n_lat, final):
    m, d = x.shape
    dff = wg.shape[3]
    tm = _pick(m, (1024, 768, 512, 256))
    tf = _pick(dff, (512, 256, 128))
    row = pl.BlockSpec((tm, d), lambda i, f: (i, 0), pipeline_mode=pl.Buffered(1))
    x_row = pl.BlockSpec((tm, d), lambda i, f: (i, 0)) if tm <= FFN_PREFETCH_MAX_ROWS else row
    if final:
        out_shape = jax.ShapeDtypeStruct((m, d), F32)
        out_specs = row
    else:
        out_shape = (jax.ShapeDtypeStruct((m, d), F32), jax.ShapeDtypeStruct((m, d), BF16))
        out_specs = (row, row)
    return pl.pallas_call(
        functools.partial(_ffn_kernel, n_lat=n_lat, final=final),
        grid=(m // tm, dff // tf),
        in_specs=[x_row,
                  pl.BlockSpec(vecs.shape, lambda i, f: (0, 0)),
                  pl.BlockSpec((None, None, d, tf), lambda i, f: (0, which, 0, f)),
                  pl.BlockSpec((None, None, d, tf), lambda i, f: (0, which, 0, f)),
                  pl.BlockSpec((None, None, tf, d), lambda i, f: (0, which, f, 0))],
        out_specs=out_specs,
        out_shape=out_shape,
        scratch_shapes=[pltpu.VMEM((tm, d), BF16), pltpu.VMEM((tm, d), F32)],
        compiler_params=_cparams(("arbitrary", "arbitrary")),
    )(x, vecs, wg, wu, wd)


def _mm_kernel(a_ref, w_ref, o_ref):
    o_ref[...] = _dot_nt(a_ref[...], w_ref[...].astype(BF16)).astype(o_ref.dtype)


def _mm_call(a, wt, row0, n, m, out_dtype):
    k = a.shape[1]
    tm = _pick(m, (1408, 1024, 768, 512, 256))
    tn = _pick(n, (1024, 512, 256, 128))
    assert row0 % 8 == 0
    return pl.pallas_call(
        _mm_kernel,
        grid=(m // tm, n // tn),
        in_specs=[pl.BlockSpec((tm, k), lambda i, j: (i, 0)),
                  pl.BlockSpec((pl.Element(tn), pl.Element(k)),
                               lambda i, j: (pl.multiple_of(row0 + j * tn, 8), 0))],
        out_specs=pl.BlockSpec((tm, tn), lambda i, j: (i, j)),
        out_shape=jax.ShapeDtypeStruct((m, n), out_dtype),
        compiler_params=_cparams(("arbitrary", "arbitrary")),
    )(a, wt)


def _mmconv_kernel(a_ref, ap_ref, an_ref, w_ref, cw_ref, cb_ref, *rest, n_lat, n_tot, scaled):
    if scaled:
        post_ref, o_ref, xs, ys, a_s = rest
    else:
        o_ref, xs, ys, a_s = rest
    i = pl.program_id(0)
    tm = a_ref.shape[0]
    tn = o_ref.shape[1]
    units = tm // CONV_UNIT
    ug = CONV_UNIT // 8
    span = CONV_UNIT + 4 * 8
    seq_units = n_lat // CONV_UNIT
    last_unit = n_tot // CONV_UNIT - 1

    @pl.when(pl.program_id(1) == 0)
    def _():
        a_s[0:HALO, :] = ap_ref[...]
        a_s[HALO:HALO + tm, :] = a_ref[...]
        a_s[HALO + tm:2 * HALO + tm, :] = an_ref[...]

    w = w_ref[...].astype(BF16)
    p_all = _dot_nt(a_s[...], w)
    p_prev = p_all[0:HALO]
    p = p_all[HALO:HALO + tm]
    p_next = p_all[HALO + tm:2 * HALO + tm]
    sub = lax.broadcasted_iota(jnp.int32, (8, 1), 0)

    for u in range(units):
        g = i * units + u
        keep_prev = jnp.where((g == 0) | (g == seq_units), 0.0, 1.0)
        keep_next = jnp.where((g == last_unit) | (g == seq_units - 1), 0.0, 1.0)
        r0 = u * CONV_UNIT
        base = u * span
        xs[base + 16:base + 16 + CONV_UNIT, :] = p[r0:r0 + CONV_UNIT]
        before = p[r0 - 16:r0] if u > 0 else p_prev
        after = p[r0 + CONV_UNIT:r0 + CONV_UNIT + 16] if u < units - 1 else p_next
        for d in range(2):
            own = pltpu.roll(p[r0 + (ug - 2 + d) * 8:r0 + (ug - 1 + d) * 8], 1, 0)
            other = pltpu.roll(before[d * 8:(d + 1) * 8], 1, 0) * keep_prev
            xs[base + d * 8:base + (d + 1) * 8, :] = jnp.where(sub == 0, other, own)
            own = pltpu.roll(p[r0 + d * 8:r0 + (d + 1) * 8], 7, 0)
            other = pltpu.roll(after[d * 8:(d + 1) * 8], 7, 0) * keep_next
            xs[base + 16 + CONV_UNIT + d * 8:base + 16 + CONV_UNIT + (d + 1) * 8, :] = (
                jnp.where(sub == 7, other, own))

        acc = cb_ref[...]
        for j in range(CONV_W):
            acc = acc + cw_ref[j:j + 1, :] * xs[base + 8 * j:base + 8 * j + CONV_UNIT, :]
        y = _silu(acc)
        if scaled:
            y = y * post_ref[...]
        for cblk in range(tn // LANES):
            lanes = slice(cblk * LANES, (cblk + 1) * LANES)
            ys[cblk, r0:r0 + CONV_UNIT, :] = y[:, lanes]
            for b8 in range(8):
                o_ref[r0 + b8 * ug:r0 + (b8 + 1) * ug, lanes] = (
                    ys[cblk, pl.ds(r0 + b8, ug, stride=8), :].astype(o_ref.dtype))


def _mmconv_call(a, w, col0, conv_w, conv_b, post, n_lat):
    m, k = a.shape
    n = conv_w.shape[1]
    tm = _pick(m, (768, 512, 256))
    tn = _pick(n, (512, 256, 128))
    assert col0 % 8 == 0 and tm % CONV_UNIT == 0 and n_lat % CONV_UNIT == 0 and m % CONV_UNIT == 0
    hb = tm // HALO
    last_halo = m // HALO - 1
    vec = pl.BlockSpec((1, tn), lambda i, j: (0, j))
    scaled = post is not None
    return pl.pallas_call(
        functools.partial(_mmconv_kernel, n_lat=n_lat, n_tot=m, scaled=scaled),
        grid=(m // tm, n // tn),
        in_specs=[pl.BlockSpec((tm, k), lambda i, j: (i, 0)),
                  pl.BlockSpec((HALO, k), lambda i, j: (jnp.maximum(i * hb - 1, 0), 0)),
                  pl.BlockSpec((HALO, k), lambda i, j: (jnp.minimum((i + 1) * hb, last_halo), 0)),
                  pl.BlockSpec((pl.Element(tn), pl.Element(k)),
                               lambda i, j: (pl.multiple_of(col0 + j * tn, 8), 0)),
                  pl.BlockSpec((8, tn), lambda i, j: (0, j)),
                  vec] + ([vec] if scaled else []),
        out_specs=pl.BlockSpec((tm, tn), lambda i, j: (i, j)),
        out_shape=jax.ShapeDtypeStruct((m, n), BF16),
        scratch_shapes=[pltpu.VMEM(((tm // CONV_UNIT) * (CONV_UNIT + 32), tn), F32),
                        pltpu.VMEM((tn // LANES, tm, LANES), F32),
                        pltpu.VMEM((tm + 2 * HALO, k), BF16)],
        compiler_params=_cparams(("arbitrary", "arbitrary")),
    )(a, a, a, w, conv_w, conv_b, *([post] if scaled else []))


def _unit_permute(x):
    rows, d = x.shape
    ug = CONV_UNIT // 8
    return x.reshape(rows // CONV_UNIT, 8, ug, d).transpose(0, 2, 1, 3).reshape(rows, d)


def _colmajor_units(h, n_lat, rows_cm):
    d = h.shape[1]
    ug = CONV_UNIT // 8
    if CONV_UNIT % rows_cm or rows_cm % ug:
        lat = h[:n_lat].reshape(rows_cm, GRID_W, d).transpose(1, 0, 2).reshape(n_lat, d)
        return _unit_permute(jnp.concatenate([lat, h[n_lat:]], axis=0))
    cpu = CONV_UNIT // rows_cm
    lat = h[:n_lat].reshape(rows_cm // ug, ug, GRID_W // cpu, cpu, d).transpose(2, 1, 3, 0, 4).reshape(n_lat, d)
    return jnp.concatenate([lat, _unit_permute(h[n_lat:])], axis=0)


def _chunk_order(c, n_lat_chunks, n_ctx_chunks, reverse):
    if reverse:
        return n_lat_chunks + n_ctx_chunks - 1 - c
    return jnp.where(c < n_ctx_chunks, n_lat_chunks + c, c - n_ctx_chunks)


def _tri(chunk, reverse):
    r = lax.broadcasted_iota(jnp.int32, (chunk, chunk), 0)
    c = lax.broadcasted_iota(jnp.int32, (chunk, chunk), 1)
    return (c >= r) if reverse else (c <= r)


def _mlstm_kernel(q_ref, k_ref, v_ref, g_ref, gn_ref, gb_ref, o_ref, ct_s, n_s, m_s, bc_s, bct_s, gt_s,
                  bcn_s, bctn_s, gtn_s, *, reverse):
    L = q_ref.shape[0]
    dk = q_ref.shape[1] // ML_HEADS
    dv = v_ref.shape[1] // ML_HEADS
    c = pl.program_id(0)

    mask = _tri(L, reverse)

    def prefix(g_blk, bc_d, bct_d, gt_d):
        g = g_blk + gb_ref[...]
        bc = _dot_exact(mask.astype(F32), _log_sigmoid(g))
        bc_d[...] = bc
        bct_d[...] = bc.T
        gt_d[...] = g.T

    @pl.when(c == 0)
    def _():
        ct_s[...] = jnp.zeros_like(ct_s)
        n_s[...] = jnp.zeros_like(n_s)
        m_s[...] = jnp.zeros_like(m_s)
        prefix(g_ref[...], bc_s, bct_s, gt_s)

    prefix(gn_ref[...], bcn_s, bctn_s, gtn_s)

    def lanes(x, n):
        return jnp.concatenate([x] * (n // LANES), axis=1)

    bc = bc_s[...]
    bc_t = bct_s[...]
    g_t = gt_s[...]
    end = 0 if reverse else L - 1
    base = 2 * ML_HEADS if reverse else 0
    ones = jnp.ones((L, LANES), BF16)

    for h in range(ML_HEADS):
        io = base + h
        fo = base + ML_HEADS + h
        bb = jnp.broadcast_to(bc[:, fo:fo + 1], (L, LANES))
        b_row = bc_t[fo:fo + 1, :]
        ig_row = g_t[io:io + 1, :]
        b_end = bc[end:end + 1, fo:fo + 1]
        m0 = m_s[h:h + 1, :]
        q = q_ref[:, h * dk:(h + 1) * dk]
        k = k_ref[:, h * dk:(h + 1) * dk]
        v = v_ref[:, h * dv:(h + 1) * dv]
        ct0 = ct_s[h]
        n0 = n_s[h]

        logw = jnp.where(mask, lanes(bb, L) + (ig_row - b_row), NEG_INF)
        m_inter = bb + m0
        m_intra = jnp.broadcast_to(jnp.max(logw, axis=1, keepdims=True), (L, LANES))
        m_out = jnp.maximum(m_inter, m_intra)
        p = (_dot_nt(q, k) * jnp.exp(logw - lanes(m_out, L))).astype(BF16)
        s_inter = jnp.exp(m_inter - m_out)
        num = _dot(p, v) + lanes(s_inter, dv) * _dot(q, ct0.astype(BF16))
        den = _dot(p, ones) + s_inter * _dot(q, n0.astype(BF16))
        inv = 1.0 / jnp.maximum(jnp.abs(den), jnp.exp(-m_out))
        o_ref[:, h * dv:(h + 1) * dv] = (num * lanes(inv, dv)).astype(o_ref.dtype)

        a_row = b_end + ig_row - b_row
        m_loc = jnp.max(a_row, axis=1, keepdims=True)
        kw_t = (k.T.astype(F32) * jnp.exp(a_row - m_loc)).astype(BF16)
        m_new = jnp.maximum(b_end + m0, m_loc)
        s_old = jnp.exp(b_end + m0 - m_new)
        s_new = jnp.exp(m_loc - m_new)
        ct_s[h] = lanes(s_old, dv) * ct0 + lanes(s_new, dv) * _dot(kw_t, v)
        n_s[h] = s_old * n0 + s_new * _dot(kw_t, ones)
        m_s[h:h + 1, :] = m_new

    bc_s[...] = bcn_s[...]
    bct_s[...] = bctn_s[...]
    gt_s[...] = gtn_s[...]


def _mlstm_call(qk, vo, gates, gate_b, n_lat, reverse):
    rows = qk.shape[0]
    L = ML_L
    nqk = qk.shape[1] // 2
    nv = vo.shape[1] // 2
    dk = nqk // ML_HEADS
    dv = nv // ML_HEADS
    nlc = n_lat // L
    ncc = (rows - n_lat) // L
    order = functools.partial(_chunk_order, n_lat_chunks=nlc, n_ctx_chunks=ncc, reverse=reverse)
    return pl.pallas_call(
        functools.partial(_mlstm_kernel, reverse=reverse),
        grid=(nlc + ncc,),
        in_specs=[pl.BlockSpec((L, nqk), lambda c: (order(c), 0)),
                  pl.BlockSpec((L, nqk), lambda c: (order(c), 1)),
                  pl.BlockSpec((L, nv), lambda c: (order(c), 0)),
                  pl.BlockSpec((L, LANES), lambda c: (order(c), 0)),
                  pl.BlockSpec((L, LANES), lambda c: (order(jnp.minimum(c + 1, nlc + ncc - 1)), 0)),
                  pl.BlockSpec((1, LANES), lambda c: (0, 0))],
        out_specs=pl.BlockSpec((L, nv), lambda c: (order(c), 0)),
        out_shape=jax.ShapeDtypeStruct((rows, nv), BF16),
        scratch_shapes=[pltpu.VMEM((ML_HEADS, dk, dv), F32),
                        pltpu.VMEM((ML_HEADS, dk, LANES), F32),
                        pltpu.VMEM((ML_HEADS, LANES), F32)]
        + 2 * [pltpu.VMEM((L, LANES), F32), pltpu.VMEM((LANES, L), F32), pltpu.VMEM((LANES, L), F32)],
        compiler_params=_cparams(("arbitrary",)),
    )(qk, qk, vo, gates, gates, gate_b)


def _ssd_kernel(x_ref, bm_ref, cm_ref, dt_ref, dtn_ref, dtb_ref, a_ref, dsk_ref, y_ref,
                st_s, bc_s, bct_s, dtt_s, bcn_s, bctn_s, dttn_s, *, reverse):
    L = x_ref.shape[0]
    nx = x_ref.shape[1]
    c = pl.program_id(0)
    mask = _tri(L, reverse)

    def prefix(dt_blk, bc_d, bct_d, dtt_d):
        dtv = _softplus(dt_blk + dtb_ref[...])
        bc = _dot_exact(mask.astype(F32), dtv * a_ref[...]) * LOG2E
        bc_d[...] = bc
        bct_d[...] = bc.T
        dtt_d[...] = jnp.log2(dtv).T

    @pl.when(c == 0)
    def _():
        st_s[...] = jnp.zeros_like(st_s)
        prefix(dt_ref[...], bc_s, bct_s, dtt_s)

    prefix(dtn_ref[...], bcn_s, bctn_s, dttn_s)

    bc = bc_s[...]
    bc_t = bct_s[...]
    ldt_t = dtt_s[...]
    end = 0 if reverse else L - 1
    nheads = nx // SSM_HEADDIM
    hpg = nheads // SSM_GROUPS
    base = nheads if reverse else 0
    pw = 2 * SSM_HEADDIM
    lo = lax.broadcasted_iota(jnp.int32, (1, pw), 1) < SSM_HEADDIM
    lo_b = jnp.where(lo, 1.0, 0.0).astype(BF16)
    hi_b = jnp.where(lo, 0.0, 1.0).astype(BF16)

    for g in range(SSM_GROUPS):
        bm = bm_ref[:, g * SSM_DSTATE:(g + 1) * SSM_DSTATE]
        cm = cm_ref[:, g * SSM_DSTATE:(g + 1) * SSM_DSTATE]
        cb = _dot_nt(cm, bm)
        bm_t = bm.astype(F32).T
        for pr in range(hpg // 2):
            col = (g * hpg + 2 * pr) * SSM_HEADDIM
            xp = x_ref[:, col:col + pw]
            ms, ebs, bts, decs = [], [], [], []
            for e in range(2):
                co = base + g * hpg + 2 * pr + e
                bb = jnp.broadcast_to(bc[:, co:co + 1], (L, L))
                r_row = bc_t[co:co + 1, :] - ldt_t[co:co + 1, :]
                b_end = bc[end:end + 1, co:co + 1]
                ms.append((cb * jnp.exp2(jnp.where(mask, bb - r_row, NEG_INF))).astype(BF16))
                ebs.append(jnp.exp2(bb[:, 0:pw]))
                bts.append((bm_t * jnp.exp2(b_end - r_row)).astype(BF16))
                decs.append(jnp.exp2(b_end))
            x_lo = xp * lo_b
            x_hi = xp * hi_b
            s0 = st_s[:, col:col + pw]
            y = _dot(ms[0], x_lo) + _dot(ms[1], x_hi)
            y = y + jnp.where(lo, ebs[0], ebs[1]) * _dot(cm, s0.astype(BF16))
            if not reverse:
                y = y + dsk_ref[:, col:col + pw] * xp.astype(F32)
            y_ref[:, col:col + pw] = y.astype(y_ref.dtype)
            st_s[:, col:col + pw] = (s0 * jnp.where(lo, decs[0], decs[1])
                                     + _dot(bts[0], x_lo) + _dot(bts[1], x_hi))

    bc_s[...] = bcn_s[...]
    bct_s[...] = bctn_s[...]
    dtt_s[...] = dttn_s[...]


def _ssd_call(xbc, dt, dt_bias, a_neg, d_skip, n_lat, reverse):
    rows = xbc.shape[0]
    L = SSM_L
    assert L >= 2 * SSM_HEADDIM
    nb = SSM_GROUPS * SSM_DSTATE
    nx = xbc.shape[1] - 2 * nb
    nlc = n_lat // L
    ncc = (rows - n_lat) // L
    order = functools.partial(_chunk_order, n_lat_chunks=nlc, n_ctx_chunks=ncc, reverse=reverse)
    xblk = nx // nb
    nc = nlc + ncc
    return pl.pallas_call(
        functools.partial(_ssd_kernel, reverse=reverse),
        grid=(nc,),
        in_specs=[pl.BlockSpec((L, nx), lambda c: (order(c), 0)),
                  pl.BlockSpec((L, nb), lambda c: (order(c), xblk)),
                  pl.BlockSpec((L, nb), lambda c: (order(c), xblk + 1)),
                  pl.BlockSpec((L, LANES), lambda c: (order(c), 0)),
                  pl.BlockSpec((L, LANES), lambda c: (order(jnp.minimum(c + 1, nc - 1)), 0)),
                  pl.BlockSpec((1, LANES), lambda c: (0, 0)),
                  pl.BlockSpec((1, LANES), lambda c: (0, 0)),
                  pl.BlockSpec((1, nx), lambda c: (0, 0))],
        out_specs=pl.BlockSpec((L, nx), lambda c: (order(c), 0)),
        out_shape=jax.ShapeDtypeStruct((rows, nx), BF16),
        scratch_shapes=[pltpu.VMEM((SSM_DSTATE, nx), F32)]
        + 2 * [pltpu.VMEM((L, LANES), F32), pltpu.VMEM((LANES, L), F32), pltpu.VMEM((LANES, L), F32)],
        compiler_params=_cparams(("arbitrary",)),
    )(xbc, xbc, xbc, dt, dt, dt_bias, a_neg, d_skip)


def _mlgate_kernel(hf_ref, hb_ref, o_ref, nw_ref, a_ref):
    dv = a_ref.shape[1] // ML_HEADS

    def body(rows):
        for h in range(ML_HEADS):
            sl = slice(h * dv, (h + 1) * dv)
            hh = _rms(hf_ref[rows, sl].astype(F32) + hb_ref[rows, sl].astype(F32)) * nw_ref[:, sl]
            a_ref[rows, sl] = (_sigmoid(o_ref[rows, sl].astype(F32)) * hh).astype(BF16)
    _row_loop(a_ref.shape[0], ROW_BLK, body)


def _mlgate_call(hf, hb, vo, norm_w, n_lat):
    nv = hf.shape[1]
    tm = _pick(n_lat, (512, 256))
    row = pl.BlockSpec((tm, nv), lambda i: (i, 0))
    return pl.pallas_call(
        _mlgate_kernel,
        grid=(n_lat // tm,),
        in_specs=[row, row, pl.BlockSpec((tm, nv), lambda i: (i, 1)), pl.BlockSpec((1, nv), lambda i: (0, 0))],
        out_specs=row,
        out_shape=jax.ShapeDtypeStruct((n_lat, nv), BF16),
        compiler_params=_cparams(("arbitrary",)),
    )(hf, hb, vo, norm_w)


def _ssmgate_kernel(yf_ref, yb_ref, z_ref, nw_ref, a_ref):
    gw = a_ref.shape[1] // SSM_GROUPS

    def body(rows):
        for g in range(SSM_GROUPS):
            sl = slice(g * gw, (g + 1) * gw)
            y = (yf_ref[rows, sl].astype(F32) + yb_ref[rows, sl].astype(F32)) * _silu(z_ref[rows, sl].astype(F32))
            a_ref[rows, sl] = (_rms(y) * nw_ref[:, sl]).astype(BF16)
    _row_loop(a_ref.shape[0], ROW_BLK, body)


def _ssmgate_call(yf, yb, z, norm_w, n_lat):
    nx = yf.shape[1]
    tm = _pick(n_lat, (512, 256))
    row = pl.BlockSpec((tm, nx), lambda i: (i, 0))
    return pl.pallas_call(
        _ssmgate_kernel,
        grid=(n_lat // tm,),
        in_specs=[row, row, row, pl.BlockSpec((1, nx), lambda i: (0, 0))],
        out_specs=row,
        out_shape=jax.ShapeDtypeStruct((n_lat, nx), BF16),
        compiler_params=_cparams(("arbitrary",)),
    )(yf, yb, z, norm_w)


def _proj_kernel(a_ref, w_ref, y_ref):
    y_ref[...] = _dot(a_ref[...], w_ref[...].astype(BF16)).astype(y_ref.dtype)


def _proj_call(a, w):
    m, k = a.shape
    n = w.shape[1]
    tm = _pick(m, (1024, 512, 256))
    tn = _pick(n, tuple(c for c in (1024, 512, 256, 128) if k * c * 4 <= W_TILE_MAX_BYTES))
    return pl.pallas_call(
        _proj_kernel,
        grid=(m // tm, n // tn),
        in_specs=[pl.BlockSpec((tm, k), lambda i, j: (i, 0)),
                  pl.BlockSpec((k, tn), lambda i, j: (0, j))],
        out_specs=pl.BlockSpec((tm, tn), lambda i, j: (i, j)),
        out_shape=jax.ShapeDtypeStruct((m, n), BF16),
        compiler_params=_cparams(("arbitrary", "arbitrary")),
    )(a, w)


def _gate_kernel(h_ref, w1_ref, w2_ref, b1_ref, b2_ref, yml_ref, yssm_ref, o_ref):
    h = h_ref[...]
    g1 = _sigmoid(_dot(h, w1_ref[...].astype(BF16)) + b1_ref[...])
    g2 = _sigmoid(_dot(h, w2_ref[...].astype(BF16)) + b2_ref[...])
    o_ref[...] = (g1 * yml_ref[...].astype(F32) + g2 * yssm_ref[...].astype(F32)).astype(o_ref.dtype)


def _gate_call(h, w_gate, b_gate, y_ml, y_ssm, n_lat):
    d = h.shape[1]
    tm = _pick(n_lat, (1024, 512, 256))
    tn = _pick(d, (512, 256, 128))
    nj = d // tn
    tile = pl.BlockSpec((tm, tn), lambda i, j: (i, j))
    return pl.pallas_call(
        _gate_kernel,
        grid=(n_lat // tm, nj),
        in_specs=[pl.BlockSpec((tm, d), lambda i, j: (i, 0)),
                  pl.BlockSpec((d, tn), lambda i, j: (0, j)),
                  pl.BlockSpec((d, tn), lambda i, j: (0, nj + j)),
                  pl.BlockSpec((1, tn), lambda i, j: (0, j)),
                  pl.BlockSpec((1, tn), lambda i, j: (0, nj + j)),
                  tile, tile],
        out_specs=tile,
        out_shape=jax.ShapeDtypeStruct((n_lat, d), BF16),
        compiler_params=_cparams(("arbitrary", "arbitrary")),
    )(h, w_gate, w_gate, b_gate, b_gate, y_ml, y_ssm)


def _out_kernel(a_ref, w_ref, x_ref, g_ref, o_ref):
    o_ref[...] = x_ref[...] + g_ref[...] * _dot(a_ref[...], w_ref[...].astype(BF16))


def _out_call(mix, w_out, x1, gate, n_lat):
    d = mix.shape[1]
    tm = _pick(n_lat, (1024, 512, 256))
    tn = _pick(d, (1024, 512, 256, 128))
    tile = pl.BlockSpec((tm, tn), lambda i, j: (i, j))
    return pl.pallas_call(
        _out_kernel,
        grid=(n_lat // tm, d // tn),
        in_specs=[pl.BlockSpec((tm, d), lambda i, j: (i, 0)),
                  pl.BlockSpec((d, tn), lambda i, j: (0, j)),
                  tile,
                  pl.BlockSpec((1, tn), lambda i, j: (0, j))],
        out_specs=tile,
        out_shape=jax.ShapeDtypeStruct((n_lat, d), F32),
        compiler_params=_cparams(("arbitrary", "arbitrary")),
    )(mix, w_out, x1, gate)


def _pad_rows(a, rows):
    return jnp.concatenate([a, jnp.zeros((rows - a.shape[0],) + a.shape[1:], a.dtype)], axis=0)


def _pad_cols(a, cols):
    return jnp.concatenate([a, jnp.zeros(a.shape[:-1] + (cols - a.shape[-1],), a.dtype)], axis=-1)


def kernel(x, c, ctx, c_ctx, w_ada, b_ada, norm_w, ffn_w_gate, ffn_w_up, ffn_w_down, w_in, ml_conv_w, ml_conv_b, ml_gate_b, ml_norm_w, w_proj_ml, ssm_conv_w, ssm_conv_b, ssm_dt_bias, ssm_a_log, ssm_d, ssm_norm_w, w_proj_ssm, w_gate, b_gate, w_out, final_norm_w):
    depth = w_ada.shape[0]
    assert depth == 1 and x.shape[0] == 1, "single layer, single batch element"
    t, d = x.shape[1], x.shape[2]
    n_tot = t + ctx.shape[1]
    rows_cm = t // GRID_W
    ml_qk = ml_conv_w.shape[2] // 2
    ml_v = ml_norm_w.shape[1]
    n_ml = 2 * ml_qk + 2 * ml_v
    ml_cols = n_ml + 4 * ML_HEADS
    ssm_inner = ssm_norm_w.shape[1]
    ssm_heads = ssm_d.shape[1]
    ssm_xbc = ssm_conv_w.shape[2]

    cc = _pad_rows(jnp.concatenate([c, c_ctx[None, :]], axis=0), 8)
    mod = _mod_call(cc, w_ada[0], b_ada[0])
    ml, mc = mod[0].reshape(N_MOD, d), mod[1].reshape(N_MOD, d)
    nw = norm_w[0]
    zero = jnp.zeros_like(nw[0])
    vec1 = _pad_rows(jnp.stack([nw[0], ml[0], ml[1], ml[2], mc[0], mc[1], mc[2],
                                nw[1], ml[3], ml[4], zero, mc[3], mc[4]]), 16)
    vec2 = _pad_rows(jnp.stack([nw[2], ml[6], ml[7], ml[8], zero, zero, zero, final_norm_w]), 16)
    wg, wu, wd = ffn_w_gate, ffn_w_up, ffn_w_down.astype(BF16)

    xall = jnp.concatenate([x[0], ctx[0]], axis=0)
    x1, h = _ffn_call(xall, vec1, wg, wu, wd, 0, t, final=False)

    h_cm = jnp.concatenate(
        [h[:t].reshape(rows_cm, GRID_W, d).transpose(1, 0, 2).reshape(t, d), h[t:]], axis=0)
    h_cm_units = _colmajor_units(h, t, rows_cm)
    wt = w_in[0].T
    n_ssm = ssm_inner + ssm_xbc
    ml_cw = _pad_rows(ml_conv_w[0], 8)
    ml_post = jnp.concatenate([jnp.full((1, ml_qk), (ml_qk // ML_HEADS) ** -0.5, F32),
                               jnp.ones((1, ml_qk), F32)], axis=1)
    qk = _mmconv_call(h_cm_units, wt, 0, ml_cw, ml_conv_b[0][None, :], ml_post, t)
    vo = _mm_call(h_cm, wt, 2 * ml_qk, 2 * ml_v, n_tot, BF16)
    gates = _mm_call(h_cm, wt, n_ml, LANES, n_tot, F32)
    z = _mm_call(h, wt, ml_cols, ssm_inner, t, BF16)
    xbc = _mmconv_call(_unit_permute(h), wt, ml_cols + ssm_inner, _pad_rows(ssm_conv_w[0], 8),
                       ssm_conv_b[0][None, :], None, t)
    dt = _mm_call(h, wt, ml_cols + n_ssm, 2 * ssm_heads, n_tot, F32)

    gate_b = _pad_cols(ml_gate_b[0].reshape(1, 4 * ML_HEADS), LANES)
    h_f = _mlstm_call(qk, vo, gates, gate_b, t, reverse=False)
    h_b = _mlstm_call(qk, vo, gates, gate_b, t, reverse=True)
    y_ml = _proj_call(_mlgate_call(h_f, h_b, vo, ml_norm_w[0][None, :], t), w_proj_ml[0])
    y_ml = y_ml.reshape(GRID_W, rows_cm, d).transpose(1, 0, 2).reshape(t, d)

    dt_bias = ssm_dt_bias[0].reshape(1, 2 * ssm_heads)
    a_neg = -jnp.exp(ssm_a_log[0].astype(F32)).reshape(1, 2 * ssm_heads)
    d_skip = jnp.repeat(ssm_d[0], SSM_HEADDIM)[None, :]
    y_f = _ssd_call(xbc, dt, dt_bias, a_neg, d_skip, t, reverse=False)
    y_b = _ssd_call(xbc, dt, dt_bias, a_neg, d_skip, t, reverse=True)
    y_ssm = _proj_call(_ssmgate_call(y_f, y_b, z, ssm_norm_w[0][None, :], t), w_proj_ssm[0])

    mix = _gate_call(h, w_gate[0], b_gate[0][None, :], y_ml, y_ssm, t)
    x2 = _out_call(mix, w_out[0], x1, ml[5][None, :], t)
    out = _ffn_call(x2, vec2, wg, wu, wd, 1, t, final=True)
    return out[None]
```

```python
import functools

import jax
import jax.numpy as jnp
from jax import lax
from jax.experimental import pallas as pl
from jax.experimental.pallas import tpu as pltpu

F32 = jnp.float32
BF16 = jnp.bfloat16

GRID_W = 64
N_MOD = 9
EPS = 1e-6
CONV_W = 5
CONV_PAD = CONV_W // 2
HALO = 16
CONV_UNIT = 256
ROW_BLK = 16
FFN_ROW_BLK = 32
FFN_PREFETCH_MAX_ROWS = 768
ML_HEADS = 8
ML_L = 256
SSM_HEADDIM = 64
SSM_GROUPS = 8
SSM_DSTATE = 128
SSM_L = 128
LANES = 128
VMEM_LIMIT_BYTES = 56 * 1024 * 1024
NEG_INF = float("-inf")
LOG2E = 1.4426950408889634


def _cparams(sem):
    return pltpu.CompilerParams(dimension_semantics=sem, vmem_limit_bytes=VMEM_LIMIT_BYTES)


def _pick(n, cands):
    for c in cands:
        if n % c == 0:
            return c
    raise ValueError(f"no tile for {n} in {cands}")


def _silu(x):
    return x * (1.0 / (1.0 + jnp.exp(-x)))


def _sigmoid(x):
    return 1.0 / (1.0 + jnp.exp(-x))


def _log_sigmoid(x):
    return jnp.minimum(x, 0.0) - jnp.log1p(jnp.exp(-jnp.abs(x)))


def _softplus(x):
    return jnp.maximum(x, 0.0) + jnp.log1p(jnp.exp(-jnp.abs(x)))


def _rms(x):
    return x * lax.rsqrt(jnp.mean(jnp.square(x), axis=-1, keepdims=True) + EPS)


def _dot(a, b):
    return jnp.dot(a, b, preferred_element_type=F32)


def _dot_nt(a, b):
    return lax.dot_general(a, b, (((1,), (1,)), ((), ())), preferred_element_type=F32)


def _dot_exact(a, b):
    return jnp.dot(a, b, preferred_element_type=F32, precision=lax.Precision.HIGHEST)


def _row_loop(n_rows, blk, body):
    def step(r, carry):
        body(pl.ds(pl.multiple_of(r * blk, blk), blk))
        return carry
    lax.fori_loop(0, n_rows // blk, step, 0, unroll=2)


def _mod_kernel(c_ref, w_ref, b_ref, o_ref):
    a = _silu(c_ref[...]).astype(BF16)
    o_ref[...] = _dot(a, w_ref[...].astype(BF16)) + b_ref[...]


def _mod_call(cc, w_ada, b_ada):
    k, n = w_ada.shape
    tn = _pick(n, (1024, 512, 256, 128))
    return pl.pallas_call(
        _mod_kernel,
        grid=(n // tn,),
        in_specs=[pl.BlockSpec((8, k), lambda j: (0, 0)),
                  pl.BlockSpec((k, tn), lambda j: (0, j)),
                  pl.BlockSpec((1, tn), lambda j: (0, j))],
        out_specs=pl.BlockSpec((8, tn), lambda j: (0, j)),
        out_shape=jax.ShapeDtypeStruct((8, n), F32),
        compiler_params=_cparams(("arbitrary",)),
    )(cc, w_ada, b_ada.reshape(1, n))


def _ffn_kernel(x_ref, vec_ref, wg_ref, wu_ref, wd_ref, *rest, n_lat, final):
    if final:
        o_ref, h_s, acc_s = rest
    else:
        x1_ref, h2_ref, h_s, acc_s = rest
    i = pl.program_id(0)
    f = pl.program_id(1)
    tm = x_ref.shape[0]

    def vec(row, rows):
        off = jnp.where(i * tm + rows.start >= n_lat, 3, 0)
        return vec_ref[pl.ds(row + off, 1), :]

    @pl.when(f == 0)
    def _():
        def body(rows):
            h = _rms(x_ref[rows, :]) * (vec_ref[0:1, :] * (1.0 + vec(2, rows))) + vec(1, rows)
            h_s[rows, :] = h.astype(BF16)
            acc_s[rows, :] = jnp.zeros((FFN_ROW_BLK, acc_s.shape[1]), F32)
        _row_loop(tm, FFN_ROW_BLK, body)

    h = h_s[...]
    a = _silu(_dot(h, wg_ref[...].astype(BF16))) * _dot(h, wu_ref[...].astype(BF16))
    acc_s[...] += _dot(a.astype(BF16), wd_ref[...])

    @pl.when(f == pl.num_programs(1) - 1)
    def _():
        def body(rows):
            x1 = x_ref[rows, :] + (0.5 * vec(3, rows)) * acc_s[rows, :]
            if final:
                o_ref[rows, :] = _rms(x1) * vec_ref[7:8, :]
            else:
                x1_ref[rows, :] = x1
                h2 = _rms(x1) * (vec_ref[7:8, :] * (1.0 + vec(9, rows))) + vec(8, rows)
                h2_ref[rows, :] = h2.astype(BF16)
        _row_loop(tm, FFN_ROW_BLK, body)


def _ffn_call(x, vecs, wg, wu, wd, which, n_lat, final):
    m, d = x.shape
    dff = wg.shape[3]
    tm = _pick(m, (1024, 768, 512, 256))
    tf = _pick(dff, (512, 256, 128))
    row = pl.BlockSpec((tm, d), lambda i, f: (i, 0), pipeline_mode=pl.Buffered(1))
    x_row = pl.BlockSpec((tm, d), lambda i, f: (i, 0)) if tm <= FFN_PREFETCH_MAX_ROWS else row
    if final:
        out_shape = jax.ShapeDtypeStruct((m, d), F32)
        out_specs = row
    else:
        out_shape = (jax.ShapeDtypeStruct((m, d), F32), jax.ShapeDtypeStruct((m, d), BF16))
        out_specs = (row, row)
    return pl.pallas_call(
        functools.partial(_ffn_kernel, n_lat=n_lat, final=final),
        grid=(m // tm, dff // tf),
        in_specs=[x_row,
                  pl.BlockSpec(vecs.shape, lambda i, f: (0, 0)),
                  pl.BlockSpec((None, None, d, tf), lambda i, f: (0, which, 0, f)),
                  pl.BlockSpec((None, None, d, tf), lambda i, f: (0, which, 0, f)),
                  pl.BlockSpec((None, None, tf, d), lambda i, f: (0, which, f, 0))],
        out_specs=out_specs,
        out_shape=out_shape,
        scratch_shapes=[pltpu.VMEM((tm, d), BF16), pltpu.VMEM((tm, d), F32)],
        compiler_params=_cparams(("arbitrary", "arbitrary")),
    )(x, vecs, wg, wu, wd)


def _mm_kernel(a_ref, w_ref, *rest):
    if len(rest) == 3:
        w2_ref, o_ref, o2_ref = rest

        @pl.when(pl.program_id(1) == 0)
        def _():
            o2_ref[...] = _dot_nt(a_ref[...], w2_ref[...].astype(BF16))
    else:
        (o_ref,) = rest
    o_ref[...] = _dot_nt(a_ref[...], w_ref[...].astype(BF16)).astype(o_ref.dtype)


def _mm_call(a, wt, row0, n, m, out_dtype, side_row0=None):
    k = a.shape[1]
    tm = _pick(m, (1408, 1024, 768, 512, 256))
    tn = _pick(n, (1024, 512, 256, 128))
    assert row0 % 8 == 0
    wspec = lambda rows, idx: pl.BlockSpec((pl.Element(rows), pl.Element(k)), idx)
    in_specs = [pl.BlockSpec((tm, k), lambda i, j: (i, 0)),
                wspec(tn, lambda i, j: (pl.multiple_of(row0 + j * tn, 8), 0))]
    out_specs = pl.BlockSpec((tm, tn), lambda i, j: (i, j))
    out_shape = jax.ShapeDtypeStruct((m, n), out_dtype)
    args = (a, wt)
    if side_row0 is not None:
        assert side_row0 % 8 == 0
        in_specs.append(wspec(LANES, lambda i, j: (side_row0, 0)))
        out_specs = (out_specs, pl.BlockSpec((tm, LANES), lambda i, j: (i, 0)))
        out_shape = (out_shape, jax.ShapeDtypeStruct((m, LANES), F32))
        args = (a, wt, wt)
    return pl.pallas_call(
        _mm_kernel,
        grid=(m // tm, n // tn),
        in_specs=in_specs,
        out_specs=out_specs,
        out_shape=out_shape,
        compiler_params=_cparams(("arbitrary", "arbitrary")),
    )(*args)


def _mmconv_kernel(a_ref, ap_ref, an_ref, w_ref, cw_ref, cb_ref, *rest, n_lat, n_tot, scaled):
    if scaled:
        post_ref, o_ref, xs, ys, a_s = rest
    else:
        o_ref, xs, ys, a_s = rest
    i = pl.program_id(0)
    tm = a_ref.shape[0]
    tn = o_ref.shape[1]
    units = tm // CONV_UNIT
    ug = CONV_UNIT // 8
    span = CONV_UNIT + 4 * 8
    seq_units = n_lat // CONV_UNIT
    last_unit = n_tot // CONV_UNIT - 1

    @pl.when(pl.program_id(1) == 0)
    def _():
        a_s[0:HALO, :] = ap_ref[...]
        a_s[HALO:HALO + tm, :] = a_ref[...]
        a_s[HALO + tm:2 * HALO + tm, :] = an_ref[...]

    w = w_ref[...].astype(BF16)
    p_all = _dot_nt(a_s[...], w)
    p_prev = p_all[0:HALO]
    p = p_all[HALO:HALO + tm]
    p_next = p_all[HALO + tm:2 * HALO + tm]
    sub = lax.broadcasted_iota(jnp.int32, (8, 1), 0)

    for u in range(units):
        g = i * units + u
        keep_prev = jnp.where((g == 0) | (g == seq_units), 0.0, 1.0)
        keep_next = jnp.where((g == last_unit) | (g == seq_units - 1), 0.0, 1.0)
        r0 = u * CONV_UNIT
        base = u * span
        xs[base + 16:base + 16 + CONV_UNIT, :] = p[r0:r0 + CONV_UNIT]
        before = p[r0 - 16:r0] if u > 0 else p_prev
        after = p[r0 + CONV_UNIT:r0 + CONV_UNIT + 16] if u < units - 1 else p_next
        for d in range(2):
            own = pltpu.roll(p[r0 + (ug - 2 + d) * 8:r0 + (ug - 1 + d) * 8], 1, 0)
            other = pltpu.roll(before[d * 8:(d + 1) * 8], 1, 0) * keep_prev
            xs[base + d * 8:base + (d + 1) * 8, :] = jnp.where(sub == 0, other, own)
            own = pltpu.roll(p[r0 + d * 8:r0 + (d + 1) * 8], 7, 0)
            other = pltpu.roll(after[d * 8:(d + 1) * 8], 7, 0) * keep_next
            xs[base + 16 + CONV_UNIT + d * 8:base + 16 + CONV_UNIT + (d + 1) * 8, :] = (
                jnp.where(sub == 7, other, own))

        acc = cb_ref[...]
        for j in range(CONV_W):
            acc = acc + cw_ref[j:j + 1, :] * xs[base + 8 * j:base + 8 * j + CONV_UNIT, :]
        y = _silu(acc)
        if scaled:
            y = y * post_ref[...]
        for cblk in range(tn // LANES):
            lanes = slice(cblk * LANES, (cblk + 1) * LANES)
            ys[cblk, r0:r0 + CONV_UNIT, :] = y[:, lanes]
            for b8 in range(8):
                o_ref[r0 + b8 * ug:r0 + (b8 + 1) * ug, lanes] = (
                    ys[cblk, pl.ds(r0 + b8, ug, stride=8), :].astype(o_ref.dtype))


def _mmconv_call(a, w, col0, conv_w, conv_b, post, n_lat):
    m, k = a.shape
    n = conv_w.shape[1]
    tm = _pick(m, (768, 512, 256))
    tn = _pick(n, (1024, 512, 256, 128))
    assert col0 % 8 == 0 and tm % CONV_UNIT == 0 and n_lat % CONV_UNIT == 0 and m % CONV_UNIT == 0
    hb = tm // HALO
    last_halo = m // HALO - 1
    vec = pl.BlockSpec((1, tn), lambda i, j: (0, j))
    scaled = post is not None
    return pl.pallas_call(
        functools.partial(_mmconv_kernel, n_lat=n_lat, n_tot=m, scaled=scaled),
        grid=(m // tm, n // tn),
        in_specs=[pl.BlockSpec((tm, k), lambda i, j: (i, 0)),
                  pl.BlockSpec((HALO, k), lambda i, j: (jnp.maximum(i * hb - 1, 0), 0)),
                  pl.BlockSpec((HALO, k), lambda i, j: (jnp.minimum((i + 1) * hb, last_halo), 0)),
                  pl.BlockSpec((pl.Element(tn), pl.Element(k)),
                               lambda i, j: (pl.multiple_of(col0 + j * tn, 8), 0)),
                  pl.BlockSpec((8, tn), lambda i, j: (0, j)),
                  vec] + ([vec] if scaled else []),
        out_specs=pl.BlockSpec((tm, tn), lambda i, j: (i, j)),
        out_shape=jax.ShapeDtypeStruct((m, n), BF16),
        scratch_shapes=[pltpu.VMEM(((tm // CONV_UNIT) * (CONV_UNIT + 32), tn), F32),
                        pltpu.VMEM((tn // LANES, tm, LANES), F32),
                        pltpu.VMEM((tm + 2 * HALO, k), BF16)],
        compiler_params=_cparams(("arbitrary", "arbitrary")),
    )(a, a, a, w, conv_w, conv_b, *([post] if scaled else []))


def _unit_permute(x):
    rows, d = x.shape
    ug = CONV_UNIT // 8
    return x.reshape(rows // CONV_UNIT, 8, ug, d).transpose(0, 2, 1, 3).reshape(rows, d)


def _chunk_order(c, n_lat_chunks, n_ctx_chunks, reverse):
    if reverse:
        return n_lat_chunks + n_ctx_chunks - 1 - c
    return jnp.where(c < n_ctx_chunks, n_lat_chunks + c, c - n_ctx_chunks)


def _tri(chunk, reverse):
    r = lax.broadcasted_iota(jnp.int32, (chunk, chunk), 0)
    c = lax.broadcasted_iota(jnp.int32, (chunk, chunk), 1)
    return (c >= r) if reverse else (c <= r)


def _mlstm_kernel(q_ref, k_ref, v_ref, g_ref, gn_ref, gb_ref, o_ref, ct_s, n_s, m_s, bc_s, bct_s, gt_s,
                  bcn_s, bctn_s, gtn_s, *, reverse):
    L = q_ref.shape[0]
    dk = q_ref.shape[1] // ML_HEADS
    dv = v_ref.shape[1] // ML_HEADS
    c = pl.program_id(0)

    mask = _tri(L, reverse)

    def prefix(g_blk, bc_d, bct_d, gt_d):
        g = g_blk + gb_ref[...]
        bc = _dot_exact(mask.astype(F32), _log_sigmoid(g))
        bc_d[...] = bc
        bct_d[...] = bc.T
        gt_d[...] = g.T

    @pl.when(c == 0)
    def _():
        ct_s[...] = jnp.zeros_like(ct_s)
        n_s[...] = jnp.zeros_like(n_s)
        m_s[...] = jnp.zeros_like(m_s)
        prefix(g_ref[...], bc_s, bct_s, gt_s)

    prefix(gn_ref[...], bcn_s, bctn_s, gtn_s)

    def lanes(x, n):
        return jnp.concatenate([x] * (n // LANES), axis=1)

    bc = bc_s[...]
    bc_t = bct_s[...]
    g_t = gt_s[...]
    end = 0 if reverse else L - 1
    base = 2 * ML_HEADS if reverse else 0
    ones = jnp.ones((L, LANES), BF16)

    for h in range(ML_HEADS):
        io = base + h
        fo = base + ML_HEADS + h
        bb = jnp.broadcast_to(bc[:, fo:fo + 1], (L, LANES))
        b_row = bc_t[fo:fo + 1, :]
        ig_row = g_t[io:io + 1, :]
        b_end = bc[end:end + 1, fo:fo + 1]
        m0 = m_s[h:h + 1, :]
        q = q_ref[:, h * dk:(h + 1) * dk]
        k = k_ref[:, h * dk:(h + 1) * dk]
        v = v_ref[:, h * dv:(h + 1) * dv]
        ct0 = ct_s[h]
        n0 = n_s[h]

        logw = jnp.where(mask, lanes(bb, L) + (ig_row - b_row), NEG_INF)
        m_inter = bb + m0
        m_intra = jnp.broadcast_to(jnp.max(logw, axis=1, keepdims=True), (L, LANES))
        m_out = jnp.maximum(m_inter, m_intra)
        p = (_dot_nt(q, k) * jnp.exp(logw - lanes(m_out, L))).astype(BF16)
        s_inter = jnp.exp(m_inter - m_out)
        num = _dot(p, v) + lanes(s_inter, dv) * _dot(q, ct0.astype(BF16))
        den = _dot(p, ones) + s_inter * _dot(q, n0.astype(BF16))
        inv = 1.0 / jnp.maximum(jnp.abs(den), jnp.exp(-m_out))
        o_ref[:, h * dv:(h + 1) * dv] = (num * lanes(inv, dv)).astype(o_ref.dtype)

        a_row = b_end + ig_row - b_row
        m_loc = jnp.max(a_row, axis=1, keepdims=True)
        kw_t = (k.T.astype(F32) * jnp.exp(a_row - m_loc)).astype(BF16)
        m_new = jnp.maximum(b_end + m0, m_loc)
        s_old = jnp.exp(b_end + m0 - m_new)
        s_new = jnp.exp(m_loc - m_new)
        ct_s[h] = lanes(s_old, dv) * ct0 + lanes(s_new, dv) * _dot(kw_t, v)
        n_s[h] = s_old * n0 + s_new * _dot(kw_t, ones)
        m_s[h:h + 1, :] = m_new

    bc_s[...] = bcn_s[...]
    bct_s[...] = bctn_s[...]
    gt_s[...] = gtn_s[...]


def _mlstm_call(qk, vo, gates, gate_b, n_lat, reverse):
    rows = qk.shape[0]
    L = ML_L
    nqk = qk.shape[1] // 2
    nv = vo.shape[1] // 2
    dk = nqk // ML_HEADS
    dv = nv // ML_HEADS
    nlc = n_lat // L
    ncc = (rows - n_lat) // L
    order = functools.partial(_chunk_order, n_lat_chunks=nlc, n_ctx_chunks=ncc, reverse=reverse)
    return pl.pallas_call(
        functools.partial(_mlstm_kernel, reverse=reverse),
        grid=(nlc + ncc,),
        in_specs=[pl.BlockSpec((L, nqk), lambda c: (order(c), 0)),
                  pl.BlockSpec((L, nqk), lambda c: (order(c), 1)),
                  pl.BlockSpec((L, nv), lambda c: (order(c), 0)),
                  pl.BlockSpec((L, LANES), lambda c: (order(c), 0)),
                  pl.BlockSpec((L, LANES), lambda c: (order(jnp.minimum(c + 1, nlc + ncc - 1)), 0)),
                  pl.BlockSpec((1, LANES), lambda c: (0, 0))],
        out_specs=pl.BlockSpec((L, nv), lambda c: (order(c), 0)),
        out_shape=jax.ShapeDtypeStruct((rows, nv), BF16),
        scratch_shapes=[pltpu.VMEM((ML_HEADS, dk, dv), F32),
                        pltpu.VMEM((ML_HEADS, dk, LANES), F32),
                        pltpu.VMEM((ML_HEADS, LANES), F32)]
        + 2 * [pltpu.VMEM((L, LANES), F32), pltpu.VMEM((LANES, L), F32), pltpu.VMEM((LANES, L), F32)],
        compiler_params=_cparams(("arbitrary",)),
    )(qk, qk, vo, gates, gates, gate_b)


def _ssd_kernel(x_ref, bm_ref, cm_ref, dt_ref, dtn_ref, dtb_ref, a_ref, dsk_ref, y_ref,
                st_s, bc_s, bct_s, dtt_s, bcn_s, bctn_s, dttn_s, *, reverse):
    L = x_ref.shape[0]
    nx = x_ref.shape[1]
    c = pl.program_id(0)
    mask = _tri(L, reverse)

    def prefix(dt_blk, bc_d, bct_d, dtt_d):
        dtv = _softplus(dt_blk + dtb_ref[...])
        bc = _dot_exact(mask.astype(F32), dtv * a_ref[...]) * LOG2E
        bc_d[...] = bc
        bct_d[...] = bc.T
        dtt_d[...] = jnp.log2(dtv).T

    @pl.when(c == 0)
    def _():
        st_s[...] = jnp.zeros_like(st_s)
        prefix(dt_ref[...], bc_s, bct_s, dtt_s)

    prefix(dtn_ref[...], bcn_s, bctn_s, dttn_s)

    bc = bc_s[...]
    bc_t = bct_s[...]
    ldt_t = dtt_s[...]
    end = 0 if reverse else L - 1
    nheads = nx // SSM_HEADDIM
    hpg = nheads // SSM_GROUPS
    base = nheads if reverse else 0
    pw = 2 * SSM_HEADDIM
    lo = lax.broadcasted_iota(jnp.int32, (1, pw), 1) < SSM_HEADDIM
    lo_b = jnp.where(lo, 1.0, 0.0).astype(BF16)
    hi_b = jnp.where(lo, 0.0, 1.0).astype(BF16)

    for g in range(SSM_GROUPS):
        bm = bm_ref[:, g * SSM_DSTATE:(g + 1) * SSM_DSTATE]
        cm = cm_ref[:, g * SSM_DSTATE:(g + 1) * SSM_DSTATE]
        cb = _dot_nt(cm, bm)
        bm_t = bm.astype(F32).T
        for pr in range(hpg // 2):
            col = (g * hpg + 2 * pr) * SSM_HEADDIM
            xp = x_ref[:, col:col + pw]
            ms, ebs, bts, decs = [], [], [], []
            for e in range(2):
                co = base + g * hpg + 2 * pr + e
                bb = jnp.broadcast_to(bc[:, co:co + 1], (L, L))
                r_row = bc_t[co:co + 1, :] - ldt_t[co:co + 1, :]
                b_end = bc[end:end + 1, co:co + 1]
                ms.append((cb * jnp.exp2(jnp.where(mask, bb - r_row, NEG_INF))).astype(BF16))
                ebs.append(jnp.exp2(bb[:, 0:pw]))
                bts.append((bm_t * jnp.exp2(b_end - r_row)).astype(BF16))
                decs.append(jnp.exp2(b_end))
            x_lo = xp * lo_b
            x_hi = xp * hi_b
            s0 = st_s[:, col:col + pw]
            y = _dot(ms[0], x_lo) + _dot(ms[1], x_hi)
            y = y + jnp.where(lo, ebs[0], ebs[1]) * _dot(cm, s0.astype(BF16))
            if not reverse:
                y = y + dsk_ref[:, col:col + pw] * xp.astype(F32)
            y_ref[:, col:col + pw] = y.astype(y_ref.dtype)
            st_s[:, col:col + pw] = (s0 * jnp.where(lo, decs[0], decs[1])
                                     + _dot(bts[0], x_lo) + _dot(bts[1], x_hi))

    bc_s[...] = bcn_s[...]
    bct_s[...] = bctn_s[...]
    dtt_s[...] = dttn_s[...]


def _ssd_call(xbc, dt, dt_bias, a_neg, d_skip, n_lat, reverse):
    rows = xbc.shape[0]
    L = SSM_L
    assert L >= 2 * SSM_HEADDIM
    nb = SSM_GROUPS * SSM_DSTATE
    nx = xbc.shape[1] - 2 * nb
    nlc = n_lat // L
    ncc = (rows - n_lat) // L
    order = functools.partial(_chunk_order, n_lat_chunks=nlc, n_ctx_chunks=ncc, reverse=reverse)
    xblk = nx // nb
    nc = nlc + ncc
    return pl.pallas_call(
        functools.partial(_ssd_kernel, reverse=reverse),
        grid=(nc,),
        in_specs=[pl.BlockSpec((L, nx), lambda c: (order(c), 0)),
                  pl.BlockSpec((L, nb), lambda c: (order(c), xblk)),
                  pl.BlockSpec((L, nb), lambda c: (order(c), xblk + 1)),
                  pl.BlockSpec((L, LANES), lambda c: (order(c), 0)),
                  pl.BlockSpec((L, LANES), lambda c: (order(jnp.minimum(c + 1, nc - 1)), 0)),
                  pl.BlockSpec((1, LANES), lambda c: (0, 0)),
                  pl.BlockSpec((1, LANES), lambda c: (0, 0)),
                  pl.BlockSpec((1, nx), lambda c: (0, 0))],
        out_specs=pl.BlockSpec((L, nx), lambda c: (order(c), 0)),
        out_shape=jax.ShapeDtypeStruct((rows, nx), BF16),
        scratch_shapes=[pltpu.VMEM((SSM_DSTATE, nx), F32)]
        + 2 * [pltpu.VMEM((L, LANES), F32), pltpu.VMEM((LANES, L), F32), pltpu.VMEM((LANES, L), F32)],
        compiler_params=_cparams(("arbitrary",)),
    )(xbc, xbc, xbc, dt, dt, dt_bias, a_neg, d_skip)


def _mlgate_kernel(hf_ref, hb_ref, o_ref, nw_ref, a_ref):
    dv = a_ref.shape[1] // ML_HEADS

    def body(rows):
        for h in range(ML_HEADS):
            sl = slice(h * dv, (h + 1) * dv)
            hh = _rms(hf_ref[rows, sl].astype(F32) + hb_ref[rows, sl].astype(F32)) * nw_ref[:, sl]
            a_ref[rows, sl] = (_sigmoid(o_ref[rows, sl].astype(F32)) * hh).astype(BF16)
    _row_loop(a_ref.shape[0], ROW_BLK, body)


def _mlgate_call(hf, hb, vo, norm_w, n_lat):
    nv = hf.shape[1]
    tm = _pick(n_lat, (512, 256))
    row = pl.BlockSpec((tm, nv), lambda i: (i, 0))
    return pl.pallas_call(
        _mlgate_kernel,
        grid=(n_lat // tm,),
        in_specs=[row, row, pl.BlockSpec((tm, nv), lambda i: (i, 1)), pl.BlockSpec((1, nv), lambda i: (0, 0))],
        out_specs=row,
        out_shape=jax.ShapeDtypeStruct((n_lat, nv), BF16),
        compiler_params=_cparams(("arbitrary",)),
    )(hf, hb, vo, norm_w)


def _ssmgate_kernel(yf_ref, yb_ref, z_ref, nw_ref, a_ref):
    gw = a_ref.shape[1] // SSM_GROUPS

    def body(rows):
        for g in range(SSM_GROUPS):
            sl = slice(g * gw, (g + 1) * gw)
            y = (yf_ref[rows, sl].astype(F32) + yb_ref[rows, sl].astype(F32)) * _silu(z_ref[rows, sl].astype(F32))
            a_ref[rows, sl] = (_rms(y) * nw_ref[:, sl]).astype(BF16)
    _row_loop(a_ref.shape[0], ROW_BLK, body)


def _ssmgate_call(yf, yb, z, norm_w, n_lat):
    nx = yf.shape[1]
    tm = _pick(n_lat, (512, 256))
    row = pl.BlockSpec((tm, nx), lambda i: (i, 0))
    return pl.pallas_call(
        _ssmgate_kernel,
        grid=(n_lat // tm,),
        in_specs=[row, row, row, pl.BlockSpec((1, nx), lambda i: (0, 0))],
        out_specs=row,
        out_shape=jax.ShapeDtypeStruct((n_lat, nx), BF16),
        compiler_params=_cparams(("arbitrary",)),
    )(yf, yb, z, norm_w)


def _proj_kernel(a_ref, w_ref, y_ref):
    y_ref[...] = _dot(a_ref[...], w_ref[...]).astype(y_ref.dtype)


def _proj_call(a, w):
    m, k = a.shape
    n = w.shape[1]
    tm = _pick(m, (1024, 512, 256))
    tn = _pick(n, (1024, 512, 256, 128))
    return pl.pallas_call(
        _proj_kernel,
        grid=(m // tm, n // tn),
        in_specs=[pl.BlockSpec((tm, k), lambda i, j: (i, 0)),
                  pl.BlockSpec((k, tn), lambda i, j: (0, j))],
        out_specs=pl.BlockSpec((tm, tn), lambda i, j: (i, j)),
        out_shape=jax.ShapeDtypeStruct((m, n), BF16),
        compiler_params=_cparams(("arbitrary", "arbitrary")),
    )(a, w)


def _gate_kernel(h_ref, w1_ref, w2_ref, b1_ref, b2_ref, yml_ref, yssm_ref, o_ref):
    h = h_ref[...]
    g1 = _sigmoid(_dot(h, w1_ref[...]) + b1_ref[...])
    g2 = _sigmoid(_dot(h, w2_ref[...]) + b2_ref[...])
    o_ref[...] = (g1 * yml_ref[...].astype(F32) + g2 * yssm_ref[...].astype(F32)).astype(o_ref.dtype)


def _gate_call(h, w_gate, b_gate, y_ml, y_ssm, n_lat):
    d = h.shape[1]
    tm = _pick(n_lat, (1024, 512, 256))
    tn = _pick(d, (1024, 512, 256, 128))
    nj = d // tn
    tile = pl.BlockSpec((tm, tn), lambda i, j: (i, j))
    return pl.pallas_call(
        _gate_kernel,
        grid=(n_lat // tm, nj),
        in_specs=[pl.BlockSpec((tm, d), lambda i, j: (i, 0)),
                  pl.BlockSpec((d, tn), lambda i, j: (0, j)),
                  pl.BlockSpec((d, tn), lambda i, j: (0, nj + j)),
                  pl.BlockSpec((1, tn), lambda i, j: (0, j)),
                  pl.BlockSpec((1, tn), lambda i, j: (0, nj + j)),
                  tile, tile],
        out_specs=tile,
        out_shape=jax.ShapeDtypeStruct((n_lat, d), BF16),
        compiler_params=_cparams(("arbitrary", "arbitrary")),
    )(h, w_gate, w_gate, b_gate, b_gate, y_ml, y_ssm)


def _out_kernel(a_ref, w_ref, x_ref, g_ref, o_ref):
    o_ref[...] = x_ref[...] + g_ref[...] * _dot(a_ref[...], w_ref[...])


def _out_call(mix, w_out, x1, gate, n_lat):
    d = mix.shape[1]
    tm = _pick(n_lat, (1024, 512, 256))
    tn = _pick(d, (1024, 512, 256, 128))
    tile = pl.BlockSpec((tm, tn), lambda i, j: (i, j))
    return pl.pallas_call(
        _out_kernel,
        grid=(n_lat // tm, d // tn),
        in_specs=[pl.BlockSpec((tm, d), lambda i, j: (i, 0)),
                  pl.BlockSpec((d, tn), lambda i, j: (0, j)),
                  tile,
                  pl.BlockSpec((1, tn), lambda i, j: (0, j))],
        out_specs=tile,
        out_shape=jax.ShapeDtypeStruct((n_lat, d), F32),
        compiler_params=_cparams(("arbitrary", "arbitrary")),
    )(mix, w_out, x1, gate)


def _pad_rows(a, rows):
    return jnp.concatenate([a, jnp.zeros((rows - a.shape[0],) + a.shape[1:], a.dtype)], axis=0)


def _pad_cols(a, cols):
    return jnp.concatenate([a, jnp.zeros(a.shape[:-1] + (cols - a.shape[-1],), a.dtype)], axis=-1)


def kernel(x, c, ctx, c_ctx, w_ada, b_ada, norm_w, ffn_w_gate, ffn_w_up, ffn_w_down, w_in, ml_conv_w, ml_conv_b, ml_gate_b, ml_norm_w, w_proj_ml, ssm_conv_w, ssm_conv_b, ssm_dt_bias, ssm_a_log, ssm_d, ssm_norm_w, w_proj_ssm, w_gate, b_gate, w_out, final_norm_w):
    depth = w_ada.shape[0]
    assert depth == 1 and x.shape[0] == 1, "single layer, single batch element"
    t, d = x.shape[1], x.shape[2]
    n_tot = t + ctx.shape[1]
    rows_cm = t // GRID_W
    ml_qk = ml_conv_w.shape[2] // 2
    ml_v = ml_norm_w.shape[1]
    n_ml = 2 * ml_qk + 2 * ml_v
    ml_cols = n_ml + 4 * ML_HEADS
    ssm_inner = ssm_norm_w.shape[1]
    ssm_heads = ssm_d.shape[1]
    ssm_xbc = ssm_conv_w.shape[2]

    cc = _pad_rows(jnp.concatenate([c, c_ctx[None, :]], axis=0), 8)
    mod = _mod_call(cc, w_ada[0], b_ada[0])
    ml, mc = mod[0].reshape(N_MOD, d), mod[1].reshape(N_MOD, d)
    nw = norm_w[0]
    zero = jnp.zeros_like(nw[0])
    vec1 = _pad_rows(jnp.stack([nw[0], ml[0], ml[1], ml[2], mc[0], mc[1], mc[2],
                                nw[1], ml[3], ml[4], zero, mc[3], mc[4]]), 16)
    vec2 = _pad_rows(jnp.stack([nw[2], ml[6], ml[7], ml[8], zero, zero, zero, final_norm_w]), 16)
    wg, wu, wd = ffn_w_gate, ffn_w_up, ffn_w_down.astype(BF16)

    xall = jnp.concatenate([x[0], ctx[0]], axis=0)
    x1, h = _ffn_call(xall, vec1, wg, wu, wd, 0, t, final=False)

    h_cm = jnp.concatenate(
        [h[:t].reshape(rows_cm, GRID_W, d).transpose(1, 0, 2).reshape(t, d), h[t:]], axis=0)
    wt = w_in[0].T
    n_ssm = ssm_inner + ssm_xbc
    ml_cw = _pad_rows(ml_conv_w[0], 8)
    ml_post = jnp.concatenate([jnp.full((1, ml_qk), (ml_qk // ML_HEADS) ** -0.5, F32),
                               jnp.ones((1, ml_qk), F32)], axis=1)
    assert 2 * ssm_heads == LANES
    qk = _mmconv_call(_unit_permute(h_cm), wt, 0, ml_cw, ml_conv_b[0][None, :], ml_post, t)
    vo, gates = _mm_call(h_cm, wt, 2 * ml_qk, 2 * ml_v, n_tot, BF16, side_row0=n_ml)
    z, dt = _mm_call(h, wt, ml_cols, ssm_inner, n_tot, BF16, side_row0=ml_cols + n_ssm)
    xbc = _mmconv_call(_unit_permute(h), wt, ml_cols + ssm_inner, _pad_rows(ssm_conv_w[0], 8),
                       ssm_conv_b[0][None, :], None, t)

    gate_b = _pad_cols(ml_gate_b[0].reshape(1, 4 * ML_HEADS), LANES)
    h_f = _mlstm_call(qk, vo, gates, gate_b, t, reverse=False)
    h_b = _mlstm_call(qk, vo, gates, gate_b, t, reverse=True)
    y_ml = _proj_call(_mlgate_call(h_f, h_b, vo, ml_norm_w[0][None, :], t), w_proj_ml[0].astype(BF16))
    y_ml = y_ml.reshape(GRID_W, rows_cm, d).transpose(1, 0, 2).reshape(t, d)

    dt_bias = ssm_dt_bias[0].reshape(1, 2 * ssm_heads)
    a_neg = -jnp.exp(ssm_a_log[0].astype(F32)).reshape(1, 2 * ssm_heads)
    d_skip = jnp.repeat(ssm_d[0], SSM_HEADDIM)[None, :]
    y_f = _ssd_call(xbc, dt, dt_bias, a_neg, d_skip, t, reverse=False)
    y_b = _ssd_call(xbc, dt, dt_bias, a_neg, d_skip, t, reverse=True)
    y_ssm = _proj_call(_ssmgate_call(y_f, y_b, z, ssm_norm_w[0][None, :], t), w_proj_ssm[0].astype(BF16))

    mix = _gate_call(h, w_gate[0].astype(BF16), b_gate[0][None, :], y_ml, y_ssm, t)
    x2 = _out_call(mix, w_out[0].astype(BF16), x1, ml[5][None, :], t)
    out = _ffn_call(x2, vec2, wg, wu, wd, 1, t, final=True)
    return out[None]
```

```python
import functools

import jax
import jax.numpy as jnp
from jax import lax
from jax.experimental import pallas as pl
from jax.experimental.pallas import tpu as pltpu

F32 = jnp.float32
BF16 = jnp.bfloat16

GRID_W = 64
N_MOD = 9
EPS = 1e-6
CONV_W = 5
CONV_PAD = CONV_W // 2
HALO = 16
CONV_UNIT = 256
ROW_BLK = 16
FFN_ROW_BLK = 32
FFN_PREFETCH_MAX_ROWS = 768
ML_HEADS = 8
ML_L = 256
SSM_HEADDIM = 64
SSM_GROUPS = 8
SSM_DSTATE = 128
SSM_L = 128
LANES = 128
SUBLANES = 8
VMEM_LIMIT_BYTES = 56 * 1024 * 1024
NEG_INF = float("-inf")
LOG2E = 1.4426950408889634


def _cparams(sem):
    return pltpu.CompilerParams(dimension_semantics=sem, vmem_limit_bytes=VMEM_LIMIT_BYTES)


def _pick(n, cands):
    for c in cands:
        if n % c == 0:
            return c
    raise ValueError(f"no tile for {n} in {cands}")


def _silu(x):
    return x * (1.0 / (1.0 + jnp.exp(-x)))


def _sigmoid(x):
    return 1.0 / (1.0 + jnp.exp(-x))


def _log_sigmoid(x):
    return jnp.minimum(x, 0.0) - jnp.log1p(jnp.exp(-jnp.abs(x)))


def _softplus(x):
    return jnp.maximum(x, 0.0) + jnp.log1p(jnp.exp(-jnp.abs(x)))


def _rms(x):
    return x * lax.rsqrt(jnp.mean(jnp.square(x), axis=-1, keepdims=True) + EPS)


def _dot(a, b):
    return jnp.dot(a, b, preferred_element_type=F32)


def _dot_nt(a, b):
    return lax.dot_general(a, b, (((1,), (1,)), ((), ())), preferred_element_type=F32)


def _dot_exact(a, b):
    return jnp.dot(a, b, preferred_element_type=F32, precision=lax.Precision.HIGHEST)


def _row_loop(n_rows, blk, body):
    def step(r, carry):
        body(pl.ds(pl.multiple_of(r * blk, blk), blk))
        return carry
    lax.fori_loop(0, n_rows // blk, step, 0, unroll=2)


def _mod_kernel(c_ref, w_ref, b_ref, o_ref):
    a = _silu(c_ref[...]).astype(BF16)
    o_ref[...] = _dot(a, w_ref[...].astype(BF16)) + b_ref[...]


def _mod_call(cc, w_ada, b_ada):
    k, n = w_ada.shape
    tn = _pick(n, (1024, 512, 256, 128))
    return pl.pallas_call(
        _mod_kernel,
        grid=(n // tn,),
        in_specs=[pl.BlockSpec((SUBLANES, k), lambda j: (0, 0)),
                  pl.BlockSpec((k, tn), lambda j: (0, j)),
                  pl.BlockSpec((1, tn), lambda j: (0, j))],
        out_specs=pl.BlockSpec((SUBLANES, tn), lambda j: (0, j)),
        out_shape=jax.ShapeDtypeStruct((SUBLANES, n), F32),
        compiler_params=_cparams(("arbitrary",)),
    )(cc, w_ada, b_ada.reshape(1, n))


def _ffn_kernel(x_ref, vec_ref, wg_ref, wu_ref, wd_ref, *rest, n_lat, final):
    if final:
        o_ref, h_s, acc_s = rest
    else:
        x1_ref, h2_ref, h_s, acc_s = rest
    i = pl.program_id(0)
    f = pl.program_id(1)
    tm = x_ref.shape[0]

    def vec(row, rows):
        off = jnp.where(i * tm + rows.start >= n_lat, 3, 0)
        return vec_ref[pl.ds(row + off, 1), :]

    @pl.when(f == 0)
    def _():
        def body(rows):
            h = _rms(x_ref[rows, :]) * (vec_ref[0:1, :] * (1.0 + vec(2, rows))) + vec(1, rows)
            h_s[rows, :] = h.astype(BF16)
            acc_s[rows, :] = jnp.zeros((FFN_ROW_BLK, acc_s.shape[1]), F32)
        _row_loop(tm, FFN_ROW_BLK, body)

    h = h_s[...]
    a = _silu(_dot(h, wg_ref[...].astype(BF16))) * _dot(h, wu_ref[...].astype(BF16))
    acc_s[...] += _dot(a.astype(BF16), wd_ref[...])

    @pl.when(f == pl.num_programs(1) - 1)
    def _():
        def body(rows):
            x1 = x_ref[rows, :] + (0.5 * vec(3, rows)) * acc_s[rows, :]
            if final:
                o_ref[rows, :] = _rms(x1) * vec_ref[7:8, :]
            else:
                x1_ref[rows, :] = x1
                h2 = _rms(x1) * (vec_ref[7:8, :] * (1.0 + vec(9, rows))) + vec(8, rows)
                h2_ref[rows, :] = h2.astype(BF16)
        _row_loop(tm, FFN_ROW_BLK, body)


def _ffn_call(x, vecs, wg, wu, wd, which, n_lat, final):
    m, d = x.shape
    dff = wg.shape[3]
    tm = _pick(m, (1024, 768, 512, 256))
    tf = _pick(dff, (512, 256, 128))
    row = pl.BlockSpec((tm, d), lambda i, f: (i, 0), pipeline_mode=pl.Buffered(1))
    x_row = pl.BlockSpec((tm, d), lambda i, f: (i, 0)) if tm <= FFN_PREFETCH_MAX_ROWS else row
    if final:
        out_shape = jax.ShapeDtypeStruct((m, d), F32)
        out_specs = row
    else:
        out_shape = (jax.ShapeDtypeStruct((m, d), F32), jax.ShapeDtypeStruct((m, d), BF16))
        out_specs = (row, row)
    return pl.pallas_call(
        functools.partial(_ffn_kernel, n_lat=n_lat, final=final),
        grid=(m // tm, dff // tf),
        in_specs=[x_row,
                  pl.BlockSpec(vecs.shape, lambda i, f: (0, 0)),
                  pl.BlockSpec((None, None, d, tf), lambda i, f: (0, which, 0, f)),
                  pl.BlockSpec((None, None, d, tf), lambda i, f: (0, which, 0, f)),
                  pl.BlockSpec((None, None, tf, d), lambda i, f: (0, which, f, 0))],
        out_specs=out_specs,
        out_shape=out_shape,
        scratch_shapes=[pltpu.VMEM((tm, d), BF16), pltpu.VMEM((tm, d), F32)],
        compiler_params=_cparams(("arbitrary", "arbitrary")),
    )(x, vecs, wg, wu, wd)


def _mm_kernel(a_ref, w_ref, *rest):
    if len(rest) == 3:
        w2_ref, o_ref, o2_ref = rest

        @pl.when(pl.program_id(1) == 0)
        def _():
            o2_ref[...] = _dot_nt(a_ref[...], w2_ref[...].astype(BF16))
    else:
        (o_ref,) = rest
    o_ref[...] = _dot_nt(a_ref[...], w_ref[...].astype(BF16)).astype(o_ref.dtype)


def _mm_call(a, wt, row0, n, m, out_dtype, side_row0=None):
    k = a.shape[1]
    tm = _pick(m, (1408, 1024, 768, 512, 256))
    tn = _pick(n, (1024, 512, 256, 128))
    assert row0 % SUBLANES == 0
    wspec = lambda rows, idx: pl.BlockSpec((pl.Element(rows), pl.Element(k)), idx)
    in_specs = [pl.BlockSpec((tm, k), lambda i, j: (i, 0)),
                wspec(tn, lambda i, j: (pl.multiple_of(row0 + j * tn, SUBLANES), 0))]
    out_specs = pl.BlockSpec((tm, tn), lambda i, j: (i, j))
    out_shape = jax.ShapeDtypeStruct((m, n), out_dtype)
    args = (a, wt)
    if side_row0 is not None:
        assert side_row0 % SUBLANES == 0
        in_specs.append(wspec(LANES, lambda i, j: (side_row0, 0)))
        out_specs = (out_specs, pl.BlockSpec((tm, LANES), lambda i, j: (i, 0)))
        out_shape = (out_shape, jax.ShapeDtypeStruct((m, LANES), F32))
        args = (a, wt, wt)
    return pl.pallas_call(
        _mm_kernel,
        grid=(m // tm, n // tn),
        in_specs=in_specs,
        out_specs=out_specs,
        out_shape=out_shape,
        compiler_params=_cparams(("arbitrary", "arbitrary")),
    )(*args)


def _mmconv_kernel(a_ref, ap_ref, an_ref, w_ref, cw_ref, cb_ref, *rest, n_lat, n_tot, scaled):
    if scaled:
        post_ref, o_ref, xs, ys, a_s = rest
    else:
        o_ref, xs, ys, a_s = rest
    i = pl.program_id(0)
    tm = a_ref.shape[0]
    tn = o_ref.shape[1]
    units = tm // CONV_UNIT
    ug = CONV_UNIT // SUBLANES
    pad = CONV_PAD * SUBLANES
    span = CONV_UNIT + 2 * pad
    seq_units = n_lat // CONV_UNIT
    last_unit = n_tot // CONV_UNIT - 1

    @pl.when(pl.program_id(1) == 0)
    def _():
        a_s[0:HALO, :] = ap_ref[...]
        a_s[HALO:HALO + tm, :] = a_ref[...]
        a_s[HALO + tm:2 * HALO + tm, :] = an_ref[...]

    w = w_ref[...].astype(BF16)
    p_all = _dot_nt(a_s[...], w)
    p_prev = p_all[0:HALO]
    p = p_all[HALO:HALO + tm]
    p_next = p_all[HALO + tm:2 * HALO + tm]
    sub = lax.broadcasted_iota(jnp.int32, (SUBLANES, 1), 0)

    def group(x, k):
        return x[k * SUBLANES:(k + 1) * SUBLANES]

    for u in range(units):
        g = i * units + u
        keep_prev = jnp.where((g == 0) | (g == seq_units), 0.0, 1.0)
        keep_next = jnp.where((g == last_unit) | (g == seq_units - 1), 0.0, 1.0)
        r0 = u * CONV_UNIT
        base = u * span
        pu = p[r0:r0 + CONV_UNIT]
        xs[base + pad:base + pad + CONV_UNIT, :] = pu
        before = p[r0 - pad:r0] if u > 0 else p_prev
        after = p[r0 + CONV_UNIT:r0 + CONV_UNIT + pad] if u < units - 1 else p_next
        for d in range(CONV_PAD):
            own = pltpu.roll(group(pu, ug - CONV_PAD + d), 1, 0)
            other = pltpu.roll(group(before, d), 1, 0) * keep_prev
            xs[base + d * SUBLANES:base + (d + 1) * SUBLANES, :] = jnp.where(sub == 0, other, own)
            own = pltpu.roll(group(pu, d), SUBLANES - 1, 0)
            other = pltpu.roll(group(after, d), SUBLANES - 1, 0) * keep_next
            lo = base + pad + CONV_UNIT + d * SUBLANES
            xs[lo:lo + SUBLANES, :] = jnp.where(sub == SUBLANES - 1, other, own)

        acc = cb_ref[...]
        for j in range(CONV_W):
            acc = acc + cw_ref[j:j + 1, :] * xs[base + SUBLANES * j:base + SUBLANES * j + CONV_UNIT, :]
        y = _silu(acc)
        if scaled:
            y = y * post_ref[...]
        for cblk in range(tn // LANES):
            lanes = slice(cblk * LANES, (cblk + 1) * LANES)
            ys[cblk, r0:r0 + CONV_UNIT, :] = y[:, lanes]
            for b in range(SUBLANES):
                o_ref[r0 + b * ug:r0 + (b + 1) * ug, lanes] = (
                    ys[cblk, pl.ds(r0 + b, ug, stride=SUBLANES), :].astype(o_ref.dtype))


def _mmconv_call(a, w, col0, conv_w, conv_b, post, n_lat):
    m, k = a.shape
    n = conv_w.shape[1]
    tm = _pick(m, (768, 512, 256))
    tn = _pick(n, (1024, 512, 256, 128))
    assert col0 % SUBLANES == 0 and tm % CONV_UNIT == 0 and n_lat % CONV_UNIT == 0 and m % CONV_UNIT == 0
    assert HALO == CONV_PAD * SUBLANES
    hb = tm // HALO
    last_halo = m // HALO - 1
    vec = pl.BlockSpec((1, tn), lambda i, j: (0, j))
    scaled = post is not None
    return pl.pallas_call(
        functools.partial(_mmconv_kernel, n_lat=n_lat, n_tot=m, scaled=scaled),
        grid=(m // tm, n // tn),
        in_specs=[pl.BlockSpec((tm, k), lambda i, j: (i, 0)),
                  pl.BlockSpec((HALO, k), lambda i, j: (jnp.maximum(i * hb - 1, 0), 0)),
                  pl.BlockSpec((HALO, k), lambda i, j: (jnp.minimum((i + 1) * hb, last_halo), 0)),
                  pl.BlockSpec((pl.Element(tn), pl.Element(k)),
                               lambda i, j: (pl.multiple_of(col0 + j * tn, SUBLANES), 0)),
                  pl.BlockSpec((SUBLANES, tn), lambda i, j: (0, j)),
                  vec] + ([vec] if scaled else []),
        out_specs=pl.BlockSpec((tm, tn), lambda i, j: (i, j)),
        out_shape=jax.ShapeDtypeStruct((m, n), BF16),
        scratch_shapes=[pltpu.VMEM(((tm // CONV_UNIT) * (CONV_UNIT + 2 * HALO), tn), F32),
                        pltpu.VMEM((tn // LANES, tm, LANES), F32),
                        pltpu.VMEM((tm + 2 * HALO, k), BF16)],
        compiler_params=_cparams(("arbitrary", "arbitrary")),
    )(a, a, a, w, conv_w, conv_b, *([post] if scaled else []))


def _unit_permute(x):
    rows, d = x.shape
    ug = CONV_UNIT // SUBLANES
    return x.reshape(rows // CONV_UNIT, SUBLANES, ug, d).transpose(0, 2, 1, 3).reshape(rows, d)


def _chunk_order(c, n_lat_chunks, n_ctx_chunks, reverse):
    if reverse:
        return n_lat_chunks + n_ctx_chunks - 1 - c
    return jnp.where(c < n_ctx_chunks, n_lat_chunks + c, c - n_ctx_chunks)


def _tri(chunk, reverse):
    r = lax.broadcasted_iota(jnp.int32, (chunk, chunk), 0)
    c = lax.broadcasted_iota(jnp.int32, (chunk, chunk), 1)
    return (c >= r) if reverse else (c <= r)


def _mlstm_kernel(q_ref, k_ref, v_ref, g_ref, gn_ref, gb_ref, o_ref, ct_s, n_s, m_s, bc_s, bct_s, gt_s,
                  bcn_s, bctn_s, gtn_s, *, reverse):
    L = q_ref.shape[0]
    dk = q_ref.shape[1] // ML_HEADS
    dv = v_ref.shape[1] // ML_HEADS
    c = pl.program_id(0)

    mask = _tri(L, reverse)

    def prefix(g_blk, bc_d, bct_d, gt_d):
        g = g_blk + gb_ref[...]
        bc = _dot_exact(mask.astype(F32), _log_sigmoid(g))
        bc_d[...] = bc
        bct_d[...] = bc.T
        gt_d[...] = g.T

    @pl.when(c == 0)
    def _():
        ct_s[...] = jnp.zeros_like(ct_s)
        n_s[...] = jnp.zeros_like(n_s)
        m_s[...] = jnp.zeros_like(m_s)
        prefix(g_ref[...], bc_s, bct_s, gt_s)

    prefix(gn_ref[...], bcn_s, bctn_s, gtn_s)

    def lanes(x, n):
        return jnp.concatenate([x] * (n // LANES), axis=1)

    bc = bc_s[...]
    bc_t = bct_s[...]
    g_t = gt_s[...]
    end = 0 if reverse else L - 1
    base = 2 * ML_HEADS if reverse else 0
    ones = jnp.ones((L, LANES), BF16)

    for h in range(ML_HEADS):
        io = base + h
        fo = base + ML_HEADS + h
        bb = jnp.broadcast_to(bc[:, fo:fo + 1], (L, LANES))
        b_row = bc_t[fo:fo + 1, :]
        ig_row = g_t[io:io + 1, :]
        b_end = bc[end:end + 1, fo:fo + 1]
        m0 = m_s[h:h + 1, :]
        q = q_ref[:, h * dk:(h + 1) * dk]
        k = k_ref[:, h * dk:(h + 1) * dk]
        v = v_ref[:, h * dv:(h + 1) * dv]
        ct0 = ct_s[h]
        n0 = n_s[h]

        logw = jnp.where(mask, lanes(bb, L) + (ig_row - b_row), NEG_INF)
        m_inter = bb + m0
        m_intra = jnp.broadcast_to(jnp.max(logw, axis=1, keepdims=True), (L, LANES))
        m_out = jnp.maximum(m_inter, m_intra)
        p = (_dot_nt(q, k) * jnp.exp(logw - lanes(m_out, L))).astype(BF16)
        s_inter = jnp.exp(m_inter - m_out)
        num = _dot(p, v) + lanes(s_inter, dv) * _dot(q, ct0.astype(BF16))
        den = _dot(p, ones) + s_inter * _dot(q, n0.astype(BF16))
        inv = 1.0 / jnp.maximum(jnp.abs(den), jnp.exp(-m_out))
        o_ref[:, h * dv:(h + 1) * dv] = (num * lanes(inv, dv)).astype(o_ref.dtype)

        a_row = b_end + ig_row - b_row
        m_loc = jnp.max(a_row, axis=1, keepdims=True)
        kw_t = (k.T.astype(F32) * jnp.exp(a_row - m_loc)).astype(BF16)
        m_new = jnp.maximum(b_end + m0, m_loc)
        s_old = jnp.exp(b_end + m0 - m_new)
        s_new = jnp.exp(m_loc - m_new)
        ct_s[h] = lanes(s_old, dv) * ct0 + lanes(s_new, dv) * _dot(kw_t, v)
        n_s[h] = s_old * n0 + s_new * _dot(kw_t, ones)
        m_s[h:h + 1, :] = m_new

    bc_s[...] = bcn_s[...]
    bct_s[...] = bctn_s[...]
    gt_s[...] = gtn_s[...]


def _mlstm_call(qk, vo, gates, gate_b, n_lat, reverse):
    rows = qk.shape[0]
    L = ML_L
    nqk = qk.shape[1] // 2
    nv = vo.shape[1] // 2
    dk = nqk // ML_HEADS
    dv = nv // ML_HEADS
    nlc = n_lat // L
    ncc = (rows - n_lat) // L
    order = functools.partial(_chunk_order, n_lat_chunks=nlc, n_ctx_chunks=ncc, reverse=reverse)
    return pl.pallas_call(
        functools.partial(_mlstm_kernel, reverse=reverse),
        grid=(nlc + ncc,),
        in_specs=[pl.BlockSpec((L, nqk), lambda c: (order(c), 0)),
                  pl.BlockSpec((L, nqk), lambda c: (order(c), 1)),
                  pl.BlockSpec((L, nv), lambda c: (order(c), 0)),
                  pl.BlockSpec((L, LANES), lambda c: (order(c), 0)),
                  pl.BlockSpec((L, LANES), lambda c: (order(jnp.minimum(c + 1, nlc + ncc - 1)), 0)),
                  pl.BlockSpec((1, LANES), lambda c: (0, 0))],
        out_specs=pl.BlockSpec((L, nv), lambda c: (order(c), 0)),
        out_shape=jax.ShapeDtypeStruct((rows, nv), BF16),
        scratch_shapes=[pltpu.VMEM((ML_HEADS, dk, dv), F32),
                        pltpu.VMEM((ML_HEADS, dk, LANES), F32),
                        pltpu.VMEM((ML_HEADS, LANES), F32)]
        + 2 * [pltpu.VMEM((L, LANES), F32), pltpu.VMEM((LANES, L), F32), pltpu.VMEM((LANES, L), F32)],
        compiler_params=_cparams(("arbitrary",)),
    )(qk, qk, vo, gates, gates, gate_b)


def _ssd_kernel(x_ref, bm_ref, cm_ref, dt_ref, dtn_ref, dtb_ref, a_ref, dsk_ref, y_ref,
                st_s, bc_s, bct_s, dtt_s, bcn_s, bctn_s, dttn_s, *, reverse):
    L = x_ref.shape[0]
    nx = x_ref.shape[1]
    c = pl.program_id(0)
    mask = _tri(L, reverse)

    def prefix(dt_blk, bc_d, bct_d, dtt_d):
        dtv = _softplus(dt_blk + dtb_ref[...])
        bc = _dot_exact(mask.astype(F32), dtv * a_ref[...]) * LOG2E
        bc_d[...] = bc
        bct_d[...] = bc.T
        dtt_d[...] = jnp.log2(dtv).T

    @pl.when(c == 0)
    def _():
        st_s[...] = jnp.zeros_like(st_s)
        prefix(dt_ref[...], bc_s, bct_s, dtt_s)

    prefix(dtn_ref[...], bcn_s, bctn_s, dttn_s)

    bc = bc_s[...]
    bc_t = bct_s[...]
    ldt_t = dtt_s[...]
    end = 0 if reverse else L - 1
    nheads = nx // SSM_HEADDIM
    hpg = nheads // SSM_GROUPS
    base = nheads if reverse else 0
    pw = 2 * SSM_HEADDIM
    lo = lax.broadcasted_iota(jnp.int32, (1, pw), 1) < SSM_HEADDIM
    lo_b = jnp.where(lo, 1.0, 0.0).astype(BF16)
    hi_b = jnp.where(lo, 0.0, 1.0).astype(BF16)

    for g in range(SSM_GROUPS):
        bm = bm_ref[:, g * SSM_DSTATE:(g + 1) * SSM_DSTATE]
        cm = cm_ref[:, g * SSM_DSTATE:(g + 1) * SSM_DSTATE]
        cb = _dot_nt(cm, bm)
        bm_t = bm.astype(F32).T
        for pr in range(hpg // 2):
            col = (g * hpg + 2 * pr) * SSM_HEADDIM
            xp = x_ref[:, col:col + pw]
            ms, ebs, bts, decs = [], [], [], []
            for e in range(2):
                co = base + g * hpg + 2 * pr + e
                bb = jnp.broadcast_to(bc[:, co:co + 1], (L, L))
                r_row = bc_t[co:co + 1, :] - ldt_t[co:co + 1, :]
                b_end = bc[end:end + 1, co:co + 1]
                ms.append((cb * jnp.exp2(jnp.where(mask, bb - r_row, NEG_INF))).astype(BF16))
                ebs.append(jnp.exp2(bb[:, 0:pw]))
                bts.append((bm_t * jnp.exp2(b_end - r_row)).astype(BF16))
                decs.append(jnp.exp2(b_end))
            x_lo = xp * lo_b
            x_hi = xp * hi_b
            s0 = st_s[:, col:col + pw]
            y = _dot(ms[0], x_lo) + _dot(ms[1], x_hi)
            y = y + jnp.where(lo, ebs[0], ebs[1]) * _dot(cm, s0.astype(BF16))
            if not reverse:
                y = y + dsk_ref[:, col:col + pw] * xp.astype(F32)
            y_ref[:, col:col + pw] = y.astype(y_ref.dtype)
            st_s[:, col:col + pw] = (s0 * jnp.where(lo, decs[0], decs[1])
                                     + _dot(bts[0], x_lo) + _dot(bts[1], x_hi))

    bc_s[...] = bcn_s[...]
    bct_s[...] = bctn_s[...]
    dtt_s[...] = dttn_s[...]


def _ssd_call(xbc, dt, dt_bias, a_neg, d_skip, n_lat, reverse):
    rows = xbc.shape[0]
    L = SSM_L
    assert L >= 2 * SSM_HEADDIM
    nb = SSM_GROUPS * SSM_DSTATE
    nx = xbc.shape[1] - 2 * nb
    nlc = n_lat // L
    ncc = (rows - n_lat) // L
    order = functools.partial(_chunk_order, n_lat_chunks=nlc, n_ctx_chunks=ncc, reverse=reverse)
    xblk = nx // nb
    nc = nlc + ncc
    return pl.pallas_call(
        functools.partial(_ssd_kernel, reverse=reverse),
        grid=(nc,),
        in_specs=[pl.BlockSpec((L, nx), lambda c: (order(c), 0)),
                  pl.BlockSpec((L, nb), lambda c: (order(c), xblk)),
                  pl.BlockSpec((L, nb), lambda c: (order(c), xblk + 1)),
                  pl.BlockSpec((L, LANES), lambda c: (order(c), 0)),
                  pl.BlockSpec((L, LANES), lambda c: (order(jnp.minimum(c + 1, nc - 1)), 0)),
                  pl.BlockSpec((1, LANES), lambda c: (0, 0)),
                  pl.BlockSpec((1, LANES), lambda c: (0, 0)),
                  pl.BlockSpec((1, nx), lambda c: (0, 0))],
        out_specs=pl.BlockSpec((L, nx), lambda c: (order(c), 0)),
        out_shape=jax.ShapeDtypeStruct((rows, nx), BF16),
        scratch_shapes=[pltpu.VMEM((SSM_DSTATE, nx), F32)]
        + 2 * [pltpu.VMEM((L, LANES), F32), pltpu.VMEM((LANES, L), F32), pltpu.VMEM((LANES, L), F32)],
        compiler_params=_cparams(("arbitrary",)),
    )(xbc, xbc, xbc, dt, dt, dt_bias, a_neg, d_skip)


def _mlgate_kernel(hf_ref, hb_ref, o_ref, nw_ref, a_ref):
    dv = a_ref.shape[1] // ML_HEADS

    def body(rows):
        for h in range(ML_HEADS):
            sl = slice(h * dv, (h + 1) * dv)
            hh = _rms(hf_ref[rows, sl].astype(F32) + hb_ref[rows, sl].astype(F32)) * nw_ref[:, sl]
            a_ref[rows, sl] = (_sigmoid(o_ref[rows, sl].astype(F32)) * hh).astype(BF16)
    _row_loop(a_ref.shape[0], ROW_BLK, body)


def _mlgate_call(hf, hb, vo, norm_w, n_lat):
    nv = hf.shape[1]
    tm = _pick(n_lat, (512, 256))
    row = pl.BlockSpec((tm, nv), lambda i: (i, 0))
    return pl.pallas_call(
        _mlgate_kernel,
        grid=(n_lat // tm,),
        in_specs=[row, row, pl.BlockSpec((tm, nv), lambda i: (i, 1)), pl.BlockSpec((1, nv), lambda i: (0, 0))],
        out_specs=row,
        out_shape=jax.ShapeDtypeStruct((n_lat, nv), BF16),
        compiler_params=_cparams(("arbitrary",)),
    )(hf, hb, vo, norm_w)


def _ssmgate_kernel(yf_ref, yb_ref, z_ref, nw_ref, a_ref):
    gw = a_ref.shape[1] // SSM_GROUPS

    def body(rows):
        for g in range(SSM_GROUPS):
            sl = slice(g * gw, (g + 1) * gw)
            y = (yf_ref[rows, sl].astype(F32) + yb_ref[rows, sl].astype(F32)) * _silu(z_ref[rows, sl].astype(F32))
            a_ref[rows, sl] = (_rms(y) * nw_ref[:, sl]).astype(BF16)
    _row_loop(a_ref.shape[0], ROW_BLK, body)


def _ssmgate_call(yf, yb, z, norm_w, n_lat):
    nx = yf.shape[1]
    tm = _pick(n_lat, (512, 256))
    row = pl.BlockSpec((tm, nx), lambda i: (i, 0))
    return pl.pallas_call(
        _ssmgate_kernel,
        grid=(n_lat // tm,),
        in_specs=[row, row, row, pl.BlockSpec((1, nx), lambda i: (0, 0))],
        out_specs=row,
        out_shape=jax.ShapeDtypeStruct((n_lat, nx), BF16),
        compiler_params=_cparams(("arbitrary",)),
    )(yf, yb, z, norm_w)


def _proj_kernel(a_ref, w_ref, y_ref):
    y_ref[...] = _dot(a_ref[...], w_ref[...]).astype(y_ref.dtype)


def _proj_call(a, w):
    m, k = a.shape
    n = w.shape[1]
    tm = _pick(m, (1024, 512, 256))
    tn = _pick(n, (1024, 512, 256, 128))
    return pl.pallas_call(
        _proj_kernel,
        grid=(m // tm, n // tn),
        in_specs=[pl.BlockSpec((tm, k), lambda i, j: (i, 0)),
                  pl.BlockSpec((k, tn), lambda i, j: (0, j))],
        out_specs=pl.BlockSpec((tm, tn), lambda i, j: (i, j)),
        out_shape=jax.ShapeDtypeStruct((m, n), BF16),
        compiler_params=_cparams(("arbitrary", "arbitrary")),
    )(a, w)


def _gate_kernel(h_ref, w1_ref, w2_ref, b1_ref, b2_ref, yml_ref, yssm_ref, o_ref):
    h = h_ref[...]
    g1 = _sigmoid(_dot(h, w1_ref[...]) + b1_ref[...])
    g2 = _sigmoid(_dot(h, w2_ref[...]) + b2_ref[...])
    o_ref[...] = (g1 * yml_ref[...].astype(F32) + g2 * yssm_ref[...].astype(F32)).astype(o_ref.dtype)


def _gate_call(h, w_gate, b_gate, y_ml, y_ssm, n_lat):
    d = h.shape[1]
    tm = _pick(n_lat, (1024, 512, 256))
    tn = _pick(d, (1024, 512, 256, 128))
    nj = d // tn
    tile = pl.BlockSpec((tm, tn), lambda i, j: (i, j))
    return pl.pallas_call(
        _gate_kernel,
        grid=(n_lat // tm, nj),
        in_specs=[pl.BlockSpec((tm, d), lambda i, j: (i, 0)),
                  pl.BlockSpec((d, tn), lambda i, j: (0, j)),
                  pl.BlockSpec((d, tn), lambda i, j: (0, nj + j)),
                  pl.BlockSpec((1, tn), lambda i, j: (0, j)),
                  pl.BlockSpec((1, tn), lambda i, j: (0, nj + j)),
                  tile, tile],
        out_specs=tile,
        out_shape=jax.ShapeDtypeStruct((n_lat, d), BF16),
        compiler_params=_cparams(("arbitrary", "arbitrary")),
    )(h, w_gate, w_gate, b_gate, b_gate, y_ml, y_ssm)


def _out_kernel(a_ref, w_ref, x_ref, g_ref, o_ref):
    o_ref[...] = x_ref[...] + g_ref[...] * _dot(a_ref[...], w_ref[...])


def _out_call(mix, w_out, x1, gate, n_lat):
    d = mix.shape[1]
    tm = _pick(n_lat, (1024, 512, 256))
    tn = _pick(d, (1024, 512, 256, 128))
    tile = pl.BlockSpec((tm, tn), lambda i, j: (i, j))
    return pl.pallas_call(
        _out_kernel,
        grid=(n_lat // tm, d // tn),
        in_specs=[pl.BlockSpec((tm, d), lambda i, j: (i, 0)),
                  pl.BlockSpec((d, tn), lambda i, j: (0, j)),
                  tile,
                  pl.BlockSpec((1, tn), lambda i, j: (0, j))],
        out_specs=tile,
        out_shape=jax.ShapeDtypeStruct((n_lat, d), F32),
        compiler_params=_cparams(("arbitrary", "arbitrary")),
    )(mix, w_out, x1, gate)


def _pad_rows(a, rows):
    return jnp.concatenate([a, jnp.zeros((rows - a.shape[0],) + a.shape[1:], a.dtype)], axis=0)


def _pad_cols(a, cols):
    return jnp.concatenate([a, jnp.zeros(a.shape[:-1] + (cols - a.shape[-1],), a.dtype)], axis=-1)


def kernel(x, c, ctx, c_ctx, w_ada, b_ada, norm_w, ffn_w_gate, ffn_w_up, ffn_w_down, w_in, ml_conv_w, ml_conv_b, ml_gate_b, ml_norm_w, w_proj_ml, ssm_conv_w, ssm_conv_b, ssm_dt_bias, ssm_a_log, ssm_d, ssm_norm_w, w_proj_ssm, w_gate, b_gate, w_out, final_norm_w):
    depth = w_ada.shape[0]
    assert depth == 1 and x.shape[0] == 1, "single layer, single batch element"
    t, d = x.shape[1], x.shape[2]
    n_tot = t + ctx.shape[1]
    rows_cm = t // GRID_W
    ml_qk = ml_conv_w.shape[2] // 2
    ml_v = ml_norm_w.shape[1]
    n_ml = 2 * ml_qk + 2 * ml_v
    ml_cols = n_ml + 4 * ML_HEADS
    ssm_inner = ssm_norm_w.shape[1]
    ssm_heads = ssm_d.shape[1]
    ssm_xbc = ssm_conv_w.shape[2]

    cc = _pad_rows(jnp.concatenate([c, c_ctx[None, :]], axis=0), SUBLANES)
    mod = _mod_call(cc, w_ada[0], b_ada[0])
    ml, mc = mod[0].reshape(N_MOD, d), mod[1].reshape(N_MOD, d)
    nw = norm_w[0]
    zero = jnp.zeros_like(nw[0])
    vec1 = _pad_rows(jnp.stack([nw[0], ml[0], ml[1], ml[2], mc[0], mc[1], mc[2],
                                nw[1], ml[3], ml[4], zero, mc[3], mc[4]]), 16)
    vec2 = _pad_rows(jnp.stack([nw[2], ml[6], ml[7], ml[8], zero, zero, zero, final_norm_w]), 16)
    wg, wu, wd = ffn_w_gate, ffn_w_up, ffn_w_down.astype(BF16)

    xall = jnp.concatenate([x[0], ctx[0]], axis=0)
    x1, h = _ffn_call(xall, vec1, wg, wu, wd, 0, t, final=False)

    h_cm = jnp.concatenate(
        [h[:t].reshape(rows_cm, GRID_W, d).transpose(1, 0, 2).reshape(t, d), h[t:]], axis=0)
    wt = w_in[0].T
    n_ssm = ssm_inner + ssm_xbc
    ml_cw = _pad_rows(ml_conv_w[0], SUBLANES)
    ml_post = jnp.concatenate([jnp.full((1, ml_qk), (ml_qk // ML_HEADS) ** -0.5, F32),
                               jnp.ones((1, ml_qk), F32)], axis=1)
    assert 2 * ssm_heads == LANES
    qk = _mmconv_call(_unit_permute(h_cm), wt, 0, ml_cw, ml_conv_b[0][None, :], ml_post, t)
    vo, gates = _mm_call(h_cm, wt, 2 * ml_qk, 2 * ml_v, n_tot, BF16, side_row0=n_ml)
    z, dt = _mm_call(h, wt, ml_cols, ssm_inner, n_tot, BF16, side_row0=ml_cols + n_ssm)
    xbc = _mmconv_call(_unit_permute(h), wt, ml_cols + ssm_inner, _pad_rows(ssm_conv_w[0], SUBLANES),
                       ssm_conv_b[0][None, :], None, t)

    gate_b = _pad_cols(ml_gate_b[0].reshape(1, 4 * ML_HEADS), LANES)
    h_f = _mlstm_call(qk, vo, gates, gate_b, t, reverse=False)
    h_b = _mlstm_call(qk, vo, gates, gate_b, t, reverse=True)
    y_ml = _proj_call(_mlgate_call(h_f, h_b, vo, ml_norm_w[0][None, :], t), w_proj_ml[0].astype(BF16))
    y_ml = y_ml.reshape(GRID_W, rows_cm, d).transpose(1, 0, 2).reshape(t, d)

    dt_bias = ssm_dt_bias[0].reshape(1, 2 * ssm_heads)
    a_neg = -jnp.exp(ssm_a_log[0].astype(F32)).reshape(1, 2 * ssm_heads)
    d_skip = jnp.repeat(ssm_d[0], SSM_HEADDIM)[None, :]
    y_f = _ssd_call(xbc, dt, dt_bias, a_neg, d_skip, t, reverse=False)
    y_b = _ssd_call(xbc, dt, dt_bias, a_neg, d_skip, t, reverse=True)
    y_ssm = _proj_call(_ssmgate_call(y_f, y_b, z, ssm_norm_w[0][None, :], t), w_proj_ssm[0].astype(BF16))

    mix = _gate_call(h, w_gate[0].astype(BF16), b_gate[0][None, :], y_ml, y_ssm, t)
    x2 = _out_call(mix, w_out[0].astype(BF16), x1, ml[5][None, :], t)
    out = _ffn_call(x2, vec2, wg, wu, wd, 1, t, final=True)
    return out[None]
```

```python
import functools

import jax
import jax.numpy as jnp
from jax import lax
from jax.experimental import pallas as pl
from jax.experimental.pallas import tpu as pltpu

F32 = jnp.float32
BF16 = jnp.bfloat16

GRID_W = 64
N_MOD = 9
EPS = 1e-6
CONV_W = 5
CONV_PAD = CONV_W // 2
HALO = 16
CONV_UNIT = 256
ROW_BLK = 16
FFN_ROW_BLK = 32
FFN_PREFETCH_MAX_ROWS = 768
ML_HEADS = 8
ML_L = 256
SSM_HEADDIM = 64
SSM_GROUPS = 8
SSM_DSTATE = 128
SSM_L = 128
LANES = 128
SUBLANES = 8
VMEM_LIMIT_BYTES = 56 * 1024 * 1024
NEG_INF = float("-inf")
LOG2E = 1.4426950408889634


def _cparams(sem):
    return pltpu.CompilerParams(dimension_semantics=sem, vmem_limit_bytes=VMEM_LIMIT_BYTES)


def _pick(n, cands):
    for c in cands:
        if n % c == 0:
            return c
    raise ValueError(f"no tile for {n} in {cands}")


def _silu(x):
    return x * (1.0 / (1.0 + jnp.exp(-x)))


def _sigmoid(x):
    return 1.0 / (1.0 + jnp.exp(-x))


def _log_sigmoid(x):
    return jnp.minimum(x, 0.0) - jnp.log1p(jnp.exp(-jnp.abs(x)))


def _softplus(x):
    return jnp.maximum(x, 0.0) + jnp.log1p(jnp.exp(-jnp.abs(x)))


def _rms(x):
    return x * lax.rsqrt(jnp.mean(jnp.square(x), axis=-1, keepdims=True) + EPS)


def _dot(a, b):
    return jnp.dot(a, b, preferred_element_type=F32)


def _dot_nt(a, b):
    return lax.dot_general(a, b, (((1,), (1,)), ((), ())), preferred_element_type=F32)


def _dot_exact(a, b):
    return jnp.dot(a, b, preferred_element_type=F32, precision=lax.Precision.HIGHEST)


def _row_loop(n_rows, blk, body):
    def step(r, carry):
        body(pl.ds(pl.multiple_of(r * blk, blk), blk))
        return carry
    lax.fori_loop(0, n_rows // blk, step, 0, unroll=2)


def _mod_kernel(c_ref, w_ref, b_ref, o_ref):
    a = _silu(c_ref[...]).astype(BF16)
    o_ref[...] = _dot(a, w_ref[...].astype(BF16)) + b_ref[...]


def _mod_call(cc, w_ada, b_ada):
    k, n = w_ada.shape
    tn = _pick(n, (1024, 512, 256, 128))
    return pl.pallas_call(
        _mod_kernel,
        grid=(n // tn,),
        in_specs=[pl.BlockSpec((SUBLANES, k), lambda j: (0, 0)),
                  pl.BlockSpec((k, tn), lambda j: (0, j)),
                  pl.BlockSpec((1, tn), lambda j: (0, j))],
        out_specs=pl.BlockSpec((SUBLANES, tn), lambda j: (0, j)),
        out_shape=jax.ShapeDtypeStruct((SUBLANES, n), F32),
        compiler_params=_cparams(("arbitrary",)),
    )(cc, w_ada, b_ada.reshape(1, n))


def _ffn_kernel(x_ref, vec_ref, wg_ref, wu_ref, wd_ref, *rest, n_lat, final):
    if final:
        o_ref, h_s, acc_s = rest
    else:
        x1_ref, h2_ref, h_s, acc_s = rest
    i = pl.program_id(0)
    f = pl.program_id(1)
    tm = x_ref.shape[0]

    def vec(row, rows):
        off = jnp.where(i * tm + rows.start >= n_lat, 3, 0)
        return vec_ref[pl.ds(row + off, 1), :]

    @pl.when(f == 0)
    def _():
        def body(rows):
            h = _rms(x_ref[rows, :]) * (vec_ref[0:1, :] * (1.0 + vec(2, rows))) + vec(1, rows)
            h_s[rows, :] = h.astype(BF16)
            acc_s[rows, :] = jnp.zeros((FFN_ROW_BLK, acc_s.shape[1]), F32)
        _row_loop(tm, FFN_ROW_BLK, body)

    h = h_s[...]
    a = _silu(_dot(h, wg_ref[...].astype(BF16))) * _dot(h, wu_ref[...].astype(BF16))
    acc_s[...] += _dot(a.astype(BF16), wd_ref[...])

    @pl.when(f == pl.num_programs(1) - 1)
    def _():
        def body(rows):
            x1 = x_ref[rows, :] + (0.5 * vec(3, rows)) * acc_s[rows, :]
            if final:
                o_ref[rows, :] = _rms(x1) * vec_ref[7:8, :]
            else:
                x1_ref[rows, :] = x1
                h2 = _rms(x1) * (vec_ref[7:8, :] * (1.0 + vec(9, rows))) + vec(8, rows)
                h2_ref[rows, :] = h2.astype(BF16)
        _row_loop(tm, FFN_ROW_BLK, body)


def _ffn_call(x, vecs, wg, wu, wd, which, n_lat, final):
    m, d = x.shape
    dff = wg.shape[3]
    tm = _pick(m, (1024, 768, 512, 256))
    tf = _pick(dff, (512, 256, 128))
    row = pl.BlockSpec((tm, d), lambda i, f: (i, 0), pipeline_mode=pl.Buffered(1))
    x_row = pl.BlockSpec((tm, d), lambda i, f: (i, 0)) if tm <= FFN_PREFETCH_MAX_ROWS else row
    if final:
        out_shape = jax.ShapeDtypeStruct((m, d), F32)
        out_specs = row
    else:
        out_shape = (jax.ShapeDtypeStruct((m, d), F32), jax.ShapeDtypeStruct((m, d), BF16))
        out_specs = (row, row)
    return pl.pallas_call(
        functools.partial(_ffn_kernel, n_lat=n_lat, final=final),
        grid=(m // tm, dff // tf),
        in_specs=[x_row,
                  pl.BlockSpec(vecs.shape, lambda i, f: (0, 0)),
                  pl.BlockSpec((None, None, d, tf), lambda i, f: (0, which, 0, f)),
                  pl.BlockSpec((None, None, d, tf), lambda i, f: (0, which, 0, f)),
                  pl.BlockSpec((None, None, tf, d), lambda i, f: (0, which, f, 0))],
        out_specs=out_specs,
        out_shape=out_shape,
        scratch_shapes=[pltpu.VMEM((tm, d), BF16), pltpu.VMEM((tm, d), F32)],
        compiler_params=_cparams(("arbitrary", "arbitrary")),
    )(x, vecs, wg, wu, wd)


def _mm_kernel(a_ref, w_ref, *rest):
    if len(rest) == 3:
        w2_ref, o_ref, o2_ref = rest

        @pl.when(pl.program_id(1) == 0)
        def _():
            o2_ref[...] = _dot_nt(a_ref[...], w2_ref[...].astype(BF16))
    else:
        (o_ref,) = rest
    o_ref[...] = _dot_nt(a_ref[...], w_ref[...].astype(BF16)).astype(o_ref.dtype)


def _mm_call(a, wt, row0, n, m, out_dtype, side_row0=None):
    k = a.shape[1]
    tm = _pick(m, (1408, 1024, 768, 512, 256))
    tn = _pick(n, (1024, 512, 256, 128))
    assert row0 % SUBLANES == 0
    wspec = lambda rows, idx: pl.BlockSpec((pl.Element(rows), pl.Element(k)), idx)
    in_specs = [pl.BlockSpec((tm, k), lambda i, j: (i, 0)),
                wspec(tn, lambda i, j: (pl.multiple_of(row0 + j * tn, SUBLANES), 0))]
    out_specs = pl.BlockSpec((tm, tn), lambda i, j: (i, j))
    out_shape = jax.ShapeDtypeStruct((m, n), out_dtype)
    args = (a, wt)
    if side_row0 is not None:
        assert side_row0 % SUBLANES == 0
        in_specs.append(wspec(LANES, lambda i, j: (side_row0, 0)))
        out_specs = (out_specs, pl.BlockSpec((tm, LANES), lambda i, j: (i, 0)))
        out_shape = (out_shape, jax.ShapeDtypeStruct((m, LANES), F32))
        args = (a, wt, wt)
    return pl.pallas_call(
        _mm_kernel,
        grid=(m // tm, n // tn),
        in_specs=in_specs,
        out_specs=out_specs,
        out_shape=out_shape,
        compiler_params=_cparams(("arbitrary", "arbitrary")),
    )(*args)


def _mmconv_kernel(a_ref, ap_ref, an_ref, w_ref, cw_ref, cb_ref, *rest, n_lat, n_tot, scaled):
    if scaled:
        post_ref, o_ref, xs, ys, a_s = rest
    else:
        o_ref, xs, ys, a_s = rest
    i = pl.program_id(0)
    tm = a_ref.shape[0]
    tn = o_ref.shape[1]
    units = tm // CONV_UNIT
    ug = CONV_UNIT // SUBLANES
    pad = CONV_PAD * SUBLANES
    span = CONV_UNIT + 2 * pad
    seq_units = n_lat // CONV_UNIT
    last_unit = n_tot // CONV_UNIT - 1

    @pl.when(pl.program_id(1) == 0)
    def _():
        a_s[0:HALO, :] = ap_ref[...]
        a_s[HALO:HALO + tm, :] = a_ref[...]
        a_s[HALO + tm:2 * HALO + tm, :] = an_ref[...]

    w = w_ref[...].astype(BF16)
    p_all = _dot_nt(a_s[...], w)
    p_prev = p_all[0:HALO]
    p = p_all[HALO:HALO + tm]
    p_next = p_all[HALO + tm:2 * HALO + tm]
    sub = lax.broadcasted_iota(jnp.int32, (SUBLANES, 1), 0)

    def group(x, k):
        return x[k * SUBLANES:(k + 1) * SUBLANES]

    for u in range(units):
        g = i * units + u
        keep_prev = jnp.where((g == 0) | (g == seq_units), 0.0, 1.0)
        keep_next = jnp.where((g == last_unit) | (g == seq_units - 1), 0.0, 1.0)
        r0 = u * CONV_UNIT
        base = u * span
        pu = p[r0:r0 + CONV_UNIT]
        xs[base + pad:base + pad + CONV_UNIT, :] = pu
        before = p[r0 - pad:r0] if u > 0 else p_prev
        after = p[r0 + CONV_UNIT:r0 + CONV_UNIT + pad] if u < units - 1 else p_next
        for d in range(CONV_PAD):
            own = pltpu.roll(group(pu, ug - CONV_PAD + d), 1, 0)
            other = pltpu.roll(group(before, d), 1, 0) * keep_prev
            xs[base + d * SUBLANES:base + (d + 1) * SUBLANES, :] = jnp.where(sub == 0, other, own)
            own = pltpu.roll(group(pu, d), SUBLANES - 1, 0)
            other = pltpu.roll(group(after, d), SUBLANES - 1, 0) * keep_next
            lo = base + pad + CONV_UNIT + d * SUBLANES
            xs[lo:lo + SUBLANES, :] = jnp.where(sub == SUBLANES - 1, other, own)

        acc = cb_ref[...]
        for j in range(CONV_W):
            acc = acc + cw_ref[j:j + 1, :] * xs[base + SUBLANES * j:base + SUBLANES * j + CONV_UNIT, :]
        y = _silu(acc)
        if scaled:
            y = y * post_ref[...]
        for cblk in range(tn // LANES):
            lanes = slice(cblk * LANES, (cblk + 1) * LANES)
            ys[cblk, r0:r0 + CONV_UNIT, :] = y[:, lanes]
            for b in range(SUBLANES):
                o_ref[r0 + b * ug:r0 + (b + 1) * ug, lanes] = (
                    ys[cblk, pl.ds(r0 + b, ug, stride=SUBLANES), :].astype(o_ref.dtype))


def _mmconv_call(a, w, col0, conv_w, conv_b, post, n_lat):
    m, k = a.shape
    n = conv_w.shape[1]
    tm = _pick(m, (768, 512, 256))
    tn = _pick(n, (1024, 512, 256, 128))
    assert col0 % SUBLANES == 0 and tm % CONV_UNIT == 0 and n_lat % CONV_UNIT == 0 and m % CONV_UNIT == 0
    assert HALO == CONV_PAD * SUBLANES
    hb = tm // HALO
    last_halo = m // HALO - 1
    vec = pl.BlockSpec((1, tn), lambda i, j: (0, j))
    scaled = post is not None
    return pl.pallas_call(
        functools.partial(_mmconv_kernel, n_lat=n_lat, n_tot=m, scaled=scaled),
        grid=(m // tm, n // tn),
        in_specs=[pl.BlockSpec((tm, k), lambda i, j: (i, 0)),
                  pl.BlockSpec((HALO, k), lambda i, j: (jnp.maximum(i * hb - 1, 0), 0)),
                  pl.BlockSpec((HALO, k), lambda i, j: (jnp.minimum((i + 1) * hb, last_halo), 0)),
                  pl.BlockSpec((pl.Element(tn), pl.Element(k)),
                               lambda i, j: (pl.multiple_of(col0 + j * tn, SUBLANES), 0)),
                  pl.BlockSpec((SUBLANES, tn), lambda i, j: (0, j)),
                  vec] + ([vec] if scaled else []),
        out_specs=pl.BlockSpec((tm, tn), lambda i, j: (i, j)),
        out_shape=jax.ShapeDtypeStruct((m, n), BF16),
        scratch_shapes=[pltpu.VMEM(((tm // CONV_UNIT) * (CONV_UNIT + 2 * HALO), tn), F32),
                        pltpu.VMEM((tn // LANES, tm, LANES), F32),
                        pltpu.VMEM((tm + 2 * HALO, k), BF16)],
        compiler_params=_cparams(("arbitrary", "arbitrary")),
    )(a, a, a, w, conv_w, conv_b, *([post] if scaled else []))


def _unit_permute(x):
    rows, d = x.shape
    ug = CONV_UNIT // SUBLANES
    return x.reshape(rows // CONV_UNIT, SUBLANES, ug, d).transpose(0, 2, 1, 3).reshape(rows, d)


def _chunk_order(c, n_lat_chunks, n_ctx_chunks, reverse):
    if reverse:
        return n_lat_chunks + n_ctx_chunks - 1 - c
    return jnp.where(c < n_ctx_chunks, n_lat_chunks + c, c - n_ctx_chunks)


def _tri(chunk, reverse):
    r = lax.broadcasted_iota(jnp.int32, (chunk, chunk), 0)
    c = lax.broadcasted_iota(jnp.int32, (chunk, chunk), 1)
    return (c >= r) if reverse else (c <= r)


def _mlstm_kernel(q_ref, k_ref, v_ref, g_ref, gn_ref, gb_ref, o_ref, ct_s, n_s, m_s, bc_s, bct_s, gt_s,
                  bcn_s, bctn_s, gtn_s, *, reverse):
    L = q_ref.shape[0]
    dk = q_ref.shape[1] // ML_HEADS
    dv = v_ref.shape[1] // ML_HEADS
    c = pl.program_id(0)

    mask = _tri(L, reverse)

    def prefix(g_blk, bc_d, bct_d, gt_d):
        g = g_blk + gb_ref[...]
        bc = _dot_exact(mask.astype(F32), _log_sigmoid(g))
        bc_d[...] = bc
        bct_d[...] = bc.T
        gt_d[...] = g.T

    @pl.when(c == 0)
    def _():
        ct_s[...] = jnp.zeros_like(ct_s)
        n_s[...] = jnp.zeros_like(n_s)
        m_s[...] = jnp.zeros_like(m_s)
        prefix(g_ref[...], bc_s, bct_s, gt_s)

    prefix(gn_ref[...], bcn_s, bctn_s, gtn_s)

    def lanes(x, n):
        return jnp.concatenate([x] * (n // LANES), axis=1)

    bc = bc_s[...]
    bc_t = bct_s[...]
    g_t = gt_s[...]
    end = 0 if reverse else L - 1
    base = 2 * ML_HEADS if reverse else 0
    ones = jnp.ones((L, LANES), BF16)

    for h in range(ML_HEADS):
        io = base + h
        fo = base + ML_HEADS + h
        bb = jnp.broadcast_to(bc[:, fo:fo + 1], (L, LANES))
        b_row = bc_t[fo:fo + 1, :]
        ig_row = g_t[io:io + 1, :]
        b_end = bc[end:end + 1, fo:fo + 1]
        m0 = m_s[h:h + 1, :]
        q = q_ref[:, h * dk:(h + 1) * dk]
        k = k_ref[:, h * dk:(h + 1) * dk]
        v = v_ref[:, h * dv:(h + 1) * dv]
        ct0 = ct_s[h]
        n0 = n_s[h]

        logw = jnp.where(mask, lanes(bb, L) + (ig_row - b_row), NEG_INF)
        m_inter = bb + m0
        m_intra = jnp.broadcast_to(jnp.max(logw, axis=1, keepdims=True), (L, LANES))
        m_out = jnp.maximum(m_inter, m_intra)
        p = (_dot_nt(q, k) * jnp.exp(logw - lanes(m_out, L))).astype(BF16)
        s_inter = jnp.exp(m_inter - m_out)
        num = _dot(p, v) + lanes(s_inter, dv) * _dot(q, ct0.astype(BF16))
        den = _dot(p, ones) + s_inter * _dot(q, n0.astype(BF16))
        inv = 1.0 / jnp.maximum(jnp.abs(den), jnp.exp(-m_out))
        o_ref[:, h * dv:(h + 1) * dv] = (num * lanes(inv, dv)).astype(o_ref.dtype)

        a_row = b_end + ig_row - b_row
        m_loc = jnp.max(a_row, axis=1, keepdims=True)
        kw_t = (k.T.astype(F32) * jnp.exp(a_row - m_loc)).astype(BF16)
        m_new = jnp.maximum(b_end + m0, m_loc)
        s_old = jnp.exp(b_end + m0 - m_new)
        s_new = jnp.exp(m_loc - m_new)
        ct_s[h] = lanes(s_old, dv) * ct0 + lanes(s_new, dv) * _dot(kw_t, v)
        n_s[h] = s_old * n0 + s_new * _dot(kw_t, ones)
        m_s[h:h + 1, :] = m_new

    bc_s[...] = bcn_s[...]
    bct_s[...] = bctn_s[...]
    gt_s[...] = gtn_s[...]


def _mlstm_call(qk, vo, gates, gate_b, n_lat, reverse):
    rows = qk.shape[0]
    L = ML_L
    nqk = qk.shape[1] // 2
    nv = vo.shape[1] // 2
    dk = nqk // ML_HEADS
    dv = nv // ML_HEADS
    nlc = n_lat // L
    ncc = (rows - n_lat) // L
    order = functools.partial(_chunk_order, n_lat_chunks=nlc, n_ctx_chunks=ncc, reverse=reverse)
    return pl.pallas_call(
        functools.partial(_mlstm_kernel, reverse=reverse),
        grid=(nlc + ncc,),
        in_specs=[pl.BlockSpec((L, nqk), lambda c: (order(c), 0)),
                  pl.BlockSpec((L, nqk), lambda c: (order(c), 1)),
                  pl.BlockSpec((L, nv), lambda c: (order(c), 0)),
                  pl.BlockSpec((L, LANES), lambda c: (order(c), 0)),
                  pl.BlockSpec((L, LANES), lambda c: (order(jnp.minimum(c + 1, nlc + ncc - 1)), 0)),
                  pl.BlockSpec((1, LANES), lambda c: (0, 0))],
        out_specs=pl.BlockSpec((L, nv), lambda c: (order(c), 0)),
        out_shape=jax.ShapeDtypeStruct((rows, nv), BF16),
        scratch_shapes=[pltpu.VMEM((ML_HEADS, dk, dv), F32),
                        pltpu.VMEM((ML_HEADS, dk, LANES), F32),
                        pltpu.VMEM((ML_HEADS, LANES), F32)]
        + 2 * [pltpu.VMEM((L, LANES), F32), pltpu.VMEM((LANES, L), F32), pltpu.VMEM((LANES, L), F32)],
        compiler_params=_cparams(("arbitrary",)),
    )(qk, qk, vo, gates, gates, gate_b)


def _ssd_kernel(x_ref, bm_ref, cm_ref, dt_ref, dtn_ref, dtb_ref, a_ref, dsk_ref, y_ref,
                st_s, bc_s, bct_s, dtt_s, bcn_s, bctn_s, dttn_s, *, reverse):
    L = x_ref.shape[0]
    nx = x_ref.shape[1]
    c = pl.program_id(0)
    mask = _tri(L, reverse)

    def prefix(dt_blk, bc_d, bct_d, dtt_d):
        dtv = _softplus(dt_blk + dtb_ref[...])
        bc = _dot_exact(mask.astype(F32), dtv * a_ref[...]) * LOG2E
        bc_d[...] = bc
        bct_d[...] = bc.T
        dtt_d[...] = jnp.log2(dtv).T

    @pl.when(c == 0)
    def _():
        st_s[...] = jnp.zeros_like(st_s)
        prefix(dt_ref[...], bc_s, bct_s, dtt_s)

    prefix(dtn_ref[...], bcn_s, bctn_s, dttn_s)

    bc = bc_s[...]
    bc_t = bct_s[...]
    ldt_t = dtt_s[...]
    end = 0 if reverse else L - 1
    nheads = nx // SSM_HEADDIM
    hpg = nheads // SSM_GROUPS
    base = nheads if reverse else 0
    pw = 2 * SSM_HEADDIM
    lo = lax.broadcasted_iota(jnp.int32, (1, pw), 1) < SSM_HEADDIM
    lo_b = jnp.where(lo, 1.0, 0.0).astype(BF16)
    hi_b = jnp.where(lo, 0.0, 1.0).astype(BF16)

    for g in range(SSM_GROUPS):
        bm = bm_ref[:, g * SSM_DSTATE:(g + 1) * SSM_DSTATE]
        cm = cm_ref[:, g * SSM_DSTATE:(g + 1) * SSM_DSTATE]
        cb = _dot_nt(cm, bm)
        bm_t = bm.astype(F32).T
        for pr in range(hpg // 2):
            col = (g * hpg + 2 * pr) * SSM_HEADDIM
            xp = x_ref[:, col:col + pw]
            x_half = (xp * lo_b, xp * hi_b)
            s0 = st_s[:, col:col + pw]
            y = None
            st = None
            bbs, decs = [], []
            for e in range(2):
                co = base + g * hpg + 2 * pr + e
                bb = jnp.broadcast_to(bc[:, co:co + 1], (L, L))
                r_row = bc_t[co:co + 1, :] - ldt_t[co:co + 1, :]
                b_end = bc[end:end + 1, co:co + 1]
                m = (cb * jnp.exp2(jnp.where(mask, bb - r_row, NEG_INF))).astype(BF16)
                ye = _dot(m, x_half[e])
                y = ye if y is None else y + ye
                bt = (bm_t * jnp.exp2(b_end - r_row)).astype(BF16)
                se = _dot(bt, x_half[e])
                st = se if st is None else st + se
                bbs.append(bb[:, 0:pw])
                decs.append(jnp.exp2(b_end))
            y = y + jnp.exp2(jnp.where(lo, bbs[0], bbs[1])) * _dot(cm, s0.astype(BF16))
            if not reverse:
                y = y + dsk_ref[:, col:col + pw] * xp.astype(F32)
            y_ref[:, col:col + pw] = y.astype(y_ref.dtype)
            st_s[:, col:col + pw] = s0 * jnp.where(lo, decs[0], decs[1]) + st

    bc_s[...] = bcn_s[...]
    bct_s[...] = bctn_s[...]
    dtt_s[...] = dttn_s[...]


def _ssd_call(xbc, dt, dt_bias, a_neg, d_skip, n_lat, reverse):
    rows = xbc.shape[0]
    L = SSM_L
    assert L >= 2 * SSM_HEADDIM
    nb = SSM_GROUPS * SSM_DSTATE
    nx = xbc.shape[1] - 2 * nb
    nlc = n_lat // L
    ncc = (rows - n_lat) // L
    order = functools.partial(_chunk_order, n_lat_chunks=nlc, n_ctx_chunks=ncc, reverse=reverse)
    xblk = nx // nb
    nc = nlc + ncc
    return pl.pallas_call(
        functools.partial(_ssd_kernel, reverse=reverse),
        grid=(nc,),
        in_specs=[pl.BlockSpec((L, nx), lambda c: (order(c), 0)),
                  pl.BlockSpec((L, nb), lambda c: (order(c), xblk)),
                  pl.BlockSpec((L, nb), lambda c: (order(c), xblk + 1)),
                  pl.BlockSpec((L, LANES), lambda c: (order(c), 0)),
                  pl.BlockSpec((L, LANES), lambda c: (order(jnp.minimum(c + 1, nc - 1)), 0)),
                  pl.BlockSpec((1, LANES), lambda c: (0, 0)),
                  pl.BlockSpec((1, LANES), lambda c: (0, 0)),
                  pl.BlockSpec((1, nx), lambda c: (0, 0))],
        out_specs=pl.BlockSpec((L, nx), lambda c: (order(c), 0)),
        out_shape=jax.ShapeDtypeStruct((rows, nx), BF16),
        scratch_shapes=[pltpu.VMEM((SSM_DSTATE, nx), F32)]
        + 2 * [pltpu.VMEM((L, LANES), F32), pltpu.VMEM((LANES, L), F32), pltpu.VMEM((LANES, L), F32)],
        compiler_params=_cparams(("arbitrary",)),
    )(xbc, xbc, xbc, dt, dt, dt_bias, a_neg, d_skip)


def _mlgate_kernel(hf_ref, hb_ref, o_ref, nw_ref, a_ref):
    dv = a_ref.shape[1] // ML_HEADS

    def body(rows):
        for h in range(ML_HEADS):
            sl = slice(h * dv, (h + 1) * dv)
            hh = _rms(hf_ref[rows, sl].astype(F32) + hb_ref[rows, sl].astype(F32)) * nw_ref[:, sl]
            a_ref[rows, sl] = (_sigmoid(o_ref[rows, sl].astype(F32)) * hh).astype(BF16)
    _row_loop(a_ref.shape[0], ROW_BLK, body)


def _mlgate_call(hf, hb, vo, norm_w, n_lat):
    nv = hf.shape[1]
    tm = _pick(n_lat, (512, 256))
    row = pl.BlockSpec((tm, nv), lambda i: (i, 0))
    return pl.pallas_call(
        _mlgate_kernel,
        grid=(n_lat // tm,),
        in_specs=[row, row, pl.BlockSpec((tm, nv), lambda i: (i, 1)), pl.BlockSpec((1, nv), lambda i: (0, 0))],
        out_specs=row,
        out_shape=jax.ShapeDtypeStruct((n_lat, nv), BF16),
        compiler_params=_cparams(("arbitrary",)),
    )(hf, hb, vo, norm_w)


def _ssmgate_kernel(yf_ref, yb_ref, z_ref, nw_ref, a_ref):
    gw = a_ref.shape[1] // SSM_GROUPS

    def body(rows):
        for g in range(SSM_GROUPS):
            sl = slice(g * gw, (g + 1) * gw)
            y = (yf_ref[rows, sl].astype(F32) + yb_ref[rows, sl].astype(F32)) * _silu(z_ref[rows, sl].astype(F32))
            a_ref[rows, sl] = (_rms(y) * nw_ref[:, sl]).astype(BF16)
    _row_loop(a_ref.shape[0], ROW_BLK, body)


def _ssmgate_call(yf, yb, z, norm_w, n_lat):
    nx = yf.shape[1]
    tm = _pick(n_lat, (512, 256))
    row = pl.BlockSpec((tm, nx), lambda i: (i, 0))
    return pl.pallas_call(
        _ssmgate_kernel,
        grid=(n_lat // tm,),
        in_specs=[row, row, row, pl.BlockSpec((1, nx), lambda i: (0, 0))],
        out_specs=row,
        out_shape=jax.ShapeDtypeStruct((n_lat, nx), BF16),
        compiler_params=_cparams(("arbitrary",)),
    )(yf, yb, z, norm_w)


def _proj_kernel(a_ref, w_ref, y_ref):
    y_ref[...] = _dot(a_ref[...], w_ref[...]).astype(y_ref.dtype)


def _proj_call(a, w):
    m, k = a.shape
    n = w.shape[1]
    tm = _pick(m, (1024, 512, 256))
    tn = _pick(n, (1024, 512, 256, 128))
    return pl.pallas_call(
        _proj_kernel,
        grid=(m // tm, n // tn),
        in_specs=[pl.BlockSpec((tm, k), lambda i, j: (i, 0)),
                  pl.BlockSpec((k, tn), lambda i, j: (0, j))],
        out_specs=pl.BlockSpec((tm, tn), lambda i, j: (i, j)),
        out_shape=jax.ShapeDtypeStruct((m, n), BF16),
        compiler_params=_cparams(("arbitrary", "arbitrary")),
    )(a, w)


def _gate_kernel(h_ref, w1_ref, w2_ref, b1_ref, b2_ref, yml_ref, yssm_ref, o_ref):
    h = h_ref[...]
    g1 = _sigmoid(_dot(h, w1_ref[...]) + b1_ref[...])
    g2 = _sigmoid(_dot(h, w2_ref[...]) + b2_ref[...])
    o_ref[...] = (g1 * yml_ref[...].astype(F32) + g2 * yssm_ref[...].astype(F32)).astype(o_ref.dtype)


def _gate_call(h, w_gate, b_gate, y_ml, y_ssm, n_lat):
    d = h.shape[1]
    tm = _pick(n_lat, (1024, 512, 256))
    tn = _pick(d, (1024, 512, 256, 128))
    nj = d // tn
    tile = pl.BlockSpec((tm, tn), lambda i, j: (i, j))
    return pl.pallas_call(
        _gate_kernel,
        grid=(n_lat // tm, nj),
        in_specs=[pl.BlockSpec((tm, d), lambda i, j: (i, 0)),
                  pl.BlockSpec((d, tn), lambda i, j: (0, j)),
                  pl.BlockSpec((d, tn), lambda i, j: (0, nj + j)),
                  pl.BlockSpec((1, tn), lambda i, j: (0, j)),
                  pl.BlockSpec((1, tn), lambda i, j: (0, nj + j)),
                  tile, tile],
        out_specs=tile,
        out_shape=jax.ShapeDtypeStruct((n_lat, d), BF16),
        compiler_params=_cparams(("arbitrary", "arbitrary")),
    )(h, w_gate, w_gate, b_gate, b_gate, y_ml, y_ssm)


def _out_kernel(a_ref, w_ref, x_ref, g_ref, o_ref):
    o_ref[...] = x_ref[...] + g_ref[...] * _dot(a_ref[...], w_ref[...])


def _out_call(mix, w_out, x1, gate, n_lat):
    d = mix.shape[1]
    tm = _pick(n_lat, (1024, 512, 256))
    tn = _pick(d, (1024, 512, 256, 128))
    tile = pl.BlockSpec((tm, tn), lambda i, j: (i, j))
    return pl.pallas_call(
        _out_kernel,
        grid=(n_lat // tm, d // tn),
        in_specs=[pl.BlockSpec((tm, d), lambda i, j: (i, 0)),
                  pl.BlockSpec((d, tn), lambda i, j: (0, j)),
                  tile,
                  pl.BlockSpec((1, tn), lambda i, j: (0, j))],
        out_specs=tile,
        out_shape=jax.ShapeDtypeStruct((n_lat, d), F32),
        compiler_params=_cparams(("arbitrary", "arbitrary")),
    )(mix, w_out, x1, gate)


def _pad_rows(a, rows):
    return jnp.concatenate([a, jnp.zeros((rows - a.shape[0],) + a.shape[1:], a.dtype)], axis=0)


def _pad_cols(a, cols):
    return jnp.concatenate([a, jnp.zeros(a.shape[:-1] + (cols - a.shape[-1],), a.dtype)], axis=-1)


def kernel(x, c, ctx, c_ctx, w_ada, b_ada, norm_w, ffn_w_gate, ffn_w_up, ffn_w_down, w_in, ml_conv_w, ml_conv_b, ml_gate_b, ml_norm_w, w_proj_ml, ssm_conv_w, ssm_conv_b, ssm_dt_bias, ssm_a_log, ssm_d, ssm_norm_w, w_proj_ssm, w_gate, b_gate, w_out, final_norm_w):
    depth = w_ada.shape[0]
    assert depth == 1 and x.shape[0] == 1, "single layer, single batch element"
    t, d = x.shape[1], x.shape[2]
    n_tot = t + ctx.shape[1]
    rows_cm = t // GRID_W
    ml_qk = ml_conv_w.shape[2] // 2
    ml_v = ml_norm_w.shape[1]
    n_ml = 2 * ml_qk + 2 * ml_v
    ml_cols = n_ml + 4 * ML_HEADS
    ssm_inner = ssm_norm_w.shape[1]
    ssm_heads = ssm_d.shape[1]
    ssm_xbc = ssm_conv_w.shape[2]

    cc = _pad_rows(jnp.concatenate([c, c_ctx[None, :]], axis=0), SUBLANES)
    mod = _mod_call(cc, w_ada[0], b_ada[0])
    ml, mc = mod[0].reshape(N_MOD, d), mod[1].reshape(N_MOD, d)
    nw = norm_w[0]
    zero = jnp.zeros_like(nw[0])
    vec1 = _pad_rows(jnp.stack([nw[0], ml[0], ml[1], ml[2], mc[0], mc[1], mc[2],
                                nw[1], ml[3], ml[4], zero, mc[3], mc[4]]), 16)
    vec2 = _pad_rows(jnp.stack([nw[2], ml[6], ml[7], ml[8], zero, zero, zero, final_norm_w]), 16)
    wg, wu, wd = ffn_w_gate, ffn_w_up, ffn_w_down.astype(BF16)

    xall = jnp.concatenate([x[0], ctx[0]], axis=0)
    x1, h = _ffn_call(xall, vec1, wg, wu, wd, 0, t, final=False)

    h_cm = jnp.concatenate(
        [h[:t].reshape(rows_cm, GRID_W, d).transpose(1, 0, 2).reshape(t, d), h[t:]], axis=0)
    wt = w_in[0].T
    n_ssm = ssm_inner + ssm_xbc
    ml_cw = _pad_rows(ml_conv_w[0], SUBLANES)
    ml_post = jnp.concatenate([jnp.full((1, ml_qk), (ml_qk // ML_HEADS) ** -0.5, F32),
                               jnp.ones((1, ml_qk), F32)], axis=1)
    assert 2 * ssm_heads == LANES
    qk = _mmconv_call(_unit_permute(h_cm), wt, 0, ml_cw, ml_conv_b[0][None, :], ml_post, t)
    vo, gates = _mm_call(h_cm, wt, 2 * ml_qk, 2 * ml_v, n_tot, BF16, side_row0=n_ml)
    z, dt = _mm_call(h, wt, ml_cols, ssm_inner, n_tot, BF16, side_row0=ml_cols + n_ssm)
    xbc = _mmconv_call(_unit_permute(h), wt, ml_cols + ssm_inner, _pad_rows(ssm_conv_w[0], SUBLANES),
                       ssm_conv_b[0][None, :], None, t)

    gate_b = _pad_cols(ml_gate_b[0].reshape(1, 4 * ML_HEADS), LANES)
    h_f = _mlstm_call(qk, vo, gates, gate_b, t, reverse=False)
    h_b = _mlstm_call(qk, vo, gates, gate_b, t, reverse=True)
    y_ml = _proj_call(_mlgate_call(h_f, h_b, vo, ml_norm_w[0][None, :], t), w_proj_ml[0].astype(BF16))
    y_ml = y_ml.reshape(GRID_W, rows_cm, d).transpose(1, 0, 2).reshape(t, d)

    dt_bias = ssm_dt_bias[0].reshape(1, 2 * ssm_heads)
    a_neg = -jnp.exp(ssm_a_log[0].astype(F32)).reshape(1, 2 * ssm_heads)
    d_skip = jnp.repeat(ssm_d[0], SSM_HEADDIM)[None, :]
    y_f = _ssd_call(xbc, dt, dt_bias, a_neg, d_skip, t, reverse=False)
    y_b = _ssd_call(xbc, dt, dt_bias, a_neg, d_skip, t, reverse=True)
    y_ssm = _proj_call(_ssmgate_call(y_f, y_b, z, ssm_norm_w[0][None, :], t), w_proj_ssm[0].astype(BF16))

    mix = _gate_call(h, w_gate[0].astype(BF16), b_gate[0][None, :], y_ml, y_ssm, t)
    x2 = _out_call(mix, w_out[0].astype(BF16), x1, ml[5][None, :], t)
    out = _ffn_call(x2, vec2, wg, wu, wd, 1, t, final=True)
    return out[None]
```

```python
import functools

import jax
import jax.numpy as jnp
from jax import lax
from jax.experimental import pallas as pl
from jax.experimental.pallas import tpu as pltpu

F32 = jnp.float32
BF16 = jnp.bfloat16

GRID_W = 64
N_MOD = 9
EPS = 1e-6
CONV_W = 5
CONV_PAD = CONV_W // 2
HALO = 16
CONV_UNIT = 256
ROW_BLK = 16
FFN_ROW_BLK = 32
ROW_LOOP_UNROLL = 4
FFN_PREFETCH_MAX_ROWS = 768
ML_HEADS = 8
ML_L = 256
SSM_HEADDIM = 64
SSM_GROUPS = 8
SSM_DSTATE = 128
SSM_L = 128
LANES = 128
SUBLANES = 8
VMEM_LIMIT_BYTES = 56 * 1024 * 1024
NEG_INF = float("-inf")
LOG2E = 1.4426950408889634


def _cparams(sem):
    return pltpu.CompilerParams(dimension_semantics=sem, vmem_limit_bytes=VMEM_LIMIT_BYTES)


def _pick(n, cands):
    for c in cands:
        if n % c == 0:
            return c
    raise ValueError(f"no tile for {n} in {cands}")


def _silu(x):
    return x * (1.0 / (1.0 + jnp.exp(-x)))


def _sigmoid(x):
    return 1.0 / (1.0 + jnp.exp(-x))


def _log_sigmoid(x):
    return jnp.minimum(x, 0.0) - jnp.log1p(jnp.exp(-jnp.abs(x)))


def _softplus(x):
    return jnp.maximum(x, 0.0) + jnp.log1p(jnp.exp(-jnp.abs(x)))


def _rms(x):
    return x * lax.rsqrt(jnp.mean(jnp.square(x), axis=-1, keepdims=True) + EPS)


def _dot(a, b):
    return jnp.dot(a, b, preferred_element_type=F32)


def _dot_nt(a, b):
    return lax.dot_general(a, b, (((1,), (1,)), ((), ())), preferred_element_type=F32)


def _dot_exact(a, b):
    return jnp.dot(a, b, preferred_element_type=F32, precision=lax.Precision.HIGHEST)


def _row_loop(n_rows, blk, body):
    def step(r, carry):
        body(pl.ds(pl.multiple_of(r * blk, blk), blk))
        return carry
    lax.fori_loop(0, n_rows // blk, step, 0, unroll=ROW_LOOP_UNROLL)


def _mod_kernel(c_ref, w_ref, b_ref, o_ref):
    a = _silu(c_ref[...]).astype(BF16)
    o_ref[...] = _dot(a, w_ref[...].astype(BF16)) + b_ref[...]


def _mod_call(cc, w_ada, b_ada):
    k, n = w_ada.shape
    tn = _pick(n, (1024, 512, 256, 128))
    return pl.pallas_call(
        _mod_kernel,
        grid=(n // tn,),
        in_specs=[pl.BlockSpec((SUBLANES, k), lambda j: (0, 0)),
                  pl.BlockSpec((k, tn), lambda j: (0, j)),
                  pl.BlockSpec((1, tn), lambda j: (0, j))],
        out_specs=pl.BlockSpec((SUBLANES, tn), lambda j: (0, j)),
        out_shape=jax.ShapeDtypeStruct((SUBLANES, n), F32),
        compiler_params=_cparams(("arbitrary",)),
    )(cc, w_ada, b_ada.reshape(1, n))


def _ffn_kernel(x_ref, vec_ref, wg_ref, wu_ref, wd_ref, *rest, n_lat, final):
    if final:
        o_ref, h_s, acc_s = rest
    else:
        x1_ref, h2_ref, h_s, acc_s = rest
    i = pl.program_id(0)
    f = pl.program_id(1)
    tm = x_ref.shape[0]

    def vec(row, rows):
        off = jnp.where(i * tm + rows.start >= n_lat, 3, 0)
        return vec_ref[pl.ds(row + off, 1), :]

    @pl.when(f == 0)
    def _():
        def body(rows):
            h = _rms(x_ref[rows, :]) * (vec_ref[0:1, :] * (1.0 + vec(2, rows))) + vec(1, rows)
            h_s[rows, :] = h.astype(BF16)
            acc_s[rows, :] = jnp.zeros((FFN_ROW_BLK, acc_s.shape[1]), F32)
        _row_loop(tm, FFN_ROW_BLK, body)

    h = h_s[...]
    a = _silu(_dot(h, wg_ref[...].astype(BF16))) * _dot(h, wu_ref[...].astype(BF16))
    acc_s[...] += _dot(a.astype(BF16), wd_ref[...])

    @pl.when(f == pl.num_programs(1) - 1)
    def _():
        def body(rows):
            x1 = x_ref[rows, :] + (0.5 * vec(3, rows)) * acc_s[rows, :]
            if final:
                o_ref[rows, :] = _rms(x1) * vec_ref[7:8, :]
            else:
                x1_ref[rows, :] = x1
                h2 = _rms(x1) * (vec_ref[7:8, :] * (1.0 + vec(9, rows))) + vec(8, rows)
                h2_ref[rows, :] = h2.astype(BF16)
        _row_loop(tm, FFN_ROW_BLK, body)


def _ffn_call(x, vecs, wg, wu, wd, which, n_lat, final):
    m, d = x.shape
    dff = wg.shape[3]
    tm = _pick(m, (1024, 768, 512, 256))
    tf = _pick(dff, (512, 256, 128))
    row = pl.BlockSpec((tm, d), lambda i, f: (i, 0), pipeline_mode=pl.Buffered(1))
    x_row = pl.BlockSpec((tm, d), lambda i, f: (i, 0)) if tm <= FFN_PREFETCH_MAX_ROWS else row
    if final:
        out_shape = jax.ShapeDtypeStruct((m, d), F32)
        out_specs = row
    else:
        out_shape = (jax.ShapeDtypeStruct((m, d), F32), jax.ShapeDtypeStruct((m, d), BF16))
        out_specs = (row, row)
    return pl.pallas_call(
        functools.partial(_ffn_kernel, n_lat=n_lat, final=final),
        grid=(m // tm, dff // tf),
        in_specs=[x_row,
                  pl.BlockSpec(vecs.shape, lambda i, f: (0, 0)),
                  pl.BlockSpec((None, None, d, tf), lambda i, f: (0, which, 0, f)),
                  pl.BlockSpec((None, None, d, tf), lambda i, f: (0, which, 0, f)),
                  pl.BlockSpec((None, None, tf, d), lambda i, f: (0, which, f, 0))],
        out_specs=out_specs,
        out_shape=out_shape,
        scratch_shapes=[pltpu.VMEM((tm, d), BF16), pltpu.VMEM((tm, d), F32)],
        compiler_params=_cparams(("arbitrary", "arbitrary")),
    )(x, vecs, wg, wu, wd)


def _mm_kernel(a_ref, w_ref, *rest):
    if len(rest) == 3:
        w2_ref, o_ref, o2_ref = rest

        @pl.when(pl.program_id(1) == 0)
        def _():
            o2_ref[...] = _dot_nt(a_ref[...], w2_ref[...].astype(BF16))
    else:
        (o_ref,) = rest
    o_ref[...] = _dot_nt(a_ref[...], w_ref[...].astype(BF16)).astype(o_ref.dtype)


def _mm_call(a, wt, row0, n, m, out_dtype, side_row0=None):
    k = a.shape[1]
    tm = _pick(m, (1408, 1024, 768, 512, 256))
    tn = _pick(n, (1024, 512, 256, 128))
    assert row0 % SUBLANES == 0
    wspec = lambda rows, idx: pl.BlockSpec((pl.Element(rows), pl.Element(k)), idx)
    in_specs = [pl.BlockSpec((tm, k), lambda i, j: (i, 0)),
                wspec(tn, lambda i, j: (pl.multiple_of(row0 + j * tn, SUBLANES), 0))]
    out_specs = pl.BlockSpec((tm, tn), lambda i, j: (i, j))
    out_shape = jax.ShapeDtypeStruct((m, n), out_dtype)
    args = (a, wt)
    if side_row0 is not None:
        assert side_row0 % SUBLANES == 0
        in_specs.append(wspec(LANES, lambda i, j: (side_row0, 0)))
        out_specs = (out_specs, pl.BlockSpec((tm, LANES), lambda i, j: (i, 0)))
        out_shape = (out_shape, jax.ShapeDtypeStruct((m, LANES), F32))
        args = (a, wt, wt)
    return pl.pallas_call(
        _mm_kernel,
        grid=(m // tm, n // tn),
        in_specs=in_specs,
        out_specs=out_specs,
        out_shape=out_shape,
        compiler_params=_cparams(("arbitrary", "arbitrary")),
    )(*args)


def _mmconv_kernel(a_ref, ap_ref, an_ref, w_ref, cw_ref, cb_ref, *rest, n_lat, n_tot, scaled):
    if scaled:
        post_ref, o_ref, xs, ys, a_s = rest
    else:
        o_ref, xs, ys, a_s = rest
    i = pl.program_id(0)
    tm = a_ref.shape[0]
    tn = o_ref.shape[1]
    units = tm // CONV_UNIT
    ug = CONV_UNIT // SUBLANES
    pad = CONV_PAD * SUBLANES
    span = CONV_UNIT + 2 * pad
    seq_units = n_lat // CONV_UNIT
    last_unit = n_tot // CONV_UNIT - 1

    @pl.when(pl.program_id(1) == 0)
    def _():
        a_s[0:HALO, :] = ap_ref[...]
        a_s[HALO:HALO + tm, :] = a_ref[...]
        a_s[HALO + tm:2 * HALO + tm, :] = an_ref[...]

    w = w_ref[...].astype(BF16)
    p_all = _dot_nt(a_s[...], w)
    p_prev = p_all[0:HALO]
    p = p_all[HALO:HALO + tm]
    p_next = p_all[HALO + tm:2 * HALO + tm]
    sub = lax.broadcasted_iota(jnp.int32, (SUBLANES, 1), 0)

    def group(x, k):
        return x[k * SUBLANES:(k + 1) * SUBLANES]

    for u in range(units):
        g = i * units + u
        keep_prev = jnp.where((g == 0) | (g == seq_units), 0.0, 1.0)
        keep_next = jnp.where((g == last_unit) | (g == seq_units - 1), 0.0, 1.0)
        r0 = u * CONV_UNIT
        base = u * span
        pu = p[r0:r0 + CONV_UNIT]
        xs[base + pad:base + pad + CONV_UNIT, :] = pu
        before = p[r0 - pad:r0] if u > 0 else p_prev
        after = p[r0 + CONV_UNIT:r0 + CONV_UNIT + pad] if u < units - 1 else p_next
        for d in range(CONV_PAD):
            own = pltpu.roll(group(pu, ug - CONV_PAD + d), 1, 0)
            other = pltpu.roll(group(before, d), 1, 0) * keep_prev
            xs[base + d * SUBLANES:base + (d + 1) * SUBLANES, :] = jnp.where(sub == 0, other, own)
            own = pltpu.roll(group(pu, d), SUBLANES - 1, 0)
            other = pltpu.roll(group(after, d), SUBLANES - 1, 0) * keep_next
            lo = base + pad + CONV_UNIT + d * SUBLANES
            xs[lo:lo + SUBLANES, :] = jnp.where(sub == SUBLANES - 1, other, own)

        acc = cb_ref[...]
        for j in range(CONV_W):
            acc = acc + cw_ref[j:j + 1, :] * xs[base + SUBLANES * j:base + SUBLANES * j + CONV_UNIT, :]
        y = _silu(acc)
        if scaled:
            y = y * post_ref[...]
        for cblk in range(tn // LANES):
            lanes = slice(cblk * LANES, (cblk + 1) * LANES)
            ys[cblk, r0:r0 + CONV_UNIT, :] = y[:, lanes]
            for b in range(SUBLANES):
                o_ref[r0 + b * ug:r0 + (b + 1) * ug, lanes] = (
                    ys[cblk, pl.ds(r0 + b, ug, stride=SUBLANES), :].astype(o_ref.dtype))


def _mmconv_call(a, w, col0, conv_w, conv_b, post, n_lat):
    m, k = a.shape
    n = conv_w.shape[1]
    tm = _pick(m, (768, 512, 256))
    tn = _pick(n, (1024, 512, 256, 128))
    assert col0 % SUBLANES == 0 and tm % CONV_UNIT == 0 and n_lat % CONV_UNIT == 0 and m % CONV_UNIT == 0
    assert HALO == CONV_PAD * SUBLANES
    hb = tm // HALO
    last_halo = m // HALO - 1
    vec = pl.BlockSpec((1, tn), lambda i, j: (0, j))
    scaled = post is not None
    return pl.pallas_call(
        functools.partial(_mmconv_kernel, n_lat=n_lat, n_tot=m, scaled=scaled),
        grid=(m // tm, n // tn),
        in_specs=[pl.BlockSpec((tm, k), lambda i, j: (i, 0)),
                  pl.BlockSpec((HALO, k), lambda i, j: (jnp.maximum(i * hb - 1, 0), 0)),
                  pl.BlockSpec((HALO, k), lambda i, j: (jnp.minimum((i + 1) * hb, last_halo), 0)),
                  pl.BlockSpec((pl.Element(tn), pl.Element(k)),
                               lambda i, j: (pl.multiple_of(col0 + j * tn, SUBLANES), 0)),
                  pl.BlockSpec((SUBLANES, tn), lambda i, j: (0, j)),
                  vec] + ([vec] if scaled else []),
        out_specs=pl.BlockSpec((tm, tn), lambda i, j: (i, j)),
        out_shape=jax.ShapeDtypeStruct((m, n), BF16),
        scratch_shapes=[pltpu.VMEM(((tm // CONV_UNIT) * (CONV_UNIT + 2 * HALO), tn), F32),
                        pltpu.VMEM((tn // LANES, tm, LANES), F32),
                        pltpu.VMEM((tm + 2 * HALO, k), BF16)],
        compiler_params=_cparams(("arbitrary", "arbitrary")),
    )(a, a, a, w, conv_w, conv_b, *([post] if scaled else []))


def _unit_permute(x):
    rows, d = x.shape
    ug = CONV_UNIT // SUBLANES
    return x.reshape(rows // CONV_UNIT, SUBLANES, ug, d).transpose(0, 2, 1, 3).reshape(rows, d)


def _chunk_order(c, n_lat_chunks, n_ctx_chunks, reverse):
    if reverse:
        return n_lat_chunks + n_ctx_chunks - 1 - c
    return jnp.where(c < n_ctx_chunks, n_lat_chunks + c, c - n_ctx_chunks)


def _tri(chunk, reverse):
    r = lax.broadcasted_iota(jnp.int32, (chunk, chunk), 0)
    c = lax.broadcasted_iota(jnp.int32, (chunk, chunk), 1)
    return (c >= r) if reverse else (c <= r)


def _mlstm_kernel(q_ref, k_ref, v_ref, g_ref, gn_ref, gb_ref, o_ref, ct_s, n_s, m_s, bc_s, bct_s, gt_s,
                  bcn_s, bctn_s, gtn_s, *, reverse):
    L = q_ref.shape[0]
    dk = q_ref.shape[1] // ML_HEADS
    dv = v_ref.shape[1] // ML_HEADS
    c = pl.program_id(0)

    mask = _tri(L, reverse)

    def prefix(g_blk, bc_d, bct_d, gt_d):
        g = g_blk + gb_ref[...]
        bc = _dot_exact(mask.astype(F32), _log_sigmoid(g))
        bc_d[...] = bc
        bct_d[...] = bc.T
        gt_d[...] = g.T

    @pl.when(c == 0)
    def _():
        ct_s[...] = jnp.zeros_like(ct_s)
        n_s[...] = jnp.zeros_like(n_s)
        m_s[...] = jnp.zeros_like(m_s)
        prefix(g_ref[...], bc_s, bct_s, gt_s)

    prefix(gn_ref[...], bcn_s, bctn_s, gtn_s)

    def lanes(x, n):
        return jnp.concatenate([x] * (n // LANES), axis=1)

    bc = bc_s[...]
    bc_t = bct_s[...]
    g_t = gt_s[...]
    end = 0 if reverse else L - 1
    base = 2 * ML_HEADS if reverse else 0
    ones = jnp.ones((L, LANES), BF16)

    for h in range(ML_HEADS):
        io = base + h
        fo = base + ML_HEADS + h
        bb = jnp.broadcast_to(bc[:, fo:fo + 1], (L, LANES))
        b_row = bc_t[fo:fo + 1, :]
        ig_row = g_t[io:io + 1, :]
        b_end = bc[end:end + 1, fo:fo + 1]
        m0 = m_s[h:h + 1, :]
        q = q_ref[:, h * dk:(h + 1) * dk]
        k = k_ref[:, h * dk:(h + 1) * dk]
        v = v_ref[:, h * dv:(h + 1) * dv]
        ct0 = ct_s[h]
        n0 = n_s[h]

        logw = jnp.where(mask, lanes(bb, L) + (ig_row - b_row), NEG_INF)
        m_inter = bb + m0
        m_intra = jnp.broadcast_to(jnp.max(logw, axis=1, keepdims=True), (L, LANES))
        m_out = jnp.maximum(m_inter, m_intra)
        p = (_dot_nt(q, k) * jnp.exp(logw - lanes(m_out, L))).astype(BF16)
        s_inter = jnp.exp(m_inter - m_out)
        num = _dot(p, v) + lanes(s_inter, dv) * _dot(q, ct0.astype(BF16))
        den = _dot(p, ones) + s_inter * _dot(q, n0.astype(BF16))
        inv = 1.0 / jnp.maximum(jnp.abs(den), jnp.exp(-m_out))
        o_ref[:, h * dv:(h + 1) * dv] = (num * lanes(inv, dv)).astype(o_ref.dtype)

        a_row = b_end + ig_row - b_row
        m_loc = jnp.max(a_row, axis=1, keepdims=True)
        kw_t = (k.T.astype(F32) * jnp.exp(a_row - m_loc)).astype(BF16)
        m_new = jnp.maximum(b_end + m0, m_loc)
        s_old = jnp.exp(b_end + m0 - m_new)
        s_new = jnp.exp(m_loc - m_new)
        ct_s[h] = lanes(s_old, dv) * ct0 + lanes(s_new, dv) * _dot(kw_t, v)
        n_s[h] = s_old * n0 + s_new * _dot(kw_t, ones)
        m_s[h:h + 1, :] = m_new

    bc_s[...] = bcn_s[...]
    bct_s[...] = bctn_s[...]
    gt_s[...] = gtn_s[...]


def _mlstm_call(qk, vo, gates, gate_b, n_lat, reverse):
    rows = qk.shape[0]
    L = ML_L
    nqk = qk.shape[1] // 2
    nv = vo.shape[1] // 2
    dk = nqk // ML_HEADS
    dv = nv // ML_HEADS
    nlc = n_lat // L
    ncc = (rows - n_lat) // L
    order = functools.partial(_chunk_order, n_lat_chunks=nlc, n_ctx_chunks=ncc, reverse=reverse)
    return pl.pallas_call(
        functools.partial(_mlstm_kernel, reverse=reverse),
        grid=(nlc + ncc,),
        in_specs=[pl.BlockSpec((L, nqk), lambda c: (order(c), 0)),
                  pl.BlockSpec((L, nqk), lambda c: (order(c), 1)),
                  pl.BlockSpec((L, nv), lambda c: (order(c), 0)),
                  pl.BlockSpec((L, LANES), lambda c: (order(c), 0)),
                  pl.BlockSpec((L, LANES), lambda c: (order(jnp.minimum(c + 1, nlc + ncc - 1)), 0)),
                  pl.BlockSpec((1, LANES), lambda c: (0, 0))],
        out_specs=pl.BlockSpec((L, nv), lambda c: (order(c), 0)),
        out_shape=jax.ShapeDtypeStruct((rows, nv), BF16),
        scratch_shapes=[pltpu.VMEM((ML_HEADS, dk, dv), F32),
                        pltpu.VMEM((ML_HEADS, dk, LANES), F32),
                        pltpu.VMEM((ML_HEADS, LANES), F32)]
        + 2 * [pltpu.VMEM((L, LANES), F32), pltpu.VMEM((LANES, L), F32), pltpu.VMEM((LANES, L), F32)],
        compiler_params=_cparams(("arbitrary",)),
    )(qk, qk, vo, gates, gates, gate_b)


def _ssd_kernel(x_ref, bm_ref, cm_ref, dt_ref, dtn_ref, dtb_ref, a_ref, dsk_ref, y_ref,
                st_s, bc_s, bct_s, dtt_s, bcn_s, bctn_s, dttn_s, *, reverse):
    L = x_ref.shape[0]
    nx = x_ref.shape[1]
    c = pl.program_id(0)
    mask = _tri(L, reverse)

    def prefix(dt_blk, bc_d, bct_d, dtt_d):
        dtv = _softplus(dt_blk + dtb_ref[...])
        bc = _dot_exact(mask.astype(F32), dtv * a_ref[...]) * LOG2E
        bc_d[...] = bc
        bct_d[...] = bc.T
        dtt_d[...] = jnp.log2(dtv).T

    @pl.when(c == 0)
    def _():
        st_s[...] = jnp.zeros_like(st_s)
        prefix(dt_ref[...], bc_s, bct_s, dtt_s)

    prefix(dtn_ref[...], bcn_s, bctn_s, dttn_s)

    bc = bc_s[...]
    bc_t = bct_s[...]
    ldt_t = dtt_s[...]
    end = 0 if reverse else L - 1
    nheads = nx // SSM_HEADDIM
    hpg = nheads // SSM_GROUPS
    base = nheads if reverse else 0
    pw = 2 * SSM_HEADDIM
    lo = lax.broadcasted_iota(jnp.int32, (1, pw), 1) < SSM_HEADDIM
    lo_b = jnp.where(lo, 1.0, 0.0).astype(BF16)
    hi_b = jnp.where(lo, 0.0, 1.0).astype(BF16)

    for g in range(SSM_GROUPS):
        bm = bm_ref[:, g * SSM_DSTATE:(g + 1) * SSM_DSTATE]
        cm = cm_ref[:, g * SSM_DSTATE:(g + 1) * SSM_DSTATE]
        cb = _dot_nt(cm, bm)
        bm_t = bm.astype(F32).T
        for pr in range(hpg // 2):
            col = (g * hpg + 2 * pr) * SSM_HEADDIM
            xp = x_ref[:, col:col + pw]
            x_half = (xp * lo_b, xp * hi_b)
            s0 = st_s[:, col:col + pw]
            y = None
            st = None
            bbs, decs = [], []
            for e in range(2):
                co = base + g * hpg + 2 * pr + e
                bb = jnp.broadcast_to(bc[:, co:co + 1], (L, L))
                r_row = bc_t[co:co + 1, :] - ldt_t[co:co + 1, :]
                b_end = bc[end:end + 1, co:co + 1]
                m = (cb * jnp.exp2(jnp.where(mask, bb - r_row, NEG_INF))).astype(BF16)
                ye = _dot(m, x_half[e])
                y = ye if y is None else y + ye
                bt = (bm_t * jnp.exp2(b_end - r_row)).astype(BF16)
                se = _dot(bt, x_half[e])
                st = se if st is None else st + se
                bbs.append(bb[:, 0:pw])
                decs.append(jnp.exp2(b_end))
            y = y + jnp.exp2(jnp.where(lo, bbs[0], bbs[1])) * _dot(cm, s0.astype(BF16))
            if not reverse:
                y = y + dsk_ref[:, col:col + pw] * xp.astype(F32)
            y_ref[:, col:col + pw] = y.astype(y_ref.dtype)
            st_s[:, col:col + pw] = s0 * jnp.where(lo, decs[0], decs[1]) + st

    bc_s[...] = bcn_s[...]
    bct_s[...] = bctn_s[...]
    dtt_s[...] = dttn_s[...]


def _ssd_call(xbc, dt, dt_bias, a_neg, d_skip, n_lat, reverse):
    rows = xbc.shape[0]
    L = SSM_L
    assert L >= 2 * SSM_HEADDIM
    nb = SSM_GROUPS * SSM_DSTATE
    nx = xbc.shape[1] - 2 * nb
    nlc = n_lat // L
    ncc = (rows - n_lat) // L
    order = functools.partial(_chunk_order, n_lat_chunks=nlc, n_ctx_chunks=ncc, reverse=reverse)
    xblk = nx // nb
    nc = nlc + ncc
    return pl.pallas_call(
        functools.partial(_ssd_kernel, reverse=reverse),
        grid=(nc,),
        in_specs=[pl.BlockSpec((L, nx), lambda c: (order(c), 0)),
                  pl.BlockSpec((L, nb), lambda c: (order(c), xblk)),
                  pl.BlockSpec((L, nb), lambda c: (order(c), xblk + 1)),
                  pl.BlockSpec((L, LANES), lambda c: (order(c), 0)),
                  pl.BlockSpec((L, LANES), lambda c: (order(jnp.minimum(c + 1, nc - 1)), 0)),
                  pl.BlockSpec((1, LANES), lambda c: (0, 0)),
                  pl.BlockSpec((1, LANES), lambda c: (0, 0)),
                  pl.BlockSpec((1, nx), lambda c: (0, 0))],
        out_specs=pl.BlockSpec((L, nx), lambda c: (order(c), 0)),
        out_shape=jax.ShapeDtypeStruct((rows, nx), BF16),
        scratch_shapes=[pltpu.VMEM((SSM_DSTATE, nx), F32)]
        + 2 * [pltpu.VMEM((L, LANES), F32), pltpu.VMEM((LANES, L), F32), pltpu.VMEM((LANES, L), F32)],
        compiler_params=_cparams(("arbitrary",)),
    )(xbc, xbc, xbc, dt, dt, dt_bias, a_neg, d_skip)


def _mlgate_kernel(hf_ref, hb_ref, o_ref, nw_ref, a_ref):
    dv = a_ref.shape[1] // ML_HEADS

    def body(rows):
        for h in range(ML_HEADS):
            sl = slice(h * dv, (h + 1) * dv)
            hh = _rms(hf_ref[rows, sl].astype(F32) + hb_ref[rows, sl].astype(F32)) * nw_ref[:, sl]
            a_ref[rows, sl] = (_sigmoid(o_ref[rows, sl].astype(F32)) * hh).astype(BF16)
    _row_loop(a_ref.shape[0], ROW_BLK, body)


def _mlgate_call(hf, hb, vo, norm_w, n_lat):
    nv = hf.shape[1]
    tm = _pick(n_lat, (512, 256))
    row = pl.BlockSpec((tm, nv), lambda i: (i, 0))
    return pl.pallas_call(
        _mlgate_kernel,
        grid=(n_lat // tm,),
        in_specs=[row, row, pl.BlockSpec((tm, nv), lambda i: (i, 1)), pl.BlockSpec((1, nv), lambda i: (0, 0))],
        out_specs=row,
        out_shape=jax.ShapeDtypeStruct((n_lat, nv), BF16),
        compiler_params=_cparams(("arbitrary",)),
    )(hf, hb, vo, norm_w)


def _ssmgate_kernel(yf_ref, yb_ref, z_ref, nw_ref, a_ref):
    gw = a_ref.shape[1] // SSM_GROUPS

    def body(rows):
        for g in range(SSM_GROUPS):
            sl = slice(g * gw, (g + 1) * gw)
            y = (yf_ref[rows, sl].astype(F32) + yb_ref[rows, sl].astype(F32)) * _silu(z_ref[rows, sl].astype(F32))
            a_ref[rows, sl] = (_rms(y) * nw_ref[:, sl]).astype(BF16)
    _row_loop(a_ref.shape[0], ROW_BLK, body)


def _ssmgate_call(yf, yb, z, norm_w, n_lat):
    nx = yf.shape[1]
    tm = _pick(n_lat, (512, 256))
    row = pl.BlockSpec((tm, nx), lambda i: (i, 0))
    return pl.pallas_call(
        _ssmgate_kernel,
        grid=(n_lat // tm,),
        in_specs=[row, row, row, pl.BlockSpec((1, nx), lambda i: (0, 0))],
        out_specs=row,
        out_shape=jax.ShapeDtypeStruct((n_lat, nx), BF16),
        compiler_params=_cparams(("arbitrary",)),
    )(yf, yb, z, norm_w)


def _proj_kernel(a_ref, w_ref, y_ref):
    y_ref[...] = _dot(a_ref[...], w_ref[...]).astype(y_ref.dtype)


def _proj_call(a, w):
    m, k = a.shape
    n = w.shape[1]
    tm = _pick(m, (1024, 512, 256))
    tn = _pick(n, (1024, 512, 256, 128))
    return pl.pallas_call(
        _proj_kernel,
        grid=(m // tm, n // tn),
        in_specs=[pl.BlockSpec((tm, k), lambda i, j: (i, 0)),
                  pl.BlockSpec((k, tn), lambda i, j: (0, j))],
        out_specs=pl.BlockSpec((tm, tn), lambda i, j: (i, j)),
        out_shape=jax.ShapeDtypeStruct((m, n), BF16),
        compiler_params=_cparams(("arbitrary", "arbitrary")),
    )(a, w)


def _gate_kernel(h_ref, w1_ref, w2_ref, b1_ref, b2_ref, yml_ref, yssm_ref, o_ref):
    h = h_ref[...]
    g1 = _sigmoid(_dot(h, w1_ref[...]) + b1_ref[...])
    g2 = _sigmoid(_dot(h, w2_ref[...]) + b2_ref[...])
    o_ref[...] = (g1 * yml_ref[...].astype(F32) + g2 * yssm_ref[...].astype(F32)).astype(o_ref.dtype)


def _gate_call(h, w_gate, b_gate, y_ml, y_ssm, n_lat):
    d = h.shape[1]
    tm = _pick(n_lat, (1024, 512, 256))
    tn = _pick(d, (1024, 512, 256, 128))
    nj = d // tn
    tile = pl.BlockSpec((tm, tn), lambda i, j: (i, j))
    return pl.pallas_call(
        _gate_kernel,
        grid=(n_lat // tm, nj),
        in_specs=[pl.BlockSpec((tm, d), lambda i, j: (i, 0)),
                  pl.BlockSpec((d, tn), lambda i, j: (0, j)),
                  pl.BlockSpec((d, tn), lambda i, j: (0, nj + j)),
                  pl.BlockSpec((1, tn), lambda i, j: (0, j)),
                  pl.BlockSpec((1, tn), lambda i, j: (0, nj + j)),
                  tile, tile],
        out_specs=tile,
        out_shape=jax.ShapeDtypeStruct((n_lat, d), BF16),
        compiler_params=_cparams(("arbitrary", "arbitrary")),
    )(h, w_gate, w_gate, b_gate, b_gate, y_ml, y_ssm)


def _out_kernel(a_ref, w_ref, x_ref, g_ref, o_ref):
    o_ref[...] = x_ref[...] + g_ref[...] * _dot(a_ref[...], w_ref[...])


def _out_call(mix, w_out, x1, gate, n_lat):
    d = mix.shape[1]
    tm = _pick(n_lat, (1024, 512, 256))
    tn = _pick(d, (1024, 512, 256, 128))
    tile = pl.BlockSpec((tm, tn), lambda i, j: (i, j))
    return pl.pallas_call(
        _out_kernel,
        grid=(n_lat // tm, d // tn),
        in_specs=[pl.BlockSpec((tm, d), lambda i, j: (i, 0)),
                  pl.BlockSpec((d, tn), lambda i, j: (0, j)),
                  tile,
                  pl.BlockSpec((1, tn), lambda i, j: (0, j))],
        out_specs=tile,
        out_shape=jax.ShapeDtypeStruct((n_lat, d), F32),
        compiler_params=_cparams(("arbitrary", "arbitrary")),
    )(mix, w_out, x1, gate)


def _pad_rows(a, rows):
    return jnp.concatenate([a, jnp.zeros((rows - a.shape[0],) + a.shape[1:], a.dtype)], axis=0)


def _pad_cols(a, cols):
    return jnp.concatenate([a, jnp.zeros(a.shape[:-1] + (cols - a.shape[-1],), a.dtype)], axis=-1)


def kernel(x, c, ctx, c_ctx, w_ada, b_ada, norm_w, ffn_w_gate, ffn_w_up, ffn_w_down, w_in, ml_conv_w, ml_conv_b, ml_gate_b, ml_norm_w, w_proj_ml, ssm_conv_w, ssm_conv_b, ssm_dt_bias, ssm_a_log, ssm_d, ssm_norm_w, w_proj_ssm, w_gate, b_gate, w_out, final_norm_w):
    depth = w_ada.shape[0]
    assert depth == 1 and x.shape[0] == 1, "single layer, single batch element"
    t, d = x.shape[1], x.shape[2]
    n_tot = t + ctx.shape[1]
    rows_cm = t // GRID_W
    ml_qk = ml_conv_w.shape[2] // 2
    ml_v = ml_norm_w.shape[1]
    n_ml = 2 * ml_qk + 2 * ml_v
    ml_cols = n_ml + 4 * ML_HEADS
    ssm_inner = ssm_norm_w.shape[1]
    ssm_heads = ssm_d.shape[1]
    ssm_xbc = ssm_conv_w.shape[2]

    cc = _pad_rows(jnp.concatenate([c, c_ctx[None, :]], axis=0), SUBLANES)
    mod = _mod_call(cc, w_ada[0], b_ada[0])
    ml, mc = mod[0].reshape(N_MOD, d), mod[1].reshape(N_MOD, d)
    nw = norm_w[0]
    zero = jnp.zeros_like(nw[0])
    vec1 = _pad_rows(jnp.stack([nw[0], ml[0], ml[1], ml[2], mc[0], mc[1], mc[2],
                                nw[1], ml[3], ml[4], zero, mc[3], mc[4]]), 16)
    vec2 = _pad_rows(jnp.stack([nw[2], ml[6], ml[7], ml[8], zero, zero, zero, final_norm_w]), 16)
    wg, wu, wd = ffn_w_gate, ffn_w_up, ffn_w_down.astype(BF16)

    xall = jnp.concatenate([x[0], ctx[0]], axis=0)
    x1, h = _ffn_call(xall, vec1, wg, wu, wd, 0, t, final=False)

    h_cm = jnp.concatenate(
        [h[:t].reshape(rows_cm, GRID_W, d).transpose(1, 0, 2).reshape(t, d), h[t:]], axis=0)
    wt = w_in[0].T
    n_ssm = ssm_inner + ssm_xbc
    ml_cw = _pad_rows(ml_conv_w[0], SUBLANES)
    ml_post = jnp.concatenate([jnp.full((1, ml_qk), (ml_qk // ML_HEADS) ** -0.5, F32),
                               jnp.ones((1, ml_qk), F32)], axis=1)
    assert 2 * ssm_heads == LANES
    qk = _mmconv_call(_unit_permute(h_cm), wt, 0, ml_cw, ml_conv_b[0][None, :], ml_post, t)
    vo, gates = _mm_call(h_cm, wt, 2 * ml_qk, 2 * ml_v, n_tot, BF16, side_row0=n_ml)
    z, dt = _mm_call(h, wt, ml_cols, ssm_inner, n_tot, BF16, side_row0=ml_cols + n_ssm)
    xbc = _mmconv_call(_unit_permute(h), wt, ml_cols + ssm_inner, _pad_rows(ssm_conv_w[0], SUBLANES),
                       ssm_conv_b[0][None, :], None, t)

    gate_b = _pad_cols(ml_gate_b[0].reshape(1, 4 * ML_HEADS), LANES)
    h_f = _mlstm_call(qk, vo, gates, gate_b, t, reverse=False)
    h_b = _mlstm_call(qk, vo, gates, gate_b, t, reverse=True)
    y_ml = _proj_call(_mlgate_call(h_f, h_b, vo, ml_norm_w[0][None, :], t), w_proj_ml[0].astype(BF16))
    y_ml = y_ml.reshape(GRID_W, rows_cm, d).transpose(1, 0, 2).reshape(t, d)

    dt_bias = ssm_dt_bias[0].reshape(1, 2 * ssm_heads)
    a_neg = -jnp.exp(ssm_a_log[0].astype(F32)).reshape(1, 2 * ssm_heads)
    d_skip = jnp.repeat(ssm_d[0], SSM_HEADDIM)[None, :]
    y_f = _ssd_call(xbc, dt, dt_bias, a_neg, d_skip, t, reverse=False)
    y_b = _ssd_call(xbc, dt, dt_bias, a_neg, d_skip, t, reverse=True)
    y_ssm = _proj_call(_ssmgate_call(y_f, y_b, z, ssm_norm_w[0][None, :], t), w_proj_ssm[0].astype(BF16))

    mix = _gate_call(h, w_gate[0].astype(BF16), b_gate[0][None, :], y_ml, y_ssm, t)
    x2 = _out_call(mix, w_out[0].astype(BF16), x1, ml[5][None, :], t)
    out = _ffn_call(x2, vec2, wg, wu, wd, 1, t, final=True)
    return out[None]
```

```python
import functools

import jax
import jax.numpy as jnp
from jax import lax
from jax.experimental import pallas as pl
from jax.experimental.pallas import tpu as pltpu

F32 = jnp.float32
BF16 = jnp.bfloat16

GRID_W = 64
N_MOD = 9
EPS = 1e-6
CONV_W = 5
CONV_PAD = CONV_W // 2
HALO = 16
CONV_UNIT = 256
ROW_BLK = 16
FFN_ROW_BLK = 32
ROW_LOOP_UNROLL = 4
FFN_PREFETCH_MAX_ROWS = 768
ML_HEADS = 8
ML_L = 256
SSM_HEADDIM = 64
SSM_GROUPS = 8
SSM_DSTATE = 128
SSM_L = 128
SSM_STEP_CHUNKS = 2
LANES = 128
SUBLANES = 8
VMEM_LIMIT_BYTES = 56 * 1024 * 1024
NEG_INF = float("-inf")
LOG2E = 1.4426950408889634


def _cparams(sem):
    return pltpu.CompilerParams(dimension_semantics=sem, vmem_limit_bytes=VMEM_LIMIT_BYTES)


def _pick(n, cands):
    for c in cands:
        if n % c == 0:
            return c
    raise ValueError(f"no tile for {n} in {cands}")


def _silu(x):
    return x * (1.0 / (1.0 + jnp.exp(-x)))


def _sigmoid(x):
    return 1.0 / (1.0 + jnp.exp(-x))


def _log_sigmoid(x):
    return jnp.minimum(x, 0.0) - jnp.log1p(jnp.exp(-jnp.abs(x)))


def _softplus(x):
    return jnp.maximum(x, 0.0) + jnp.log1p(jnp.exp(-jnp.abs(x)))


def _rms(x):
    return x * lax.rsqrt(jnp.mean(jnp.square(x), axis=-1, keepdims=True) + EPS)


def _dot(a, b):
    return jnp.dot(a, b, preferred_element_type=F32)


def _dot_nt(a, b):
    return lax.dot_general(a, b, (((1,), (1,)), ((), ())), preferred_element_type=F32)


def _dot_exact(a, b):
    return jnp.dot(a, b, preferred_element_type=F32, precision=lax.Precision.HIGHEST)


def _row_loop(n_rows, blk, body):
    def step(r, carry):
        body(pl.ds(pl.multiple_of(r * blk, blk), blk))
        return carry
    lax.fori_loop(0, n_rows // blk, step, 0, unroll=ROW_LOOP_UNROLL)


def _mod_kernel(c_ref, w_ref, b_ref, o_ref):
    a = _silu(c_ref[...]).astype(BF16)
    o_ref[...] = _dot(a, w_ref[...].astype(BF16)) + b_ref[...]


def _mod_call(cc, w_ada, b_ada):
    k, n = w_ada.shape
    tn = _pick(n, (1024, 512, 256, 128))
    return pl.pallas_call(
        _mod_kernel,
        grid=(n // tn,),
        in_specs=[pl.BlockSpec((SUBLANES, k), lambda j: (0, 0)),
                  pl.BlockSpec((k, tn), lambda j: (0, j)),
                  pl.BlockSpec((1, tn), lambda j: (0, j))],
        out_specs=pl.BlockSpec((SUBLANES, tn), lambda j: (0, j)),
        out_shape=jax.ShapeDtypeStruct((SUBLANES, n), F32),
        compiler_params=_cparams(("arbitrary",)),
    )(cc, w_ada, b_ada.reshape(1, n))


def _ffn_kernel(x_ref, vec_ref, wg_ref, wu_ref, wd_ref, *rest, n_lat, final):
    if final:
        o_ref, h_s, acc_s = rest
    else:
        x1_ref, h2_ref, h_s, acc_s = rest
    i = pl.program_id(0)
    f = pl.program_id(1)
    tm = x_ref.shape[0]

    def vec(row, rows):
        off = jnp.where(i * tm + rows.start >= n_lat, 3, 0)
        return vec_ref[pl.ds(row + off, 1), :]

    @pl.when(f == 0)
    def _():
        def body(rows):
            h = _rms(x_ref[rows, :]) * (vec_ref[0:1, :] * (1.0 + vec(2, rows))) + vec(1, rows)
            h_s[rows, :] = h.astype(BF16)
            acc_s[rows, :] = jnp.zeros((FFN_ROW_BLK, acc_s.shape[1]), F32)
        _row_loop(tm, FFN_ROW_BLK, body)

    h = h_s[...]
    a = _silu(_dot(h, wg_ref[...].astype(BF16))) * _dot(h, wu_ref[...].astype(BF16))
    acc_s[...] += _dot(a.astype(BF16), wd_ref[...])

    @pl.when(f == pl.num_programs(1) - 1)
    def _():
        def body(rows):
            x1 = x_ref[rows, :] + (0.5 * vec(3, rows)) * acc_s[rows, :]
            if final:
                o_ref[rows, :] = _rms(x1) * vec_ref[7:8, :]
            else:
                x1_ref[rows, :] = x1
                h2 = _rms(x1) * (vec_ref[7:8, :] * (1.0 + vec(9, rows))) + vec(8, rows)
                h2_ref[rows, :] = h2.astype(BF16)
        _row_loop(tm, FFN_ROW_BLK, body)


def _ffn_call(x, vecs, wg, wu, wd, which, n_lat, final):
    m, d = x.shape
    dff = wg.shape[3]
    tm = _pick(m, (1024, 768, 512, 256))
    tf = _pick(dff, (512, 256, 128))
    row = pl.BlockSpec((tm, d), lambda i, f: (i, 0), pipeline_mode=pl.Buffered(1))
    x_row = pl.BlockSpec((tm, d), lambda i, f: (i, 0)) if tm <= FFN_PREFETCH_MAX_ROWS else row
    if final:
        out_shape = jax.ShapeDtypeStruct((m, d), F32)
        out_specs = row
    else:
        out_shape = (jax.ShapeDtypeStruct((m, d), F32), jax.ShapeDtypeStruct((m, d), BF16))
        out_specs = (row, row)
    return pl.pallas_call(
        functools.partial(_ffn_kernel, n_lat=n_lat, final=final),
        grid=(m // tm, dff // tf),
        in_specs=[x_row,
                  pl.BlockSpec(vecs.shape, lambda i, f: (0, 0)),
                  pl.BlockSpec((None, None, d, tf), lambda i, f: (0, which, 0, f)),
                  pl.BlockSpec((None, None, d, tf), lambda i, f: (0, which, 0, f)),
                  pl.BlockSpec((None, None, tf, d), lambda i, f: (0, which, f, 0))],
        out_specs=out_specs,
        out_shape=out_shape,
        scratch_shapes=[pltpu.VMEM((tm, d), BF16), pltpu.VMEM((tm, d), F32)],
        compiler_params=_cparams(("arbitrary", "arbitrary")),
    )(x, vecs, wg, wu, wd)


def _mm_kernel(a_ref, w_ref, *rest):
    if len(rest) == 3:
        w2_ref, o_ref, o2_ref = rest

        @pl.when(pl.program_id(1) == 0)
        def _():
            o2_ref[...] = _dot_nt(a_ref[...], w2_ref[...].astype(BF16))
    else:
        (o_ref,) = rest
    o_ref[...] = _dot_nt(a_ref[...], w_ref[...].astype(BF16)).astype(o_ref.dtype)


def _mm_call(a, wt, row0, n, m, out_dtype, side_row0=None):
    k = a.shape[1]
    tm = _pick(m, (1408, 1024, 768, 512, 256))
    tn = _pick(n, (1024, 512, 256, 128))
    assert row0 % SUBLANES == 0
    wspec = lambda rows, idx: pl.BlockSpec((pl.Element(rows), pl.Element(k)), idx)
    in_specs = [pl.BlockSpec((tm, k), lambda i, j: (i, 0)),
                wspec(tn, lambda i, j: (pl.multiple_of(row0 + j * tn, SUBLANES), 0))]
    out_specs = pl.BlockSpec((tm, tn), lambda i, j: (i, j))
    out_shape = jax.ShapeDtypeStruct((m, n), out_dtype)
    args = (a, wt)
    if side_row0 is not None:
        assert side_row0 % SUBLANES == 0
        in_specs.append(wspec(LANES, lambda i, j: (side_row0, 0)))
        out_specs = (out_specs, pl.BlockSpec((tm, LANES), lambda i, j: (i, 0)))
        out_shape = (out_shape, jax.ShapeDtypeStruct((m, LANES), F32))
        args = (a, wt, wt)
    return pl.pallas_call(
        _mm_kernel,
        grid=(m // tm, n // tn),
        in_specs=in_specs,
        out_specs=out_specs,
        out_shape=out_shape,
        compiler_params=_cparams(("arbitrary", "arbitrary")),
    )(*args)


def _mmconv_kernel(a_ref, ap_ref, an_ref, w_ref, cw_ref, cb_ref, *rest, n_lat, n_tot, scaled):
    if scaled:
        post_ref, o_ref, xs, ys, a_s = rest
    else:
        o_ref, xs, ys, a_s = rest
    i = pl.program_id(0)
    tm = a_ref.shape[0]
    tn = o_ref.shape[1]
    units = tm // CONV_UNIT
    ug = CONV_UNIT // SUBLANES
    pad = CONV_PAD * SUBLANES
    span = CONV_UNIT + 2 * pad
    seq_units = n_lat // CONV_UNIT
    last_unit = n_tot // CONV_UNIT - 1

    @pl.when(pl.program_id(1) == 0)
    def _():
        a_s[0:HALO, :] = ap_ref[...]
        a_s[HALO:HALO + tm, :] = a_ref[...]
        a_s[HALO + tm:2 * HALO + tm, :] = an_ref[...]

    w = w_ref[...].astype(BF16)
    p_all = _dot_nt(a_s[...], w)
    p_prev = p_all[0:HALO]
    p = p_all[HALO:HALO + tm]
    p_next = p_all[HALO + tm:2 * HALO + tm]
    sub = lax.broadcasted_iota(jnp.int32, (SUBLANES, 1), 0)

    def group(x, k):
        return x[k * SUBLANES:(k + 1) * SUBLANES]

    for u in range(units):
        g = i * units + u
        keep_prev = jnp.where((g == 0) | (g == seq_units), 0.0, 1.0)
        keep_next = jnp.where((g == last_unit) | (g == seq_units - 1), 0.0, 1.0)
        r0 = u * CONV_UNIT
        base = u * span
        pu = p[r0:r0 + CONV_UNIT]
        xs[base + pad:base + pad + CONV_UNIT, :] = pu
        before = p[r0 - pad:r0] if u > 0 else p_prev
        after = p[r0 + CONV_UNIT:r0 + CONV_UNIT + pad] if u < units - 1 else p_next
        for d in range(CONV_PAD):
            own = pltpu.roll(group(pu, ug - CONV_PAD + d), 1, 0)
            other = pltpu.roll(group(before, d), 1, 0) * keep_prev
            xs[base + d * SUBLANES:base + (d + 1) * SUBLANES, :] = jnp.where(sub == 0, other, own)
            own = pltpu.roll(group(pu, d), SUBLANES - 1, 0)
            other = pltpu.roll(group(after, d), SUBLANES - 1, 0) * keep_next
            lo = base + pad + CONV_UNIT + d * SUBLANES
            xs[lo:lo + SUBLANES, :] = jnp.where(sub == SUBLANES - 1, other, own)

        acc = cb_ref[...]
        for j in range(CONV_W):
            acc = acc + cw_ref[j:j + 1, :] * xs[base + SUBLANES * j:base + SUBLANES * j + CONV_UNIT, :]
        y = _silu(acc)
        if scaled:
            y = y * post_ref[...]
        for cblk in range(tn // LANES):
            lanes = slice(cblk * LANES, (cblk + 1) * LANES)
            ys[cblk, r0:r0 + CONV_UNIT, :] = y[:, lanes]
            for b in range(SUBLANES):
                o_ref[r0 + b * ug:r0 + (b + 1) * ug, lanes] = (
                    ys[cblk, pl.ds(r0 + b, ug, stride=SUBLANES), :].astype(o_ref.dtype))


def _mmconv_call(a, w, col0, conv_w, conv_b, post, n_lat):
    m, k = a.shape
    n = conv_w.shape[1]
    tm = _pick(m, (768, 512, 256))
    tn = _pick(n, (1024, 512, 256, 128))
    assert col0 % SUBLANES == 0 and tm % CONV_UNIT == 0 and n_lat % CONV_UNIT == 0 and m % CONV_UNIT == 0
    assert HALO == CONV_PAD * SUBLANES
    hb = tm // HALO
    last_halo = m // HALO - 1
    vec = pl.BlockSpec((1, tn), lambda i, j: (0, j))
    scaled = post is not None
    return pl.pallas_call(
        functools.partial(_mmconv_kernel, n_lat=n_lat, n_tot=m, scaled=scaled),
        grid=(m // tm, n // tn),
        in_specs=[pl.BlockSpec((tm, k), lambda i, j: (i, 0)),
                  pl.BlockSpec((HALO, k), lambda i, j: (jnp.maximum(i * hb - 1, 0), 0)),
                  pl.BlockSpec((HALO, k), lambda i, j: (jnp.minimum((i + 1) * hb, last_halo), 0)),
                  pl.BlockSpec((pl.Element(tn), pl.Element(k)),
                               lambda i, j: (pl.multiple_of(col0 + j * tn, SUBLANES), 0)),
                  pl.BlockSpec((SUBLANES, tn), lambda i, j: (0, j)),
                  vec] + ([vec] if scaled else []),
        out_specs=pl.BlockSpec((tm, tn), lambda i, j: (i, j)),
        out_shape=jax.ShapeDtypeStruct((m, n), BF16),
        scratch_shapes=[pltpu.VMEM(((tm // CONV_UNIT) * (CONV_UNIT + 2 * HALO), tn), F32),
                        pltpu.VMEM((tn // LANES, tm, LANES), F32),
                        pltpu.VMEM((tm + 2 * HALO, k), BF16)],
        compiler_params=_cparams(("arbitrary", "arbitrary")),
    )(a, a, a, w, conv_w, conv_b, *([post] if scaled else []))


def _unit_permute(x):
    rows, d = x.shape
    ug = CONV_UNIT // SUBLANES
    return x.reshape(rows // CONV_UNIT, SUBLANES, ug, d).transpose(0, 2, 1, 3).reshape(rows, d)


def _chunk_order(c, n_lat_chunks, n_ctx_chunks, reverse):
    if reverse:
        return n_lat_chunks + n_ctx_chunks - 1 - c
    return jnp.where(c < n_ctx_chunks, n_lat_chunks + c, c - n_ctx_chunks)


def _tri(chunk, reverse):
    r = lax.broadcasted_iota(jnp.int32, (chunk, chunk), 0)
    c = lax.broadcasted_iota(jnp.int32, (chunk, chunk), 1)
    return (c >= r) if reverse else (c <= r)


def _mlstm_kernel(q_ref, k_ref, v_ref, g_ref, gn_ref, gb_ref, o_ref, ct_s, n_s, m_s, bc_s, bct_s, gt_s,
                  bcn_s, bctn_s, gtn_s, *, reverse):
    L = q_ref.shape[0]
    dk = q_ref.shape[1] // ML_HEADS
    dv = v_ref.shape[1] // ML_HEADS
    c = pl.program_id(0)

    mask = _tri(L, reverse)

    def prefix(g_blk, bc_d, bct_d, gt_d):
        g = g_blk + gb_ref[...]
        bc = _dot_exact(mask.astype(F32), _log_sigmoid(g))
        bc_d[...] = bc
        bct_d[...] = bc.T
        gt_d[...] = g.T

    @pl.when(c == 0)
    def _():
        ct_s[...] = jnp.zeros_like(ct_s)
        n_s[...] = jnp.zeros_like(n_s)
        m_s[...] = jnp.zeros_like(m_s)
        prefix(g_ref[...], bc_s, bct_s, gt_s)

    prefix(gn_ref[...], bcn_s, bctn_s, gtn_s)

    def lanes(x, n):
        return jnp.concatenate([x] * (n // LANES), axis=1)

    bc = bc_s[...]
    bc_t = bct_s[...]
    g_t = gt_s[...]
    end = 0 if reverse else L - 1
    base = 2 * ML_HEADS if reverse else 0
    ones = jnp.ones((L, LANES), BF16)

    for h in range(ML_HEADS):
        io = base + h
        fo = base + ML_HEADS + h
        bb = jnp.broadcast_to(bc[:, fo:fo + 1], (L, LANES))
        b_row = bc_t[fo:fo + 1, :]
        ig_row = g_t[io:io + 1, :]
        b_end = bc[end:end + 1, fo:fo + 1]
        m0 = m_s[h:h + 1, :]
        q = q_ref[:, h * dk:(h + 1) * dk]
        k = k_ref[:, h * dk:(h + 1) * dk]
        v = v_ref[:, h * dv:(h + 1) * dv]
        ct0 = ct_s[h]
        n0 = n_s[h]

        logw = jnp.where(mask, lanes(bb, L) + (ig_row - b_row), NEG_INF)
        m_inter = bb + m0
        m_intra = jnp.broadcast_to(jnp.max(logw, axis=1, keepdims=True), (L, LANES))
        m_out = jnp.maximum(m_inter, m_intra)
        p = (_dot_nt(q, k) * jnp.exp(logw - lanes(m_out, L))).astype(BF16)
        s_inter = jnp.exp(m_inter - m_out)
        num = _dot(p, v) + lanes(s_inter, dv) * _dot(q, ct0.astype(BF16))
        den = _dot(p, ones) + s_inter * _dot(q, n0.astype(BF16))
        inv = 1.0 / jnp.maximum(jnp.abs(den), jnp.exp(-m_out))
        o_ref[:, h * dv:(h + 1) * dv] = (num * lanes(inv, dv)).astype(o_ref.dtype)

        a_row = b_end + ig_row - b_row
        m_loc = jnp.max(a_row, axis=1, keepdims=True)
        kw_t = (k.T.astype(F32) * jnp.exp(a_row - m_loc)).astype(BF16)
        m_new = jnp.maximum(b_end + m0, m_loc)
        s_old = jnp.exp(b_end + m0 - m_new)
        s_new = jnp.exp(m_loc - m_new)
        ct_s[h] = lanes(s_old, dv) * ct0 + lanes(s_new, dv) * _dot(kw_t, v)
        n_s[h] = s_old * n0 + s_new * _dot(kw_t, ones)
        m_s[h:h + 1, :] = m_new

    bc_s[...] = bcn_s[...]
    bct_s[...] = bctn_s[...]
    gt_s[...] = gtn_s[...]


def _mlstm_call(qk, vo, gates, gate_b, n_lat, reverse):
    rows = qk.shape[0]
    L = ML_L
    nqk = qk.shape[1] // 2
    nv = vo.shape[1] // 2
    dk = nqk // ML_HEADS
    dv = nv // ML_HEADS
    nlc = n_lat // L
    ncc = (rows - n_lat) // L
    order = functools.partial(_chunk_order, n_lat_chunks=nlc, n_ctx_chunks=ncc, reverse=reverse)
    return pl.pallas_call(
        functools.partial(_mlstm_kernel, reverse=reverse),
        grid=(nlc + ncc,),
        in_specs=[pl.BlockSpec((L, nqk), lambda c: (order(c), 0)),
                  pl.BlockSpec((L, nqk), lambda c: (order(c), 1)),
                  pl.BlockSpec((L, nv), lambda c: (order(c), 0)),
                  pl.BlockSpec((L, LANES), lambda c: (order(c), 0)),
                  pl.BlockSpec((L, LANES), lambda c: (order(jnp.minimum(c + 1, nlc + ncc - 1)), 0)),
                  pl.BlockSpec((1, LANES), lambda c: (0, 0))],
        out_specs=pl.BlockSpec((L, nv), lambda c: (order(c), 0)),
        out_shape=jax.ShapeDtypeStruct((rows, nv), BF16),
        scratch_shapes=[pltpu.VMEM((ML_HEADS, dk, dv), F32),
                        pltpu.VMEM((ML_HEADS, dk, LANES), F32),
                        pltpu.VMEM((ML_HEADS, LANES), F32)]
        + 2 * [pltpu.VMEM((L, LANES), F32), pltpu.VMEM((LANES, L), F32), pltpu.VMEM((LANES, L), F32)],
        compiler_params=_cparams(("arbitrary",)),
    )(qk, qk, vo, gates, gates, gate_b)


def _ssd_kernel(x_ref, bm_ref, cm_ref, dt_ref, dtn_ref, dtb_ref, a_ref, dsk_ref, y_ref,
                st_s, bc_s, bct_s, dtt_s, bcn_s, bctn_s, dttn_s, *, reverse):
    L = SSM_L
    nsub = x_ref.shape[0] // L
    nx = x_ref.shape[1]
    c = pl.program_id(0)
    mask = _tri(L, reverse)

    def prefix(dt_blk, bc_d, bct_d, dtt_d):
        for s in range(nsub):
            dtv = _softplus(dt_blk[s * L:(s + 1) * L] + dtb_ref[...])
            bc = _dot_exact(mask.astype(F32), dtv * a_ref[...]) * LOG2E
            bc_d[s] = bc
            bct_d[s] = bc.T
            dtt_d[s] = jnp.log2(dtv).T

    @pl.when(c == 0)
    def _():
        st_s[...] = jnp.zeros_like(st_s)
        prefix(dt_ref[...], bc_s, bct_s, dtt_s)

    prefix(dtn_ref[...], bcn_s, bctn_s, dttn_s)

    end = 0 if reverse else L - 1
    nheads = nx // SSM_HEADDIM
    hpg = nheads // SSM_GROUPS
    base = nheads if reverse else 0
    pw = 2 * SSM_HEADDIM
    lo = lax.broadcasted_iota(jnp.int32, (1, pw), 1) < SSM_HEADDIM
    lo_b = jnp.where(lo, 1.0, 0.0).astype(BF16)
    hi_b = jnp.where(lo, 0.0, 1.0).astype(BF16)

    for sub in (range(nsub - 1, -1, -1) if reverse else range(nsub)):
        rows = slice(sub * L, (sub + 1) * L)
        bc = bc_s[sub]
        bc_t = bct_s[sub]
        ldt_t = dtt_s[sub]
        _ssd_chunk(x_ref, bm_ref, cm_ref, dsk_ref, y_ref, st_s, rows, bc, bc_t, ldt_t, mask, lo, lo_b, hi_b,
                   end=end, hpg=hpg, base=base, pw=pw, reverse=reverse)

    bc_s[...] = bcn_s[...]
    bct_s[...] = bctn_s[...]
    dtt_s[...] = dttn_s[...]


def _ssd_chunk(x_ref, bm_ref, cm_ref, dsk_ref, y_ref, st_s, rows, bc, bc_t, ldt_t, mask, lo, lo_b, hi_b,
               *, end, hpg, base, pw, reverse):
    L = SSM_L
    for g in range(SSM_GROUPS):
        bm = bm_ref[rows, g * SSM_DSTATE:(g + 1) * SSM_DSTATE]
        cm = cm_ref[rows, g * SSM_DSTATE:(g + 1) * SSM_DSTATE]
        cb = _dot_nt(cm, bm)
        bm_t = bm.astype(F32).T
        for pr in range(hpg // 2):
            col = (g * hpg + 2 * pr) * SSM_HEADDIM
            xp = x_ref[rows, col:col + pw]
            x_half = (xp * lo_b, xp * hi_b)
            s0 = st_s[:, col:col + pw]
            y = None
            st = None
            bbs, decs = [], []
            for e in range(2):
                co = base + g * hpg + 2 * pr + e
                bb = jnp.broadcast_to(bc[:, co:co + 1], (L, L))
                r_row = bc_t[co:co + 1, :] - ldt_t[co:co + 1, :]
                b_end = bc[end:end + 1, co:co + 1]
                m = (cb * jnp.exp2(jnp.where(mask, bb - r_row, NEG_INF))).astype(BF16)
                ye = _dot(m, x_half[e])
                y = ye if y is None else y + ye
                bt = (bm_t * jnp.exp2(b_end - r_row)).astype(BF16)
                se = _dot(bt, x_half[e])
                st = se if st is None else st + se
                bbs.append(bb[:, 0:pw])
                decs.append(jnp.exp2(b_end))
            y = y + jnp.exp2(jnp.where(lo, bbs[0], bbs[1])) * _dot(cm, s0.astype(BF16))
            if not reverse:
                y = y + dsk_ref[:, col:col + pw] * xp.astype(F32)
            y_ref[rows, col:col + pw] = y.astype(y_ref.dtype)
            st_s[:, col:col + pw] = s0 * jnp.where(lo, decs[0], decs[1]) + st


def _ssd_call(xbc, dt, dt_bias, a_neg, d_skip, n_lat, reverse):
    rows = xbc.shape[0]
    n_sub = SSM_STEP_CHUNKS if n_lat % (SSM_STEP_CHUNKS * SSM_L) == 0 and (rows - n_lat) % (SSM_STEP_CHUNKS * SSM_L) == 0 else 1
    L = n_sub * SSM_L
    assert SSM_L >= 2 * SSM_HEADDIM
    nb = SSM_GROUPS * SSM_DSTATE
    nx = xbc.shape[1] - 2 * nb
    nlc = n_lat // L
    ncc = (rows - n_lat) // L
    order = functools.partial(_chunk_order, n_lat_chunks=nlc, n_ctx_chunks=ncc, reverse=reverse)
    xblk = nx // nb
    nc = nlc + ncc
    return pl.pallas_call(
        functools.partial(_ssd_kernel, reverse=reverse),
        grid=(nc,),
        in_specs=[pl.BlockSpec((L, nx), lambda c: (order(c), 0)),
                  pl.BlockSpec((L, nb), lambda c: (order(c), xblk)),
                  pl.BlockSpec((L, nb), lambda c: (order(c), xblk + 1)),
                  pl.BlockSpec((L, LANES), lambda c: (order(c), 0)),
                  pl.BlockSpec((L, LANES), lambda c: (order(jnp.minimum(c + 1, nc - 1)), 0)),
                  pl.BlockSpec((1, LANES), lambda c: (0, 0)),
                  pl.BlockSpec((1, LANES), lambda c: (0, 0)),
                  pl.BlockSpec((1, nx), lambda c: (0, 0))],
        out_specs=pl.BlockSpec((L, nx), lambda c: (order(c), 0)),
        out_shape=jax.ShapeDtypeStruct((rows, nx), BF16),
        scratch_shapes=[pltpu.VMEM((SSM_DSTATE, nx), F32)]
        + 2 * [pltpu.VMEM((n_sub, SSM_L, LANES), F32), pltpu.VMEM((n_sub, LANES, SSM_L), F32),
               pltpu.VMEM((n_sub, LANES, SSM_L), F32)],
        compiler_params=_cparams(("arbitrary",)),
    )(xbc, xbc, xbc, dt, dt, dt_bias, a_neg, d_skip)


def _mlgate_kernel(hf_ref, hb_ref, o_ref, nw_ref, a_ref):
    dv = a_ref.shape[1] // ML_HEADS

    def body(rows):
        for h in range(ML_HEADS):
            sl = slice(h * dv, (h + 1) * dv)
            hh = _rms(hf_ref[rows, sl].astype(F32) + hb_ref[rows, sl].astype(F32)) * nw_ref[:, sl]
            a_ref[rows, sl] = (_sigmoid(o_ref[rows, sl].astype(F32)) * hh).astype(BF16)
    _row_loop(a_ref.shape[0], ROW_BLK, body)


def _mlgate_call(hf, hb, vo, norm_w, n_lat):
    nv = hf.shape[1]
    tm = _pick(n_lat, (512, 256))
    row = pl.BlockSpec((tm, nv), lambda i: (i, 0))
    return pl.pallas_call(
        _mlgate_kernel,
        grid=(n_lat // tm,),
        in_specs=[row, row, pl.BlockSpec((tm, nv), lambda i: (i, 1)), pl.BlockSpec((1, nv), lambda i: (0, 0))],
        out_specs=row,
        out_shape=jax.ShapeDtypeStruct((n_lat, nv), BF16),
        compiler_params=_cparams(("arbitrary",)),
    )(hf, hb, vo, norm_w)


def _ssmgate_kernel(yf_ref, yb_ref, z_ref, nw_ref, a_ref):
    gw = a_ref.shape[1] // SSM_GROUPS

    def body(rows):
        for g in range(SSM_GROUPS):
            sl = slice(g * gw, (g + 1) * gw)
            y = (yf_ref[rows, sl].astype(F32) + yb_ref[rows, sl].astype(F32)) * _silu(z_ref[rows, sl].astype(F32))
            a_ref[rows, sl] = (_rms(y) * nw_ref[:, sl]).astype(BF16)
    _row_loop(a_ref.shape[0], ROW_BLK, body)


def _ssmgate_call(yf, yb, z, norm_w, n_lat):
    nx = yf.shape[1]
    tm = _pick(n_lat, (512, 256))
    row = pl.BlockSpec((tm, nx), lambda i: (i, 0))
    return pl.pallas_call(
        _ssmgate_kernel,
        grid=(n_lat // tm,),
        in_specs=[row, row, row, pl.BlockSpec((1, nx), lambda i: (0, 0))],
        out_specs=row,
        out_shape=jax.ShapeDtypeStruct((n_lat, nx), BF16),
        compiler_params=_cparams(("arbitrary",)),
    )(yf, yb, z, norm_w)


def _proj_kernel(a_ref, w_ref, y_ref):
    y_ref[...] = _dot(a_ref[...], w_ref[...]).astype(y_ref.dtype)


def _proj_call(a, w):
    m, k = a.shape
    n = w.shape[1]
    tm = _pick(m, (1024, 512, 256))
    tn = _pick(n, (1024, 512, 256, 128))
    return pl.pallas_call(
        _proj_kernel,
        grid=(m // tm, n // tn),
        in_specs=[pl.BlockSpec((tm, k), lambda i, j: (i, 0)),
                  pl.BlockSpec((k, tn), lambda i, j: (0, j))],
        out_specs=pl.BlockSpec((tm, tn), lambda i, j: (i, j)),
        out_shape=jax.ShapeDtypeStruct((m, n), BF16),
        compiler_params=_cparams(("arbitrary", "arbitrary")),
    )(a, w)


def _gate_kernel(h_ref, w1_ref, w2_ref, b1_ref, b2_ref, yml_ref, yssm_ref, o_ref):
    h = h_ref[...]
    g1 = _sigmoid(_dot(h, w1_ref[...]) + b1_ref[...])
    g2 = _sigmoid(_dot(h, w2_ref[...]) + b2_ref[...])
    o_ref[...] = (g1 * yml_ref[...].astype(F32) + g2 * yssm_ref[...].astype(F32)).astype(o_ref.dtype)


def _gate_call(h, w_gate, b_gate, y_ml, y_ssm, n_lat):
    d = h.shape[1]
    tm = _pick(n_lat, (1024, 512, 256))
    tn = _pick(d, (1024, 512, 256, 128))
    nj = d // tn
    tile = pl.BlockSpec((tm, tn), lambda i, j: (i, j))
    return pl.pallas_call(
        _gate_kernel,
        grid=(n_lat // tm, nj),
        in_specs=[pl.BlockSpec((tm, d), lambda i, j: (i, 0)),
                  pl.BlockSpec((d, tn), lambda i, j: (0, j)),
                  pl.BlockSpec((d, tn), lambda i, j: (0, nj + j)),
                  pl.BlockSpec((1, tn), lambda i, j: (0, j)),
                  pl.BlockSpec((1, tn), lambda i, j: (0, nj + j)),
                  tile, tile],
        out_specs=tile,
        out_shape=jax.ShapeDtypeStruct((n_lat, d), BF16),
        compiler_params=_cparams(("arbitrary", "arbitrary")),
    )(h, w_gate, w_gate, b_gate, b_gate, y_ml, y_ssm)


def _out_kernel(a_ref, w_ref, x_ref, g_ref, o_ref):
    o_ref[...] = x_ref[...] + g_ref[...] * _dot(a_ref[...], w_ref[...])


def _out_call(mix, w_out, x1, gate, n_lat):
    d = mix.shape[1]
    tm = _pick(n_lat, (1024, 512, 256))
    tn = _pick(d, (1024, 512, 256, 128))
    tile = pl.BlockSpec((tm, tn), lambda i, j: (i, j))
    return pl.pallas_call(
        _out_kernel,
        grid=(n_lat // tm, d // tn),
        in_specs=[pl.BlockSpec((tm, d), lambda i, j: (i, 0)),
                  pl.BlockSpec((d, tn), lambda i, j: (0, j)),
                  tile,
                  pl.BlockSpec((1, tn), lambda i, j: (0, j))],
        out_specs=tile,
        out_shape=jax.ShapeDtypeStruct((n_lat, d), F32),
        compiler_params=_cparams(("arbitrary", "arbitrary")),
    )(mix, w_out, x1, gate)


def _pad_rows(a, rows):
    return jnp.concatenate([a, jnp.zeros((rows - a.shape[0],) + a.shape[1:], a.dtype)], axis=0)


def _pad_cols(a, cols):
    return jnp.concatenate([a, jnp.zeros(a.shape[:-1] + (cols - a.shape[-1],), a.dtype)], axis=-1)


def kernel(x, c, ctx, c_ctx, w_ada, b_ada, norm_w, ffn_w_gate, ffn_w_up, ffn_w_down, w_in, ml_conv_w, ml_conv_b, ml_gate_b, ml_norm_w, w_proj_ml, ssm_conv_w, ssm_conv_b, ssm_dt_bias, ssm_a_log, ssm_d, ssm_norm_w, w_proj_ssm, w_gate, b_gate, w_out, final_norm_w):
    depth = w_ada.shape[0]
    assert depth == 1 and x.shape[0] == 1, "single layer, single batch element"
    t, d = x.shape[1], x.shape[2]
    n_tot = t + ctx.shape[1]
    rows_cm = t // GRID_W
    ml_qk = ml_conv_w.shape[2] // 2
    ml_v = ml_norm_w.shape[1]
    n_ml = 2 * ml_qk + 2 * ml_v
    ml_cols = n_ml + 4 * ML_HEADS
    ssm_inner = ssm_norm_w.shape[1]
    ssm_heads = ssm_d.shape[1]
    ssm_xbc = ssm_conv_w.shape[2]

    cc = _pad_rows(jnp.concatenate([c, c_ctx[None, :]], axis=0), SUBLANES)
    mod = _mod_call(cc, w_ada[0], b_ada[0])
    ml, mc = mod[0].reshape(N_MOD, d), mod[1].reshape(N_MOD, d)
    nw = norm_w[0]
    zero = jnp.zeros_like(nw[0])
    vec1 = _pad_rows(jnp.stack([nw[0], ml[0], ml[1], ml[2], mc[0], mc[1], mc[2],
                                nw[1], ml[3], ml[4], zero, mc[3], mc[4]]), 16)
    vec2 = _pad_rows(jnp.stack([nw[2], ml[6], ml[7], ml[8], zero, zero, zero, final_norm_w]), 16)
    wg, wu, wd = ffn_w_gate, ffn_w_up, ffn_w_down.astype(BF16)

    xall = jnp.concatenate([x[0], ctx[0]], axis=0)
    x1, h = _ffn_call(xall, vec1, wg, wu, wd, 0, t, final=False)

    h_cm = jnp.concatenate(
        [h[:t].reshape(rows_cm, GRID_W, d).transpose(1, 0, 2).reshape(t, d), h[t:]], axis=0)
    wt = w_in[0].T
    n_ssm = ssm_inner + ssm_xbc
    ml_cw = _pad_rows(ml_conv_w[0], SUBLANES)
    ml_post = jnp.concatenate([jnp.full((1, ml_qk), (ml_qk // ML_HEADS) ** -0.5, F32),
                               jnp.ones((1, ml_qk), F32)], axis=1)
    assert 2 * ssm_heads == LANES
    qk = _mmconv_call(_unit_permute(h_cm), wt, 0, ml_cw, ml_conv_b[0][None, :], ml_post, t)
    vo, gates = _mm_call(h_cm, wt, 2 * ml_qk, 2 * ml_v, n_tot, BF16, side_row0=n_ml)
    z, dt = _mm_call(h, wt, ml_cols, ssm_inner, n_tot, BF16, side_row0=ml_cols + n_ssm)
    xbc = _mmconv_call(_unit_permute(h), wt, ml_cols + ssm_inner, _pad_rows(ssm_conv_w[0], SUBLANES),
                       ssm_conv_b[0][None, :], None, t)

    gate_b = _pad_cols(ml_gate_b[0].reshape(1, 4 * ML_HEADS), LANES)
    h_f = _mlstm_call(qk, vo, gates, gate_b, t, reverse=False)
    h_b = _mlstm_call(qk, vo, gates, gate_b, t, reverse=True)
    y_ml = _proj_call(_mlgate_call(h_f, h_b, vo, ml_norm_w[0][None, :], t), w_proj_ml[0].astype(BF16))
    y_ml = y_ml.reshape(GRID_W, rows_cm, d).transpose(1, 0, 2).reshape(t, d)

    dt_bias = ssm_dt_bias[0].reshape(1, 2 * ssm_heads)
    a_neg = -jnp.exp(ssm_a_log[0].astype(F32)).reshape(1, 2 * ssm_heads)
    d_skip = jnp.repeat(ssm_d[0], SSM_HEADDIM)[None, :]
    y_f = _ssd_call(xbc, dt, dt_bias, a_neg, d_skip, t, reverse=False)
    y_b = _ssd_call(xbc, dt, dt_bias, a_neg, d_skip, t, reverse=True)
    y_ssm = _proj_call(_ssmgate_call(y_f, y_b, z, ssm_norm_w[0][None, :], t), w_proj_ssm[0].astype(BF16))

    mix = _gate_call(h, w_gate[0].astype(BF16), b_gate[0][None, :], y_ml, y_ssm, t)
    x2 = _out_call(mix, w_out[0].astype(BF16), x1, ml[5][None, :], t)
    out = _ffn_call(x2, vec2, wg, wu, wd, 1, t, final=True)
    return out[None]
```

```python
import functools

import jax
import jax.numpy as jnp
from jax import lax
from jax.experimental import pallas as pl
from jax.experimental.pallas import tpu as pltpu

F32 = jnp.float32
BF16 = jnp.bfloat16

GRID_W = 64
N_MOD = 9
EPS = 1e-6
CONV_W = 5
CONV_PAD = CONV_W // 2
HALO = 16
CONV_UNIT = 256
ROW_BLK = 16
FFN_ROW_BLK = 32
ROW_LOOP_UNROLL = 4
FFN_PREFETCH_MAX_ROWS = 768
ML_HEADS = 8
ML_L = 256
SSM_HEADDIM = 64
SSM_GROUPS = 8
SSM_DSTATE = 128
SSM_L = 128
SSM_STEP_CHUNKS = 2
LANES = 128
SUBLANES = 8
VMEM_LIMIT_BYTES = 56 * 1024 * 1024
NEG_INF = float("-inf")
LOG2E = 1.4426950408889634


def _cparams(sem):
    return pltpu.CompilerParams(dimension_semantics=sem, vmem_limit_bytes=VMEM_LIMIT_BYTES)


def _pick(n, cands):
    for c in cands:
        if n % c == 0:
            return c
    raise ValueError(f"no tile for {n} in {cands}")


def _silu(x):
    return x * (1.0 / (1.0 + jnp.exp(-x)))


def _sigmoid(x):
    return 1.0 / (1.0 + jnp.exp(-x))


def _log_sigmoid(x):
    return jnp.minimum(x, 0.0) - jnp.log1p(jnp.exp(-jnp.abs(x)))


def _softplus(x):
    return jnp.maximum(x, 0.0) + jnp.log1p(jnp.exp(-jnp.abs(x)))


def _rms(x):
    return x * lax.rsqrt(jnp.mean(jnp.square(x), axis=-1, keepdims=True) + EPS)


def _dot(a, b):
    return jnp.dot(a, b, preferred_element_type=F32)


def _dot_nt(a, b):
    return lax.dot_general(a, b, (((1,), (1,)), ((), ())), preferred_element_type=F32)


def _dot_exact(a, b):
    return jnp.dot(a, b, preferred_element_type=F32, precision=lax.Precision.HIGHEST)


def _row_loop(n_rows, blk, body):
    def step(r, carry):
        body(pl.ds(pl.multiple_of(r * blk, blk), blk))
        return carry
    lax.fori_loop(0, n_rows // blk, step, 0, unroll=ROW_LOOP_UNROLL)


def _mod_kernel(c_ref, w_ref, b_ref, o_ref):
    a = _silu(c_ref[...]).astype(BF16)
    o_ref[...] = _dot(a, w_ref[...].astype(BF16)) + b_ref[...]


def _mod_call(cc, w_ada, b_ada):
    k, n = w_ada.shape
    tn = _pick(n, (1024, 512, 256, 128))
    return pl.pallas_call(
        _mod_kernel,
        grid=(n // tn,),
        in_specs=[pl.BlockSpec((SUBLANES, k), lambda j: (0, 0)),
                  pl.BlockSpec((k, tn), lambda j: (0, j)),
                  pl.BlockSpec((1, tn), lambda j: (0, j))],
        out_specs=pl.BlockSpec((SUBLANES, tn), lambda j: (0, j)),
        out_shape=jax.ShapeDtypeStruct((SUBLANES, n), F32),
        compiler_params=_cparams(("arbitrary",)),
    )(cc, w_ada, b_ada.reshape(1, n))


def _ffn_kernel(x_ref, vec_ref, wg_ref, wu_ref, wd_ref, *rest, n_lat, final):
    if final:
        o_ref, h_s, acc_s = rest
    else:
        x1_ref, h2_ref, h_s, acc_s = rest
    i = pl.program_id(0)
    f = pl.program_id(1)
    tm = x_ref.shape[0]

    def vec(row, rows):
        off = jnp.where(i * tm + rows.start >= n_lat, 3, 0)
        return vec_ref[pl.ds(row + off, 1), :]

    @pl.when(f == 0)
    def _():
        def body(rows):
            h = _rms(x_ref[rows, :]) * (vec_ref[0:1, :] * (1.0 + vec(2, rows))) + vec(1, rows)
            h_s[rows, :] = h.astype(BF16)
            acc_s[rows, :] = jnp.zeros((FFN_ROW_BLK, acc_s.shape[1]), F32)
        _row_loop(tm, FFN_ROW_BLK, body)

    h = h_s[...]
    a = _silu(_dot(h, wg_ref[...].astype(BF16))) * _dot(h, wu_ref[...].astype(BF16))
    acc_s[...] += _dot(a.astype(BF16), wd_ref[...])

    @pl.when(f == pl.num_programs(1) - 1)
    def _():
        def body(rows):
            x1 = x_ref[rows, :] + (0.5 * vec(3, rows)) * acc_s[rows, :]
            if final:
                o_ref[rows, :] = _rms(x1) * vec_ref[7:8, :]
            else:
                x1_ref[rows, :] = x1
                h2 = _rms(x1) * (vec_ref[7:8, :] * (1.0 + vec(9, rows))) + vec(8, rows)
                h2_ref[rows, :] = h2.astype(BF16)
        _row_loop(tm, FFN_ROW_BLK, body)


def _ffn_call(x, vecs, wg, wu, wd, which, n_lat, final):
    m, d = x.shape
    dff = wg.shape[3]
    tm = _pick(m, (1024, 768, 512, 256))
    tf = _pick(dff, (512, 256, 128))
    row = pl.BlockSpec((tm, d), lambda i, f: (i, 0), pipeline_mode=pl.Buffered(1))
    x_row = pl.BlockSpec((tm, d), lambda i, f: (i, 0)) if tm <= FFN_PREFETCH_MAX_ROWS else row
    if final:
        out_shape = jax.ShapeDtypeStruct((m, d), F32)
        out_specs = row
    else:
        out_shape = (jax.ShapeDtypeStruct((m, d), F32), jax.ShapeDtypeStruct((m, d), BF16))
        out_specs = (row, row)
    return pl.pallas_call(
        functools.partial(_ffn_kernel, n_lat=n_lat, final=final),
        grid=(m // tm, dff // tf),
        in_specs=[x_row,
                  pl.BlockSpec(vecs.shape, lambda i, f: (0, 0)),
                  pl.BlockSpec((None, None, d, tf), lambda i, f: (0, which, 0, f)),
                  pl.BlockSpec((None, None, d, tf), lambda i, f: (0, which, 0, f)),
                  pl.BlockSpec((None, None, tf, d), lambda i, f: (0, which, f, 0))],
        out_specs=out_specs,
        out_shape=out_shape,
        scratch_shapes=[pltpu.VMEM((tm, d), BF16), pltpu.VMEM((tm, d), F32)],
        compiler_params=_cparams(("parallel", "arbitrary")),
    )(x, vecs, wg, wu, wd)


def _mm_kernel(a_ref, w_ref, *rest):
    if len(rest) == 3:
        w2_ref, o_ref, o2_ref = rest

        @pl.when(pl.program_id(1) == 0)
        def _():
            o2_ref[...] = _dot_nt(a_ref[...], w2_ref[...].astype(BF16))
    else:
        (o_ref,) = rest
    o_ref[...] = _dot_nt(a_ref[...], w_ref[...].astype(BF16)).astype(o_ref.dtype)


def _mm_call(a, wt, row0, n, m, out_dtype, side_row0=None):
    k = a.shape[1]
    tm = _pick(m, (1408, 1024, 768, 512, 256))
    tn = _pick(n, (1024, 512, 256, 128))
    assert row0 % SUBLANES == 0
    wspec = lambda rows, idx: pl.BlockSpec((pl.Element(rows), pl.Element(k)), idx)
    in_specs = [pl.BlockSpec((tm, k), lambda i, j: (i, 0)),
                wspec(tn, lambda i, j: (pl.multiple_of(row0 + j * tn, SUBLANES), 0))]
    out_specs = pl.BlockSpec((tm, tn), lambda i, j: (i, j))
    out_shape = jax.ShapeDtypeStruct((m, n), out_dtype)
    args = (a, wt)
    if side_row0 is not None:
        assert side_row0 % SUBLANES == 0
        in_specs.append(wspec(LANES, lambda i, j: (side_row0, 0)))
        out_specs = (out_specs, pl.BlockSpec((tm, LANES), lambda i, j: (i, 0)))
        out_shape = (out_shape, jax.ShapeDtypeStruct((m, LANES), F32))
        args = (a, wt, wt)
    return pl.pallas_call(
        _mm_kernel,
        grid=(m // tm, n // tn),
        in_specs=in_specs,
        out_specs=out_specs,
        out_shape=out_shape,
        compiler_params=_cparams(("parallel", "arbitrary")),
    )(*args)


def _mmconv_kernel(a_ref, ap_ref, an_ref, w_ref, cw_ref, cb_ref, *rest, n_lat, n_tot, scaled):
    if scaled:
        post_ref, o_ref, xs, ys, a_s = rest
    else:
        o_ref, xs, ys, a_s = rest
    i = pl.program_id(0)
    tm = a_ref.shape[0]
    tn = o_ref.shape[1]
    units = tm // CONV_UNIT
    ug = CONV_UNIT // SUBLANES
    pad = CONV_PAD * SUBLANES
    span = CONV_UNIT + 2 * pad
    seq_units = n_lat // CONV_UNIT
    last_unit = n_tot // CONV_UNIT - 1

    @pl.when(pl.program_id(1) == 0)
    def _():
        a_s[0:HALO, :] = ap_ref[...]
        a_s[HALO:HALO + tm, :] = a_ref[...]
        a_s[HALO + tm:2 * HALO + tm, :] = an_ref[...]

    w = w_ref[...].astype(BF16)
    p_all = _dot_nt(a_s[...], w)
    p_prev = p_all[0:HALO]
    p = p_all[HALO:HALO + tm]
    p_next = p_all[HALO + tm:2 * HALO + tm]
    sub = lax.broadcasted_iota(jnp.int32, (SUBLANES, 1), 0)

    def group(x, k):
        return x[k * SUBLANES:(k + 1) * SUBLANES]

    for u in range(units):
        g = i * units + u
        keep_prev = jnp.where((g == 0) | (g == seq_units), 0.0, 1.0)
        keep_next = jnp.where((g == last_unit) | (g == seq_units - 1), 0.0, 1.0)
        r0 = u * CONV_UNIT
        base = u * span
        pu = p[r0:r0 + CONV_UNIT]
        xs[base + pad:base + pad + CONV_UNIT, :] = pu
        before = p[r0 - pad:r0] if u > 0 else p_prev
        after = p[r0 + CONV_UNIT:r0 + CONV_UNIT + pad] if u < units - 1 else p_next
        for d in range(CONV_PAD):
            own = pltpu.roll(group(pu, ug - CONV_PAD + d), 1, 0)
            other = pltpu.roll(group(before, d), 1, 0) * keep_prev
            xs[base + d * SUBLANES:base + (d + 1) * SUBLANES, :] = jnp.where(sub == 0, other, own)
            own = pltpu.roll(group(pu, d), SUBLANES - 1, 0)
            other = pltpu.roll(group(after, d), SUBLANES - 1, 0) * keep_next
            lo = base + pad + CONV_UNIT + d * SUBLANES
            xs[lo:lo + SUBLANES, :] = jnp.where(sub == SUBLANES - 1, other, own)

        acc = cb_ref[...]
        for j in range(CONV_W):
            acc = acc + cw_ref[j:j + 1, :] * xs[base + SUBLANES * j:base + SUBLANES * j + CONV_UNIT, :]
        y = _silu(acc)
        if scaled:
            y = y * post_ref[...]
        for cblk in range(tn // LANES):
            lanes = slice(cblk * LANES, (cblk + 1) * LANES)
            ys[cblk, r0:r0 + CONV_UNIT, :] = y[:, lanes]
            for b in range(SUBLANES):
                o_ref[r0 + b * ug:r0 + (b + 1) * ug, lanes] = (
                    ys[cblk, pl.ds(r0 + b, ug, stride=SUBLANES), :].astype(o_ref.dtype))


def _mmconv_call(a, w, col0, conv_w, conv_b, post, n_lat):
    m, k = a.shape
    n = conv_w.shape[1]
    tm = _pick(m, (768, 512, 256))
    tn = _pick(n, (1024, 512, 256, 128))
    assert col0 % SUBLANES == 0 and tm % CONV_UNIT == 0 and n_lat % CONV_UNIT == 0 and m % CONV_UNIT == 0
    assert HALO == CONV_PAD * SUBLANES
    hb = tm // HALO
    last_halo = m // HALO - 1
    vec = pl.BlockSpec((1, tn), lambda i, j: (0, j))
    scaled = post is not None
    return pl.pallas_call(
        functools.partial(_mmconv_kernel, n_lat=n_lat, n_tot=m, scaled=scaled),
        grid=(m // tm, n // tn),
        in_specs=[pl.BlockSpec((tm, k), lambda i, j: (i, 0)),
                  pl.BlockSpec((HALO, k), lambda i, j: (jnp.maximum(i * hb - 1, 0), 0)),
                  pl.BlockSpec((HALO, k), lambda i, j: (jnp.minimum((i + 1) * hb, last_halo), 0)),
                  pl.BlockSpec((pl.Element(tn), pl.Element(k)),
                               lambda i, j: (pl.multiple_of(col0 + j * tn, SUBLANES), 0)),
                  pl.BlockSpec((SUBLANES, tn), lambda i, j: (0, j)),
                  vec] + ([vec] if scaled else []),
        out_specs=pl.BlockSpec((tm, tn), lambda i, j: (i, j)),
        out_shape=jax.ShapeDtypeStruct((m, n), BF16),
        scratch_shapes=[pltpu.VMEM(((tm // CONV_UNIT) * (CONV_UNIT + 2 * HALO), tn), F32),
                        pltpu.VMEM((tn // LANES, tm, LANES), F32),
                        pltpu.VMEM((tm + 2 * HALO, k), BF16)],
        compiler_params=_cparams(("parallel", "arbitrary")),
    )(a, a, a, w, conv_w, conv_b, *([post] if scaled else []))


def _unit_permute(x):
    rows, d = x.shape
    ug = CONV_UNIT // SUBLANES
    return x.reshape(rows // CONV_UNIT, SUBLANES, ug, d).transpose(0, 2, 1, 3).reshape(rows, d)


def _chunk_order(c, n_lat_chunks, n_ctx_chunks, reverse):
    if reverse:
        return n_lat_chunks + n_ctx_chunks - 1 - c
    return jnp.where(c < n_ctx_chunks, n_lat_chunks + c, c - n_ctx_chunks)


def _tri(chunk, reverse):
    r = lax.broadcasted_iota(jnp.int32, (chunk, chunk), 0)
    c = lax.broadcasted_iota(jnp.int32, (chunk, chunk), 1)
    return (c >= r) if reverse else (c <= r)


def _mlstm_kernel(q_ref, k_ref, v_ref, g_ref, gn_ref, gb_ref, o_ref, ct_s, n_s, m_s, bc_s, bct_s, gt_s,
                  bcn_s, bctn_s, gtn_s, *, reverse):
    L = q_ref.shape[0]
    dk = q_ref.shape[1] // ML_HEADS
    dv = v_ref.shape[1] // ML_HEADS
    c = pl.program_id(0)

    mask = _tri(L, reverse)

    def prefix(g_blk, bc_d, bct_d, gt_d):
        g = g_blk + gb_ref[...]
        bc = _dot_exact(mask.astype(F32), _log_sigmoid(g))
        bc_d[...] = bc
        bct_d[...] = bc.T
        gt_d[...] = g.T

    @pl.when(c == 0)
    def _():
        ct_s[...] = jnp.zeros_like(ct_s)
        n_s[...] = jnp.zeros_like(n_s)
        m_s[...] = jnp.zeros_like(m_s)
        prefix(g_ref[...], bc_s, bct_s, gt_s)

    prefix(gn_ref[...], bcn_s, bctn_s, gtn_s)

    def lanes(x, n):
        return jnp.concatenate([x] * (n // LANES), axis=1)

    bc = bc_s[...]
    bc_t = bct_s[...]
    g_t = gt_s[...]
    end = 0 if reverse else L - 1
    base = 2 * ML_HEADS if reverse else 0
    ones = jnp.ones((L, LANES), BF16)

    for h in range(ML_HEADS):
        io = base + h
        fo = base + ML_HEADS + h
        bb = jnp.broadcast_to(bc[:, fo:fo + 1], (L, LANES))
        b_row = bc_t[fo:fo + 1, :]
        ig_row = g_t[io:io + 1, :]
        b_end = bc[end:end + 1, fo:fo + 1]
        m0 = m_s[h:h + 1, :]
        q = q_ref[:, h * dk:(h + 1) * dk]
        k = k_ref[:, h * dk:(h + 1) * dk]
        v = v_ref[:, h * dv:(h + 1) * dv]
        ct0 = ct_s[h]
        n0 = n_s[h]

        logw = jnp.where(mask, lanes(bb, L) + (ig_row - b_row), NEG_INF)
        m_inter = bb + m0
        m_intra = jnp.broadcast_to(jnp.max(logw, axis=1, keepdims=True), (L, LANES))
        m_out = jnp.maximum(m_inter, m_intra)
        p = (_dot_nt(q, k) * jnp.exp(logw - lanes(m_out, L))).astype(BF16)
        s_inter = jnp.exp(m_inter - m_out)
        num = _dot(p, v) + lanes(s_inter, dv) * _dot(q, ct0.astype(BF16))
        den = _dot(p, ones) + s_inter * _dot(q, n0.astype(BF16))
        inv = 1.0 / jnp.maximum(jnp.abs(den), jnp.exp(-m_out))
        o_ref[:, h * dv:(h + 1) * dv] = (num * lanes(inv, dv)).astype(o_ref.dtype)

        a_row = b_end + ig_row - b_row
        m_loc = jnp.max(a_row, axis=1, keepdims=True)
        kw_t = (k.T.astype(F32) * jnp.exp(a_row - m_loc)).astype(BF16)
        m_new = jnp.maximum(b_end + m0, m_loc)
        s_old = jnp.exp(b_end + m0 - m_new)
        s_new = jnp.exp(m_loc - m_new)
        ct_s[h] = lanes(s_old, dv) * ct0 + lanes(s_new, dv) * _dot(kw_t, v)
        n_s[h] = s_old * n0 + s_new * _dot(kw_t, ones)
        m_s[h:h + 1, :] = m_new

    bc_s[...] = bcn_s[...]
    bct_s[...] = bctn_s[...]
    gt_s[...] = gtn_s[...]


def _mlstm_call(qk, vo, gates, gate_b, n_lat, reverse):
    rows = qk.shape[0]
    L = ML_L
    nqk = qk.shape[1] // 2
    nv = vo.shape[1] // 2
    dk = nqk // ML_HEADS
    dv = nv // ML_HEADS
    nlc = n_lat // L
    ncc = (rows - n_lat) // L
    order = functools.partial(_chunk_order, n_lat_chunks=nlc, n_ctx_chunks=ncc, reverse=reverse)
    return pl.pallas_call(
        functools.partial(_mlstm_kernel, reverse=reverse),
        grid=(nlc + ncc,),
        in_specs=[pl.BlockSpec((L, nqk), lambda c: (order(c), 0)),
                  pl.BlockSpec((L, nqk), lambda c: (order(c), 1)),
                  pl.BlockSpec((L, nv), lambda c: (order(c), 0)),
                  pl.BlockSpec((L, LANES), lambda c: (order(c), 0)),
                  pl.BlockSpec((L, LANES), lambda c: (order(jnp.minimum(c + 1, nlc + ncc - 1)), 0)),
                  pl.BlockSpec((1, LANES), lambda c: (0, 0))],
        out_specs=pl.BlockSpec((L, nv), lambda c: (order(c), 0)),
        out_shape=jax.ShapeDtypeStruct((rows, nv), BF16),
        scratch_shapes=[pltpu.VMEM((ML_HEADS, dk, dv), F32),
                        pltpu.VMEM((ML_HEADS, dk, LANES), F32),
                        pltpu.VMEM((ML_HEADS, LANES), F32)]
        + 2 * [pltpu.VMEM((L, LANES), F32), pltpu.VMEM((LANES, L), F32), pltpu.VMEM((LANES, L), F32)],
        compiler_params=_cparams(("arbitrary",)),
    )(qk, qk, vo, gates, gates, gate_b)


def _ssd_kernel(x_ref, bm_ref, cm_ref, dt_ref, dtn_ref, dtb_ref, a_ref, dsk_ref, y_ref,
                st_s, bc_s, bct_s, dtt_s, bcn_s, bctn_s, dttn_s, *, reverse):
    L = SSM_L
    nsub = x_ref.shape[0] // L
    nx = x_ref.shape[1]
    c = pl.program_id(0)
    mask = _tri(L, reverse)

    def prefix(dt_blk, bc_d, bct_d, dtt_d):
        for s in range(nsub):
            dtv = _softplus(dt_blk[s * L:(s + 1) * L] + dtb_ref[...])
            bc = _dot_exact(mask.astype(F32), dtv * a_ref[...]) * LOG2E
            bc_d[s] = bc
            bct_d[s] = bc.T
            dtt_d[s] = jnp.log2(dtv).T

    @pl.when(c == 0)
    def _():
        st_s[...] = jnp.zeros_like(st_s)
        prefix(dt_ref[...], bc_s, bct_s, dtt_s)

    prefix(dtn_ref[...], bcn_s, bctn_s, dttn_s)

    end = 0 if reverse else L - 1
    nheads = nx // SSM_HEADDIM
    hpg = nheads // SSM_GROUPS
    base = nheads if reverse else 0
    pw = 2 * SSM_HEADDIM
    lo = lax.broadcasted_iota(jnp.int32, (1, pw), 1) < SSM_HEADDIM
    lo_b = jnp.where(lo, 1.0, 0.0).astype(BF16)
    hi_b = jnp.where(lo, 0.0, 1.0).astype(BF16)

    for sub in (range(nsub - 1, -1, -1) if reverse else range(nsub)):
        rows = slice(sub * L, (sub + 1) * L)
        bc = bc_s[sub]
        bc_t = bct_s[sub]
        ldt_t = dtt_s[sub]
        _ssd_chunk(x_ref, bm_ref, cm_ref, dsk_ref, y_ref, st_s, rows, bc, bc_t, ldt_t, mask, lo, lo_b, hi_b,
                   end=end, hpg=hpg, base=base, pw=pw, reverse=reverse)

    bc_s[...] = bcn_s[...]
    bct_s[...] = bctn_s[...]
    dtt_s[...] = dttn_s[...]


def _ssd_chunk(x_ref, bm_ref, cm_ref, dsk_ref, y_ref, st_s, rows, bc, bc_t, ldt_t, mask, lo, lo_b, hi_b,
               *, end, hpg, base, pw, reverse):
    L = SSM_L
    for g in range(SSM_GROUPS):
        bm = bm_ref[rows, g * SSM_DSTATE:(g + 1) * SSM_DSTATE]
        cm = cm_ref[rows, g * SSM_DSTATE:(g + 1) * SSM_DSTATE]
        cb = _dot_nt(cm, bm)
        bm_t = bm.astype(F32).T
        for pr in range(hpg // 2):
            col = (g * hpg + 2 * pr) * SSM_HEADDIM
            xp = x_ref[rows, col:col + pw]
            x_half = (xp * lo_b, xp * hi_b)
            s0 = st_s[:, col:col + pw]
            y = None
            st = None
            bbs, decs = [], []
            for e in range(2):
                co = base + g * hpg + 2 * pr + e
                bb = jnp.broadcast_to(bc[:, co:co + 1], (L, L))
                r_row = bc_t[co:co + 1, :] - ldt_t[co:co + 1, :]
                b_end = bc[end:end + 1, co:co + 1]
                m = (cb * jnp.exp2(jnp.where(mask, bb - r_row, NEG_INF))).astype(BF16)
                ye = _dot(m, x_half[e])
                y = ye if y is None else y + ye
                bt = (bm_t * jnp.exp2(b_end - r_row)).astype(BF16)
                se = _dot(bt, x_half[e])
                st = se if st is None else st + se
                bbs.append(bb[:, 0:pw])
                decs.append(jnp.exp2(b_end))
            y = y + jnp.exp2(jnp.where(lo, bbs[0], bbs[1])) * _dot(cm, s0.astype(BF16))
            if not reverse:
                y = y + dsk_ref[:, col:col + pw] * xp.astype(F32)
            y_ref[rows, col:col + pw] = y.astype(y_ref.dtype)
            st_s[:, col:col + pw] = s0 * jnp.where(lo, decs[0], decs[1]) + st


def _ssd_call(xbc, dt, dt_bias, a_neg, d_skip, n_lat, reverse):
    rows = xbc.shape[0]
    n_sub = SSM_STEP_CHUNKS if n_lat % (SSM_STEP_CHUNKS * SSM_L) == 0 and (rows - n_lat) % (SSM_STEP_CHUNKS * SSM_L) == 0 else 1
    L = n_sub * SSM_L
    assert SSM_L >= 2 * SSM_HEADDIM
    nb = SSM_GROUPS * SSM_DSTATE
    nx = xbc.shape[1] - 2 * nb
    nlc = n_lat // L
    ncc = (rows - n_lat) // L
    order = functools.partial(_chunk_order, n_lat_chunks=nlc, n_ctx_chunks=ncc, reverse=reverse)
    xblk = nx // nb
    nc = nlc + ncc
    return pl.pallas_call(
        functools.partial(_ssd_kernel, reverse=reverse),
        grid=(nc,),
        in_specs=[pl.BlockSpec((L, nx), lambda c: (order(c), 0)),
                  pl.BlockSpec((L, nb), lambda c: (order(c), xblk)),
                  pl.BlockSpec((L, nb), lambda c: (order(c), xblk + 1)),
                  pl.BlockSpec((L, LANES), lambda c: (order(c), 0)),
                  pl.BlockSpec((L, LANES), lambda c: (order(jnp.minimum(c + 1, nc - 1)), 0)),
                  pl.BlockSpec((1, LANES), lambda c: (0, 0)),
                  pl.BlockSpec((1, LANES), lambda c: (0, 0)),
                  pl.BlockSpec((1, nx), lambda c: (0, 0))],
        out_specs=pl.BlockSpec((L, nx), lambda c: (order(c), 0)),
        out_shape=jax.ShapeDtypeStruct((rows, nx), BF16),
        scratch_shapes=[pltpu.VMEM((SSM_DSTATE, nx), F32)]
        + 2 * [pltpu.VMEM((n_sub, SSM_L, LANES), F32), pltpu.VMEM((n_sub, LANES, SSM_L), F32),
               pltpu.VMEM((n_sub, LANES, SSM_L), F32)],
        compiler_params=_cparams(("arbitrary",)),
    )(xbc, xbc, xbc, dt, dt, dt_bias, a_neg, d_skip)


def _mlgate_kernel(hf_ref, hb_ref, o_ref, nw_ref, a_ref):
    dv = a_ref.shape[1] // ML_HEADS

    def body(rows):
        for h in range(ML_HEADS):
            sl = slice(h * dv, (h + 1) * dv)
            hh = _rms(hf_ref[rows, sl].astype(F32) + hb_ref[rows, sl].astype(F32)) * nw_ref[:, sl]
            a_ref[rows, sl] = (_sigmoid(o_ref[rows, sl].astype(F32)) * hh).astype(BF16)
    _row_loop(a_ref.shape[0], ROW_BLK, body)


def _mlgate_call(hf, hb, vo, norm_w, n_lat):
    nv = hf.shape[1]
    tm = _pick(n_lat, (512, 256))
    row = pl.BlockSpec((tm, nv), lambda i: (i, 0))
    return pl.pallas_call(
        _mlgate_kernel,
        grid=(n_lat // tm,),
        in_specs=[row, row, pl.BlockSpec((tm, nv), lambda i: (i, 1)), pl.BlockSpec((1, nv), lambda i: (0, 0))],
        out_specs=row,
        out_shape=jax.ShapeDtypeStruct((n_lat, nv), BF16),
        compiler_params=_cparams(("parallel",)),
    )(hf, hb, vo, norm_w)


def _ssmgate_kernel(yf_ref, yb_ref, z_ref, nw_ref, a_ref):
    gw = a_ref.shape[1] // SSM_GROUPS

    def body(rows):
        for g in range(SSM_GROUPS):
            sl = slice(g * gw, (g + 1) * gw)
            y = (yf_ref[rows, sl].astype(F32) + yb_ref[rows, sl].astype(F32)) * _silu(z_ref[rows, sl].astype(F32))
            a_ref[rows, sl] = (_rms(y) * nw_ref[:, sl]).astype(BF16)
    _row_loop(a_ref.shape[0], ROW_BLK, body)


def _ssmgate_call(yf, yb, z, norm_w, n_lat):
    nx = yf.shape[1]
    tm = _pick(n_lat, (512, 256))
    row = pl.BlockSpec((tm, nx), lambda i: (i, 0))
    return pl.pallas_call(
        _ssmgate_kernel,
        grid=(n_lat // tm,),
        in_specs=[row, row, row, pl.BlockSpec((1, nx), lambda i: (0, 0))],
        out_specs=row,
        out_shape=jax.ShapeDtypeStruct((n_lat, nx), BF16),
        compiler_params=_cparams(("parallel",)),
    )(yf, yb, z, norm_w)


def _proj_kernel(a_ref, w_ref, y_ref):
    y_ref[...] = _dot(a_ref[...], w_ref[...]).astype(y_ref.dtype)


def _proj_call(a, w):
    m, k = a.shape
    n = w.shape[1]
    tm = _pick(m, (1024, 512, 256))
    tn = _pick(n, (1024, 512, 256, 128))
    return pl.pallas_call(
        _proj_kernel,
        grid=(m // tm, n // tn),
        in_specs=[pl.BlockSpec((tm, k), lambda i, j: (i, 0)),
                  pl.BlockSpec((k, tn), lambda i, j: (0, j))],
        out_specs=pl.BlockSpec((tm, tn), lambda i, j: (i, j)),
        out_shape=jax.ShapeDtypeStruct((m, n), BF16),
        compiler_params=_cparams(("parallel", "arbitrary")),
    )(a, w)


def _gate_kernel(h_ref, w1_ref, w2_ref, b1_ref, b2_ref, yml_ref, yssm_ref, o_ref):
    h = h_ref[...]
    g1 = _sigmoid(_dot(h, w1_ref[...]) + b1_ref[...])
    g2 = _sigmoid(_dot(h, w2_ref[...]) + b2_ref[...])
    o_ref[...] = (g1 * yml_ref[...].astype(F32) + g2 * yssm_ref[...].astype(F32)).astype(o_ref.dtype)


def _gate_call(h, w_gate, b_gate, y_ml, y_ssm, n_lat):
    d = h.shape[1]
    tm = _pick(n_lat, (1024, 512, 256))
    tn = _pick(d, (1024, 512, 256, 128))
    nj = d // tn
    tile = pl.BlockSpec((tm, tn), lambda i, j: (i, j))
    return pl.pallas_call(
        _gate_kernel,
        grid=(n_lat // tm, nj),
        in_specs=[pl.BlockSpec((tm, d), lambda i, j: (i, 0)),
                  pl.BlockSpec((d, tn), lambda i, j: (0, j)),
                  pl.BlockSpec((d, tn), lambda i, j: (0, nj + j)),
                  pl.BlockSpec((1, tn), lambda i, j: (0, j)),
                  pl.BlockSpec((1, tn), lambda i, j: (0, nj + j)),
                  tile, tile],
        out_specs=tile,
        out_shape=jax.ShapeDtypeStruct((n_lat, d), BF16),
        compiler_params=_cparams(("parallel", "arbitrary")),
    )(h, w_gate, w_gate, b_gate, b_gate, y_ml, y_ssm)


def _out_kernel(a_ref, w_ref, x_ref, g_ref, o_ref):
    o_ref[...] = x_ref[...] + g_ref[...] * _dot(a_ref[...], w_ref[...])


def _out_call(mix, w_out, x1, gate, n_lat):
    d = mix.shape[1]
    tm = _pick(n_lat, (1024, 512, 256))
    tn = _pick(d, (1024, 512, 256, 128))
    tile = pl.BlockSpec((tm, tn), lambda i, j: (i, j))
    return pl.pallas_call(
        _out_kernel,
        grid=(n_lat // tm, d // tn),
        in_specs=[pl.BlockSpec((tm, d), lambda i, j: (i, 0)),
                  pl.BlockSpec((d, tn), lambda i, j: (0, j)),
                  tile,
                  pl.BlockSpec((1, tn), lambda i, j: (0, j))],
        out_specs=tile,
        out_shape=jax.ShapeDtypeStruct((n_lat, d), F32),
        compiler_params=_cparams(("parallel", "arbitrary")),
    )(mix, w_out, x1, gate)


def _pad_rows(a, rows):
    return jnp.concatenate([a, jnp.zeros((rows - a.shape[0],) + a.shape[1:], a.dtype)], axis=0)


def _pad_cols(a, cols):
    return jnp.concatenate([a, jnp.zeros(a.shape[:-1] + (cols - a.shape[-1],), a.dtype)], axis=-1)


def kernel(x, c, ctx, c_ctx, w_ada, b_ada, norm_w, ffn_w_gate, ffn_w_up, ffn_w_down, w_in, ml_conv_w, ml_conv_b, ml_gate_b, ml_norm_w, w_proj_ml, ssm_conv_w, ssm_conv_b, ssm_dt_bias, ssm_a_log, ssm_d, ssm_norm_w, w_proj_ssm, w_gate, b_gate, w_out, final_norm_w):
    depth = w_ada.shape[0]
    assert depth == 1 and x.shape[0] == 1, "single layer, single batch element"
    t, d = x.shape[1], x.shape[2]
    n_tot = t + ctx.shape[1]
    rows_cm = t // GRID_W
    ml_qk = ml_conv_w.shape[2] // 2
    ml_v = ml_norm_w.shape[1]
    n_ml = 2 * ml_qk + 2 * ml_v
    ml_cols = n_ml + 4 * ML_HEADS
    ssm_inner = ssm_norm_w.shape[1]
    ssm_heads = ssm_d.shape[1]
    ssm_xbc = ssm_conv_w.shape[2]

    cc = _pad_rows(jnp.concatenate([c, c_ctx[None, :]], axis=0), SUBLANES)
    mod = _mod_call(cc, w_ada[0], b_ada[0])
    ml, mc = mod[0].reshape(N_MOD, d), mod[1].reshape(N_MOD, d)
    nw = norm_w[0]
    zero = jnp.zeros_like(nw[0])
    vec1 = _pad_rows(jnp.stack([nw[0], ml[0], ml[1], ml[2], mc[0], mc[1], mc[2],
                                nw[1], ml[3], ml[4], zero, mc[3], mc[4]]), 16)
    vec2 = _pad_rows(jnp.stack([nw[2], ml[6], ml[7], ml[8], zero, zero, zero, final_norm_w]), 16)
    wg, wu, wd = ffn_w_gate, ffn_w_up, ffn_w_down.astype(BF16)

    xall = jnp.concatenate([x[0], ctx[0]], axis=0)
    x1, h = _ffn_call(xall, vec1, wg, wu, wd, 0, t, final=False)

    h_cm = jnp.concatenate(
        [h[:t].reshape(rows_cm, GRID_W, d).transpose(1, 0, 2).reshape(t, d), h[t:]], axis=0)
    wt = w_in[0].T
    n_ssm = ssm_inner + ssm_xbc
    ml_cw = _pad_rows(ml_conv_w[0], SUBLANES)
    ml_post = jnp.concatenate([jnp.full((1, ml_qk), (ml_qk // ML_HEADS) ** -0.5, F32),
                               jnp.ones((1, ml_qk), F32)], axis=1)
    assert 2 * ssm_heads == LANES
    qk = _mmconv_call(_unit_permute(h_cm), wt, 0, ml_cw, ml_conv_b[0][None, :], ml_post, t)
    vo, gates = _mm_call(h_cm, wt, 2 * ml_qk, 2 * ml_v, n_tot, BF16, side_row0=n_ml)
    z, dt = _mm_call(h, wt, ml_cols, ssm_inner, n_tot, BF16, side_row0=ml_cols + n_ssm)
    xbc = _mmconv_call(_unit_permute(h), wt, ml_cols + ssm_inner, _pad_rows(ssm_conv_w[0], SUBLANES),
                       ssm_conv_b[0][None, :], None, t)

    gate_b = _pad_cols(ml_gate_b[0].reshape(1, 4 * ML_HEADS), LANES)
    h_f = _mlstm_call(qk, vo, gates, gate_b, t, reverse=False)
    h_b = _mlstm_call(qk, vo, gates, gate_b, t, reverse=True)
    y_ml = _proj_call(_mlgate_call(h_f, h_b, vo, ml_norm_w[0][None, :], t), w_proj_ml[0].astype(BF16))
    y_ml = y_ml.reshape(GRID_W, rows_cm, d).transpose(1, 0, 2).reshape(t, d)

    dt_bias = ssm_dt_bias[0].reshape(1, 2 * ssm_heads)
    a_neg = -jnp.exp(ssm_a_log[0].astype(F32)).reshape(1, 2 * ssm_heads)
    d_skip = jnp.repeat(ssm_d[0], SSM_HEADDIM)[None, :]
    y_f = _ssd_call(xbc, dt, dt_bias, a_neg, d_skip, t, reverse=False)
    y_b = _ssd_call(xbc, dt, dt_bias, a_neg, d_skip, t, reverse=True)
    y_ssm = _proj_call(_ssmgate_call(y_f, y_b, z, ssm_norm_w[0][None, :], t), w_proj_ssm[0].astype(BF16))

    mix = _gate_call(h, w_gate[0].astype(BF16), b_gate[0][None, :], y_ml, y_ssm, t)
    x2 = _out_call(mix, w_out[0].astype(BF16), x1, ml[5][None, :], t)
    out = _ffn_call(x2, vec2, wg, wu, wd, 1, t, final=True)
    return out[None]
```
